```python
import math
import jax, jax.numpy as jnp
from jax import lax
import numpy as np

D_MODEL = 2048
BATCH = 4
SEQ = 2048
DEPTH = 2

N_MIXERS = 2
N_HEADS = 16
HEAD_DIM = D_MODEL // N_HEADS
DSA_KV_HEADS = 4
DSA_GROUP = N_HEADS // DSA_KV_HEADS
IDX_HEADS = 16
IDX_DIM = 128
TOPK_MAX = 256
Q_BLOCK = 128
D_FF = 5632
CONV_WIDTH = 3
ROPE_THETA = 10000.0
NORM_EPS = 1e-6
N_A_LAYERS = (DEPTH + 1) // 2
N_B_LAYERS = DEPTH // 2

DSA_Q_COLS = N_HEADS * HEAD_DIM
DSA_KV_COLS = DSA_KV_HEADS * HEAD_DIM
DSA_QI_COLS = IDX_HEADS * IDX_DIM
DSA_KI_COLS = IDX_DIM
DSA_WI_COLS = IDX_HEADS
DSA_IN_COLS = DSA_Q_COLS + 2 * DSA_KV_COLS + DSA_QI_COLS + DSA_KI_COLS + DSA_WI_COLS

kernel_name = "hybrid_stickbreaking_dsa_convffn"


def _rmsnorm(x, g):
    xf = x.astype(jnp.float32)
    y = xf * lax.rsqrt(jnp.mean(xf * xf, axis=-1, keepdims=True) + NORM_EPS)
    return (y * g.astype(jnp.float32)).astype(x.dtype)


def _rope_tables(seq_len, dim):
    inv_freq = 1.0 / (ROPE_THETA ** (jnp.arange(0, dim, 2, dtype=jnp.float32) / dim))
    ang = jnp.arange(seq_len, dtype=jnp.float32)[:, None] * inv_freq[None, :]
    return jnp.cos(ang), jnp.sin(ang)


def _rope(x, cos, sin):
    half = x.shape[-1] // 2
    shp = (x.shape[1],) + (1,) * (x.ndim - 3) + (half,)
    c = cos.reshape(shp).astype(x.dtype)
    s = sin.reshape(shp).astype(x.dtype)
    x1, x2 = x[..., :half], x[..., half:]
    return jnp.concatenate([x1 * c - x2 * s, x2 * c + x1 * s], axis=-1)


def _stick_breaking_attention(h, w_qkv, w_o):
    B, S, _ = h.shape
    qkv = (h @ w_qkv).reshape(B, S, 3, N_HEADS, HEAD_DIM)
    q = qkv[:, :, 0].transpose(0, 2, 1, 3)
    k = qkv[:, :, 1].transpose(0, 2, 1, 3)
    v = qkv[:, :, 2].transpose(0, 2, 1, 3)
    nb = S // Q_BLOCK
    q_blocks = q.reshape(B, N_HEADS, nb, Q_BLOCK, HEAD_DIM).transpose(2, 0, 1, 3, 4)
    s_pos = jnp.arange(S)
    scale = 1.0 / math.sqrt(HEAD_DIM)

    def one_block(args):
        qb, bi = args
        t_pos = bi * Q_BLOCK + jnp.arange(Q_BLOCK)
        z = jnp.einsum('bhtd,bhsd->bhts', qb, k).astype(jnp.float32) * scale
        mask = (s_pos[None, :] < t_pos[:, None])[None, None]
        neg_log_1m_beta = jnp.where(mask, jax.nn.softplus(z), 0.0)
        between = lax.cumsum(neg_log_1m_beta, axis=3, reverse=True) - neg_log_1m_beta
        log_a = jax.nn.log_sigmoid(z) - between
        a = jnp.where(mask, jnp.exp(log_a), 0.0)
        return jnp.einsum('bhts,bhsd->bhtd', a.astype(v.dtype), v)

    out = lax.map(one_block, (q_blocks, jnp.arange(nb)))
    out = out.transpose(1, 0, 3, 2, 4).reshape(B, S, N_HEADS * HEAD_DIM)
    return out @ w_o


def _dsa_attention(h, w_in, q_norm_g, k_norm_g, ik_norm_g, w_o):
    B, S, _ = h.shape
    proj = h @ w_in
    o0 = DSA_Q_COLS
    o1 = o0 + DSA_KV_COLS
    o2 = o1 + DSA_KV_COLS
    o3 = o2 + DSA_QI_COLS
    o4 = o3 + DSA_KI_COLS
    q = proj[..., :o0].reshape(B, S, N_HEADS, HEAD_DIM)
    k = proj[..., o0:o1].reshape(B, S, DSA_KV_HEADS, HEAD_DIM)
    v = proj[..., o1:o2].reshape(B, S, DSA_KV_HEADS, HEAD_DIM)
    qi = proj[..., o2:o3].reshape(B, S, IDX_HEADS, IDX_DIM)
    ki = proj[..., o3:o4]
    wi = proj[..., o4:]

    cos, sin = _rope_tables(S, HEAD_DIM)
    q = _rope(_rmsnorm(q, q_norm_g), cos, sin)
    k = _rope(_rmsnorm(k, k_norm_g), cos, sin)
    cos_i, sin_i = _rope_tables(S, IDX_DIM)
    qi = _rope(qi, cos_i, sin_i)
    ki = _rope(_rmsnorm(ki, ik_norm_g), cos_i, sin_i)

    topk = min(TOPK_MAX, S // 4)
    nb = S // Q_BLOCK
    q_b = q.reshape(B, nb, Q_BLOCK, N_HEADS, HEAD_DIM).swapaxes(0, 1)
    qi_b = qi.reshape(B, nb, Q_BLOCK, IDX_HEADS, IDX_DIM).swapaxes(0, 1)
    wi_b = wi.reshape(B, nb, Q_BLOCK, IDX_HEADS).swapaxes(0, 1)
    s_pos = jnp.arange(S)
    b_idx = jnp.arange(B)[:, None, None]
    idx_scale = 1.0 / math.sqrt(IDX_DIM)
    head_w_scale = 1.0 / math.sqrt(IDX_HEADS)
    attn_scale = 1.0 / math.sqrt(HEAD_DIM)

    def one_block(args):
        qb, qib, wib, bi = args
        t_pos = bi * Q_BLOCK + jnp.arange(Q_BLOCK)
        logits = jnp.einsum('bthe,bse->bths', qib, ki).astype(jnp.float32) * idx_scale
        score = jnp.einsum('bths,bth->bts', jax.nn.relu(logits),
                           wib.astype(jnp.float32)) * head_w_scale
        admissible = (s_pos[None, :] <= t_pos[:, None])[None]
        score = jnp.where(admissible, score, -jnp.inf)
        _, sel = lax.top_k(score, topk)
        k_sel = k[b_idx, sel]
        v_sel = v[b_idx, sel]
        valid = sel <= t_pos[None, :, None]
        qg = qb.reshape(B, Q_BLOCK, DSA_KV_HEADS, DSA_GROUP, HEAD_DIM)
        s = jnp.einsum('btgrd,btkgd->btgrk', qg, k_sel).astype(jnp.float32) * attn_scale
        s = jnp.where(valid[:, :, None, None, :], s, -jnp.inf)
        p = jax.nn.softmax(s, axis=-1)
        o = jnp.einsum('btgrk,btkgd->btgrd', p.astype(v.dtype), v_sel)
        return o.reshape(B, Q_BLOCK, N_HEADS * HEAD_DIM)

    out = lax.map(one_block, (q_b, qi_b, wi_b, jnp.arange(nb)))
    out = out.swapaxes(0, 1).reshape(B, S, N_HEADS * HEAD_DIM)
    return out @ w_o


def _conv_ffn(h, w_up, conv_w, conv_b, w_down):
    u = h @ w_up
    S = u.shape[1]
    up = jnp.pad(u, ((0, 0), (CONV_WIDTH - 1, 0), (0, 0)))
    c = conv_b
    for j in range(CONV_WIDTH):
        c = c + up[:, j:j + S] * conv_w[j]
    gate, val = jnp.split(c, 2, axis=-1)
    return (jax.nn.silu(gate) * val) @ w_down


def setup_inputs(seed: int = 0) -> dict:
    key = jax.random.key(seed)
    ks = jax.random.split(key, 16)
    f32 = jnp.float32

    def nrm(k, shape, fan_in):
        return jax.random.normal(k, shape, f32) * (fan_in ** -0.5)

    def gain(k, shape):
        return 1.0 + 0.02 * jax.random.normal(k, shape, f32)

    return {
        "x": jax.random.normal(ks[0], (BATCH, SEQ, D_MODEL), f32),
        "attn_norm_g": gain(ks[1], (DEPTH, D_MODEL)),
        "ffn_norm_g": gain(ks[2], (DEPTH, D_MODEL)),
        "sb_w_qkv": nrm(ks[3], (N_A_LAYERS, D_MODEL, 3 * N_HEADS * HEAD_DIM), D_MODEL),
        "sb_w_o": nrm(ks[4], (N_A_LAYERS, N_HEADS * HEAD_DIM, D_MODEL), N_HEADS * HEAD_DIM),
        "dsa_w_in": nrm(ks[5], (N_B_LAYERS, D_MODEL, DSA_IN_COLS), D_MODEL),
        "dsa_q_norm_g": gain(ks[6], (N_B_LAYERS, HEAD_DIM)),
        "dsa_k_norm_g": gain(ks[7], (N_B_LAYERS, HEAD_DIM)),
        "dsa_ik_norm_g": gain(ks[8], (N_B_LAYERS, IDX_DIM)),
        "dsa_w_o": nrm(ks[9], (N_B_LAYERS, N_HEADS * HEAD_DIM, D_MODEL), N_HEADS * HEAD_DIM),
        "ffn_w_up": nrm(ks[10], (DEPTH, D_MODEL, 2 * D_FF), D_MODEL),
        "ffn_conv_w": nrm(ks[11], (DEPTH, CONV_WIDTH, 2 * D_FF), CONV_WIDTH),
        "ffn_conv_b": 0.01 * jax.random.normal(ks[12], (DEPTH, 2 * D_FF), f32),
        "ffn_w_down": nrm(ks[13], (DEPTH, D_FF, D_MODEL), D_FF),
    }


def reference(x, attn_norm_g, ffn_norm_g, sb_w_qkv, sb_w_o, dsa_w_in, dsa_q_norm_g,
              dsa_k_norm_g, dsa_ik_norm_g, dsa_w_o, ffn_w_up, ffn_conv_w, ffn_conv_b,
              ffn_w_down):
    for i in range(DEPTH):
        h = _rmsnorm(x, attn_norm_g[i])
        j = i // N_MIXERS
        if i % N_MIXERS == 0:
            mix = _stick_breaking_attention(h, sb_w_qkv[j], sb_w_o[j])
        else:
            mix = _dsa_attention(h, dsa_w_in[j], dsa_q_norm_g[j], dsa_k_norm_g[j],
                                 dsa_ik_norm_g[j], dsa_w_o[j])
        x = x + mix
        h = _rmsnorm(x, ffn_norm_g[i])
        x = x + _conv_ffn(h, ffn_w_up[i], ffn_conv_w[i], ffn_conv_b[i], ffn_w_down[i])
    return x
```

```python
import functools
import math

import jax
import jax.numpy as jnp
from jax import lax
from jax.experimental import pallas as pl
from jax.experimental.pallas import tpu as pltpu

N_HEADS = 16
HEAD_DIM = 128
DSA_KV_HEADS = 4
DSA_GROUP = N_HEADS // DSA_KV_HEADS
IDX_HEADS = 16
IDX_DIM = 128
TOPK_MAX = 256
D_FF = 5632
CONV_WIDTH = 3
ROPE_THETA = 10000.0
NORM_EPS = 1e-6

V7X_VMEM_BYTES = 64 * 1024 * 1024
VMEM_LIMIT_BYTES = V7X_VMEM_BYTES - 8 * 1024 * 1024

F32 = jnp.float32
BF16 = jnp.bfloat16
MASK_BIAS = -1e30
F32_LOWEST = float(jnp.finfo(jnp.float32).min)
INT32_MIN = -(2 ** 31)


def _params(*sem):
    return pltpu.CompilerParams(dimension_semantics=sem, vmem_limit_bytes=VMEM_LIMIT_BYTES)


def _rmsnorm_kernel(x_ref, g_ref, o_ref):
    x = x_ref[...]
    ms = jnp.mean(x * x, axis=-1, keepdims=True)
    o_ref[...] = ((x * lax.rsqrt(ms + NORM_EPS)) * g_ref[0]).astype(o_ref.dtype)


def _rmsnorm(x, g_all, layer, *, tm=512):
    m, d = x.shape
    return pl.pallas_call(
        _rmsnorm_kernel,
        grid=(m // tm,),
        in_specs=[
            pl.BlockSpec((tm, d), lambda i: (i, 0)),
            pl.BlockSpec((1, 1, d), lambda i: (layer, 0, 0)),
        ],
        out_specs=pl.BlockSpec((tm, d), lambda i: (i, 0)),
        out_shape=jax.ShapeDtypeStruct((m, d), BF16),
        compiler_params=_params("arbitrary"),
        name="rmsnorm",
    )(x, g_all)


def _matmul_kernel(*refs, nk, has_res):
    if has_res:
        a_ref, w_ref, r_ref, o_ref, *scratch = refs
    else:
        a_ref, w_ref, o_ref, *scratch = refs
        r_ref = None
    part = jnp.dot(a_ref[...], w_ref[...].astype(BF16), preferred_element_type=F32)
    if nk == 1:
        if has_res:
            part = r_ref[...] + part
        o_ref[...] = part.astype(o_ref.dtype)
        return
    (acc_ref,) = scratch
    k = pl.program_id(2)

    @pl.when(k == 0)
    def _():
        acc_ref[...] = part

    @pl.when(k > 0)
    def _():
        acc_ref[...] += part

    @pl.when(k == nk - 1)
    def _():
        out = acc_ref[...]
        if has_res:
            out = r_ref[...] + out
        o_ref[...] = out.astype(o_ref.dtype)


def _matmul(a, w_all, layer, n_out, *, tm, tn, tk, out_dtype, res=None, name):
    m, k_dim = a.shape
    nk = k_dim // tk
    in_specs = [
        pl.BlockSpec((tm, tk), lambda i, j, k: (i, k)),
        pl.BlockSpec((None, tk, tn), lambda i, j, k: (layer, k, j)),
    ]
    args = [a, w_all]
    if res is not None:
        in_specs.append(pl.BlockSpec((tm, tn), lambda i, j, k: (i, j)))
        args.append(res)
    return pl.pallas_call(
        functools.partial(_matmul_kernel, nk=nk, has_res=res is not None),
        grid=(m // tm, n_out // tn, nk),
        in_specs=in_specs,
        out_specs=pl.BlockSpec((tm, tn), lambda i, j, k: (i, j)),
        out_shape=jax.ShapeDtypeStruct((m, n_out), out_dtype),
        scratch_shapes=[pltpu.VMEM((tm, tn), F32)] if nk > 1 else [],
        compiler_params=_params("arbitrary", "arbitrary", "arbitrary"),
        name=name,
    )(*args)


def _softplus_parts(z):
    tail = jnp.log1p(jnp.exp(-jnp.abs(z)))
    return jnp.maximum(z, 0.0) + tail, jnp.minimum(z, 0.0) - tail


def _suffix_sums(sp, upper):
    hi = sp.astype(BF16)
    lo = (sp - hi.astype(F32)).astype(BF16)
    return (jnp.dot(hi, upper, preferred_element_type=F32)
            + jnp.dot(lo, upper, preferred_element_type=F32))


def _sb_attn_kernel(q_ref, k_ref, v_ref, o_ref, *, blk):
    i = pl.program_id(2)
    scale = 1.0 / math.sqrt(HEAD_DIM)
    q = q_ref[0]
    rows = lax.broadcasted_iota(jnp.int32, (blk, blk), 0)
    cols = lax.broadcasted_iota(jnp.int32, (blk, blk), 1)
    upper = jnp.where(rows > cols, 1.0, 0.0).astype(BF16)

    def scores(j):
        start = pl.multiple_of(j * blk, blk)
        k_blk = k_ref[0, pl.ds(start, blk), :]
        v_blk = v_ref[0, pl.ds(start, blk), :]
        z = lax.dot_general(q, k_blk, (((1,), (1,)), ((), ())), preferred_element_type=F32) * scale
        return z, v_blk

    z, v_blk = scores(i)
    causal = cols < rows
    sp, ls = _softplus_parts(z)
    sp = jnp.where(causal, sp, 0.0)
    between = _suffix_sums(sp, upper)
    a = jnp.where(causal, jnp.exp(ls - between), 0.0)
    acc = jnp.dot(a.astype(BF16), v_blk, preferred_element_type=F32)
    carry = between[:, 0:1] + sp[:, 0:1]

    def body(step, state):
        acc, carry = state
        z, v_blk = scores(i - 1 - step)
        sp, ls = _softplus_parts(z)
        between = _suffix_sums(sp, upper)
        a = jnp.exp(ls - (between + carry))
        acc = acc + jnp.dot(a.astype(BF16), v_blk, preferred_element_type=F32)
        carry = carry + (between[:, 0:1] + sp[:, 0:1])
        return acc, carry

    acc, _ = lax.fori_loop(0, i, body, (acc, carry))
    o_ref[0] = acc.astype(o_ref.dtype)


def _sb_attention(qkv, *, blk=256):
    b, s, _ = qkv.shape
    return pl.pallas_call(
        functools.partial(_sb_attn_kernel, blk=blk),
        grid=(b, N_HEADS, s // blk),
        in_specs=[
            pl.BlockSpec((1, blk, HEAD_DIM), lambda bi, h, i: (bi, i, h)),
            pl.BlockSpec((1, s, HEAD_DIM), lambda bi, h, i: (bi, 0, N_HEADS + h)),
            pl.BlockSpec((1, s, HEAD_DIM), lambda bi, h, i: (bi, 0, 2 * N_HEADS + h)),
        ],
        out_specs=pl.BlockSpec((1, blk, HEAD_DIM), lambda bi, h, i: (bi, i, h)),
        out_shape=jax.ShapeDtypeStruct((b, s, N_HEADS * HEAD_DIM), BF16),
        compiler_params=_params("arbitrary", "arbitrary", "arbitrary"),
        name="sb_attention",
    )(qkv, qkv, qkv)


def _conv_silu_gate(u_g, u_v, cw_g, cw_v, cb_g, cb_v, prev_g, prev_v):
    def conv(u, cw, cb, prev):
        full = cb + cw[2:3] * u + cw[1:2] * pltpu.roll(u, 1, 0) + cw[0:1] * pltpu.roll(u, 2, 0)
        r8 = lax.broadcasted_iota(jnp.int32, (8, u.shape[1]), 0)
        top = u[0:8]
        back1 = jnp.where(r8 == 0, prev[7:8], pltpu.roll(top, 1, 0))
        back2 = jnp.where(r8 == 0, prev[6:7], jnp.where(r8 == 1, prev[7:8], pltpu.roll(top, 2, 0)))
        head = cb + cw[2:3] * top + cw[1:2] * back1 + cw[0:1] * back2
        return full, head

    g_full, g_head = conv(u_g, cw_g, cb_g, prev_g)
    v_full, v_head = conv(u_v, cw_v, cb_v, prev_v)
    return jax.nn.silu(g_full) * v_full, jax.nn.silu(g_head) * v_head


def _ffn_up_kernel(h_ref, wg_ref, wv_ref, cwg_ref, cwv_ref, cbg_ref, cbv_ref, o_ref,
                   wg_bf, wv_bf, prev_g, prev_v, *, tiles_per_seq):
    i = pl.program_id(1)

    @pl.when(i == 0)
    def _():
        wg_bf[...] = wg_ref[...].astype(BF16)
        wv_bf[...] = wv_ref[...].astype(BF16)

    @pl.when(i % tiles_per_seq == 0)
    def _():
        prev_g[...] = jnp.zeros_like(prev_g)
        prev_v[...] = jnp.zeros_like(prev_v)

    h = h_ref[...]
    u_g = jnp.dot(h, wg_bf[...], preferred_element_type=F32)
    u_v = jnp.dot(h, wv_bf[...], preferred_element_type=F32)
    full, head = _conv_silu_gate(u_g, u_v, cwg_ref[...], cwv_ref[...], cbg_ref[0], cbv_ref[0],
                                 prev_g[...], prev_v[...])
    tm = u_g.shape[0]
    prev_g[...] = u_g[tm - 8:tm]
    prev_v[...] = u_v[tm - 8:tm]
    o_ref[...] = full.astype(o_ref.dtype)
    o_ref[0:8, :] = head.astype(o_ref.dtype)


def _ffn_up(h, w_up_all, conv_w_all, conv_b_all, layer, seq_len, *, tm=512, tn=512):
    m, d = h.shape
    nj = D_FF // tn
    return pl.pallas_call(
        functools.partial(_ffn_up_kernel, tiles_per_seq=seq_len // tm),
        grid=(nj, m // tm),
        in_specs=[
            pl.BlockSpec((tm, d), lambda j, i: (i, 0)),
            pl.BlockSpec((None, d, tn), lambda j, i: (layer, 0, j)),
            pl.BlockSpec((None, d, tn), lambda j, i: (layer, 0, nj + j)),
            pl.BlockSpec((None, CONV_WIDTH, tn), lambda j, i: (layer, 0, j)),
            pl.BlockSpec((None, CONV_WIDTH, tn), lambda j, i: (layer, 0, nj + j)),
            pl.BlockSpec((1, 1, tn), lambda j, i: (layer, 0, j)),
            pl.BlockSpec((1, 1, tn), lambda j, i: (layer, 0, nj + j)),
        ],
        out_specs=pl.BlockSpec((tm, tn), lambda j, i: (i, j)),
        out_shape=jax.ShapeDtypeStruct((m, D_FF), BF16),
        scratch_shapes=[
            pltpu.VMEM((d, tn), BF16),
            pltpu.VMEM((d, tn), BF16),
            pltpu.VMEM((8, tn), F32),
            pltpu.VMEM((8, tn), F32),
        ],
        compiler_params=_params("arbitrary", "arbitrary"),
        name="ffn_up_conv_gate",
    )(h, w_up_all, w_up_all, conv_w_all, conv_w_all, conv_b_all, conv_b_all)


def _head_norm_rope(x, gain, cos, sin_signed):
    if gain is not None:
        ms = jnp.mean(x * x, axis=-1, keepdims=True)
        x = (x * lax.rsqrt(ms + NORM_EPS)) * gain
    return x * cos + pltpu.roll(x, HEAD_DIM // 2, 1) * sin_signed


def _dsa_prep_kernel(p_ref, t_ref, cos_ref, sin_ref, qg_ref, kg_ref, ig_ref,
                     q_ref, k_ref, v_ref, qi_ref, ki_ref):
    cos = cos_ref[...]
    sin = sin_ref[...]
    o_k = N_HEADS * HEAD_DIM
    o_v = o_k + DSA_KV_HEADS * HEAD_DIM
    o_qi = o_v + DSA_KV_HEADS * HEAD_DIM
    for h in range(N_HEADS):
        sl = slice(h * HEAD_DIM, (h + 1) * HEAD_DIM)
        q_ref[:, sl] = _head_norm_rope(p_ref[:, sl], qg_ref[...], cos, sin).astype(BF16)
        qi_ref[:, sl] = _head_norm_rope(
            p_ref[:, o_qi + h * IDX_DIM:o_qi + (h + 1) * IDX_DIM], None, cos, sin).astype(BF16)
    for g in range(DSA_KV_HEADS):
        sl = slice(g * HEAD_DIM, (g + 1) * HEAD_DIM)
        k_ref[:, sl] = _head_norm_rope(
            p_ref[:, o_k + g * HEAD_DIM:o_k + (g + 1) * HEAD_DIM], kg_ref[...], cos, sin).astype(BF16)
    v_ref[...] = p_ref[:, o_v:o_qi].astype(BF16)
    ki_ref[...] = _head_norm_rope(t_ref[:, 0:IDX_DIM], ig_ref[...], cos, sin).astype(BF16)


def _dsa_prep(proj, tail, cos, sin_signed, q_gain, k_gain, ik_gain, layer, seq_len, *, tm=256):
    m = proj.shape[0]
    n_main = proj.shape[1]
    n_tail = tail.shape[1]
    tiles_per_seq = seq_len // tm
    hq = N_HEADS * HEAD_DIM
    hk = DSA_KV_HEADS * HEAD_DIM
    row = lambda i: (i, 0)
    gain_spec = pl.BlockSpec((None, 1, HEAD_DIM), lambda i: (layer, 0, 0))
    tab_spec = pl.BlockSpec((tm, HEAD_DIM), lambda i: (i % tiles_per_seq, 0))
    return pl.pallas_call(
        _dsa_prep_kernel,
        grid=(m // tm,),
        in_specs=[
            pl.BlockSpec((tm, n_main), row),
            pl.BlockSpec((tm, n_tail), row),
            tab_spec, tab_spec, gain_spec, gain_spec, gain_spec,
        ],
        out_specs=[
            pl.BlockSpec((tm, hq), row),
            pl.BlockSpec((tm, hk), row),
            pl.BlockSpec((tm, hk), row),
            pl.BlockSpec((tm, IDX_HEADS * IDX_DIM), row),
            pl.BlockSpec((tm, IDX_DIM), row),
        ],
        out_shape=[
            jax.ShapeDtypeStruct((m, hq), BF16),
            jax.ShapeDtypeStruct((m, hk), BF16),
            jax.ShapeDtypeStruct((m, hk), BF16),
            jax.ShapeDtypeStruct((m, IDX_HEADS * IDX_DIM), BF16),
            jax.ShapeDtypeStruct((m, IDX_DIM), BF16),
        ],
        compiler_params=_params("arbitrary"),
        name="dsa_norm_rope",
    )(proj, tail, cos, sin_signed, q_gain, k_gain, ik_gain)


def _key_to_f32(key):
    bits = jnp.where(key < 0, key ^ INT32_MIN, ~key)
    return lax.bitcast_convert_type(bits, F32)


def _indexer_kernel(qi_ref, ki_ref, t_ref, bias_ref, sc_ref, *, tq, tkc, nkc, topk):
    i = pl.program_id(1)
    t0 = i * tq
    idx_scale = 1.0 / math.sqrt(IDX_DIM)
    head_w_scale = 1.0 / math.sqrt(IDX_HEADS)
    rows = lax.broadcasted_iota(jnp.int32, (tq, tkc), 0) + t0
    cols = lax.broadcasted_iota(jnp.int32, (tq, tkc), 1)

    for c in range(nkc):
        lanes = slice(c * tkc, (c + 1) * tkc)

        @pl.when(c * tkc < t0 + tq)
        def _():
            k_blk = ki_ref[lanes, :]
            acc = jnp.zeros((tq, tkc), F32)
            for h in range(IDX_HEADS):
                logit = lax.dot_general(
                    qi_ref[:, h * IDX_DIM:(h + 1) * IDX_DIM], k_blk, (((1,), (1,)), ((), ())),
                    preferred_element_type=F32) * idx_scale
                acc = acc + jnp.maximum(logit, 0.0) * t_ref[:, IDX_DIM + h:IDX_DIM + h + 1]
            score = acc * head_w_scale
            sc_ref[:, lanes] = jnp.where(cols + c * tkc <= rows, score, -jnp.inf)

        @pl.when(c * tkc >= t0 + tq)
        def _():
            sc_ref[:, lanes] = jnp.full((tq, tkc), -jnp.inf, F32)

    def count_ge(thr):
        return jnp.sum(jnp.where(sc_ref[...] >= thr, 1.0, 0.0), axis=1, keepdims=True)

    def write_bias(select_fn):
        for c in range(nkc):
            lanes = slice(c * tkc, (c + 1) * tkc)
            bias_ref[0, 0, c] = jnp.where(select_fn(sc_ref[:, lanes], c), 0.0, MASK_BIAS)

    @pl.when(t0 + tq <= topk)
    def _():
        write_bias(lambda sc, c: sc >= F32_LOWEST)

    @pl.when(t0 + tq > topk)
    def _():
        def bit_step(step, key):
            trial = key | lax.shift_left(jnp.int32(1), 31 - step)
            keep = count_ge(_key_to_f32(trial)) >= topk
            return jnp.where(keep, trial, key)

        key = lax.fori_loop(0, 32, bit_step, jnp.zeros((tq, 1), jnp.int32))
        thr = _key_to_f32(key)
        write_bias(lambda sc, c: sc >= thr)

        n_ge = count_ge(thr)

        @pl.when(jnp.max(n_ge) > topk)
        def _():
            n_gt = jnp.sum(jnp.where(sc_ref[...] > thr, 1.0, 0.0), axis=1, keepdims=True)
            room = topk - n_gt
            r2 = lax.broadcasted_iota(jnp.int32, (tkc, tkc), 0)
            c2 = lax.broadcasted_iota(jnp.int32, (tkc, tkc), 1)
            before = jnp.where(r2 < c2, 1.0, 0.0).astype(BF16)
            seen = jnp.zeros((tq, 1), F32)
            for c in range(nkc):
                lanes = slice(c * tkc, (c + 1) * tkc)
                sc = sc_ref[:, lanes]
                tied = jnp.where(sc == thr, 1.0, 0.0)
                rank = seen + jnp.dot(tied.astype(BF16), before, preferred_element_type=F32)
                take = jnp.where(sc > thr, 1.0, jnp.where(rank < room, tied, 0.0))
                bias_ref[0, 0, c] = jnp.where(take > 0.0, 0.0, MASK_BIAS)
                seen = seen + jnp.sum(tied, axis=1, keepdims=True)


def _dsa_indexer(qi, ki, tail, batch, seq_len, topk, *, tq=128, tkc=512):
    nq = seq_len // tq
    nkc = seq_len // tkc
    return pl.pallas_call(
        functools.partial(_indexer_kernel, tq=tq, tkc=tkc, nkc=nkc, topk=topk),
        grid=(batch, nq),
        in_specs=[
            pl.BlockSpec((tq, IDX_HEADS * IDX_DIM), lambda b, i: (b * nq + i, 0)),
            pl.BlockSpec((seq_len, IDX_DIM), lambda b, i: (b, 0)),
            pl.BlockSpec((tq, tail.shape[1]), lambda b, i: (b * nq + i, 0)),
        ],
        out_specs=pl.BlockSpec((1, 1, nkc, tq, tkc), lambda b, i: (b, i, 0, 0, 0)),
        out_shape=jax.ShapeDtypeStruct((batch, nq, nkc, tq, tkc), F32),
        scratch_shapes=[pltpu.VMEM((tq, seq_len), F32)],
        compiler_params=_params("arbitrary", "arbitrary"),
        name="dsa_indexer_topk",
    )(qi, ki, tail)


def _dsa_attn_kernel(q_ref, k_ref, v_ref, bias_ref, o_ref, *, tq, tkc):
    i = pl.program_id(1)
    scale = 1.0 / math.sqrt(HEAD_DIM)
    n_chunks = ((i + 1) * tq + tkc - 1) // tkc
    rows = DSA_GROUP * tq
    q = jnp.concatenate([q_ref[:, r * HEAD_DIM:(r + 1) * HEAD_DIM] for r in range(DSA_GROUP)], axis=0)

    def body(c, state):
        m_run, l_run, acc = state
        start = pl.multiple_of(c * tkc, tkc)
        k_blk = k_ref[pl.ds(start, tkc), :]
        v_blk = v_ref[pl.ds(start, tkc), :]
        s = lax.dot_general(q, k_blk, (((1,), (1,)), ((), ())), preferred_element_type=F32) * scale
        s = (s.reshape(DSA_GROUP, tq, tkc) + bias_ref[0, 0, c][None]).reshape(rows, tkc)
        m_new = jnp.maximum(m_run, jnp.max(s, axis=1, keepdims=True))
        alpha = jnp.exp(m_run - m_new)
        p = jnp.exp(s - m_new)
        l_run = alpha * l_run + jnp.sum(p, axis=1, keepdims=True)
        acc = alpha * acc + jnp.dot(p.astype(BF16), v_blk, preferred_element_type=F32)
        return m_new, l_run, acc

    init = (jnp.full((rows, 1), -jnp.inf, F32), jnp.zeros((rows, 1), F32),
            jnp.zeros((rows, HEAD_DIM), F32))
    _, l_run, acc = lax.fori_loop(0, n_chunks, body, init)
    out = acc / l_run
    for r in range(DSA_GROUP):
        o_ref[:, r * HEAD_DIM:(r + 1) * HEAD_DIM] = out[r * tq:(r + 1) * tq].astype(o_ref.dtype)


def _dsa_attention(q, k, v, bias, batch, seq_len, *, tq=128, tkc=512):
    nq = seq_len // tq
    gw = DSA_GROUP * HEAD_DIM
    return pl.pallas_call(
        functools.partial(_dsa_attn_kernel, tq=tq, tkc=tkc),
        grid=(batch, nq, DSA_KV_HEADS),
        in_specs=[
            pl.BlockSpec((tq, gw), lambda b, i, g: (b * nq + i, g)),
            pl.BlockSpec((seq_len, HEAD_DIM), lambda b, i, g: (b, g)),
            pl.BlockSpec((seq_len, HEAD_DIM), lambda b, i, g: (b, g)),
            pl.BlockSpec((1, 1, seq_len // tkc, tq, tkc), lambda b, i, g: (b, i, 0, 0, 0)),
        ],
        out_specs=pl.BlockSpec((tq, gw), lambda b, i, g: (b * nq + i, g)),
        out_shape=jax.ShapeDtypeStruct((batch * seq_len, N_HEADS * HEAD_DIM), BF16),
        compiler_params=_params("arbitrary", "arbitrary", "arbitrary"),
        name="dsa_attention",
    )(q, k, v, bias)


def _rope_tables(seq_len, dim):
    inv_freq = 1.0 / (ROPE_THETA ** (jnp.arange(0, dim, 2, dtype=F32) / dim))
    ang = jnp.arange(seq_len, dtype=F32)[:, None] * inv_freq[None, :]
    cos, sin = jnp.cos(ang), jnp.sin(ang)
    return jnp.concatenate([cos, cos], axis=-1), jnp.concatenate([-sin, sin], axis=-1)


def _conv_ffn(x, norm_g, w_up, conv_w, conv_b, w_down, layer, seq_len):
    h = _rmsnorm(x, norm_g, layer)
    gated = _ffn_up(h, w_up, conv_w, conv_b, layer, seq_len)
    return _matmul(gated, w_down, layer, x.shape[1], tm=1024, tn=512, tk=D_FF // 2,
                   out_dtype=F32, res=x, name="ffn_down_residual")


def kernel(x, attn_norm_g, ffn_norm_g, sb_w_qkv, sb_w_o, dsa_w_in, dsa_q_norm_g, dsa_k_norm_g,
           dsa_ik_norm_g, dsa_w_o, ffn_w_up, ffn_conv_w, ffn_conv_b, ffn_w_down):
    batch, seq_len, d_model = x.shape
    m = batch * seq_len
    hd = N_HEADS * HEAD_DIM
    x = x.reshape(m, d_model)
    attn_g = attn_norm_g[:, None, :]
    ffn_g = ffn_norm_g[:, None, :]
    conv_b = ffn_conv_b[:, None, :]

    h = _rmsnorm(x, attn_g, 0)
    qkv = _matmul(h, sb_w_qkv, 0, 3 * hd, tm=1024, tn=512, tk=d_model, out_dtype=BF16, name="sb_qkv_proj")
    mixed = _sb_attention(qkv.reshape(batch, seq_len, 3 * hd)).reshape(m, hd)
    x = _matmul(mixed, sb_w_o, 0, d_model, tm=1024, tn=512, tk=hd, out_dtype=F32, res=x,
                name="sb_out_residual")
    x = _conv_ffn(x, ffn_g, ffn_w_up, ffn_conv_w, conv_b, ffn_w_down, 0, seq_len)

    n_main = 2 * hd + 2 * DSA_KV_HEADS * HEAD_DIM
    h = _rmsnorm(x, attn_g, 1)
    proj = _matmul(h, dsa_w_in, 0, n_main, tm=1024, tn=512, tk=d_model, out_dtype=F32, name="dsa_in_proj")
    w_tail = dsa_w_in[:, :, n_main:]
    tail = _matmul(h, w_tail, 0, w_tail.shape[2], tm=1024, tn=w_tail.shape[2], tk=d_model,
                   out_dtype=F32, name="dsa_in_proj_tail")
    cos, sin_signed = _rope_tables(seq_len, HEAD_DIM)
    q, k, v, qi, ki = _dsa_prep(proj, tail, cos, sin_signed, dsa_q_norm_g[:, None, :],
                                dsa_k_norm_g[:, None, :], dsa_ik_norm_g[:, None, :], 0, seq_len)
    topk = min(TOPK_MAX, seq_len // 4)
    bias = _dsa_indexer(qi, ki, tail, batch, seq_len, topk)
    mixed = _dsa_attention(q, k, v, bias, batch, seq_len)
    x = _matmul(mixed, dsa_w_o, 0, d_model, tm=1024, tn=512, tk=hd, out_dtype=F32, res=x,
                name="dsa_out_residual")
    x = _conv_ffn(x, ffn_g, ffn_w_up, ffn_conv_w, conv_b, ffn_w_down, 1, seq_len)
    return x.reshape(batch, seq_len, d_model)
```

```python
import functools
import math

import jax
import jax.numpy as jnp
from jax import lax
from jax.experimental import pallas as pl
from jax.experimental.pallas import tpu as pltpu

N_HEADS = 16
HEAD_DIM = 128
DSA_KV_HEADS = 4
DSA_GROUP = N_HEADS // DSA_KV_HEADS
IDX_HEADS = 16
IDX_DIM = 128
TOPK_MAX = 256
D_FF = 5632
CONV_WIDTH = 3
ROPE_THETA = 10000.0
NORM_EPS = 1e-6

V7X_VMEM_BYTES = 64 * 1024 * 1024
VMEM_LIMIT_BYTES = V7X_VMEM_BYTES - 8 * 1024 * 1024

F32 = jnp.float32
BF16 = jnp.bfloat16
MASK_BIAS = -1e30
F32_LOWEST = float(jnp.finfo(jnp.float32).min)
INT32_MIN = -(2 ** 31)
LOG2E = math.log2(math.e)


def _params(*sem, flags=None):
    return pltpu.CompilerParams(dimension_semantics=sem, vmem_limit_bytes=VMEM_LIMIT_BYTES, flags=flags)


def _rmsnorm_kernel(x_ref, g_ref, o_ref):
    x = x_ref[...]
    ms = jnp.mean(x * x, axis=-1, keepdims=True)
    o_ref[...] = ((x * lax.rsqrt(ms + NORM_EPS)) * g_ref[0]).astype(o_ref.dtype)


def _rmsnorm(x, g_all, layer, *, tm=512):
    m, d = x.shape
    return pl.pallas_call(
        _rmsnorm_kernel,
        grid=(m // tm,),
        in_specs=[
            pl.BlockSpec((tm, d), lambda i: (i, 0)),
            pl.BlockSpec((1, 1, d), lambda i: (layer, 0, 0)),
        ],
        out_specs=pl.BlockSpec((tm, d), lambda i: (i, 0)),
        out_shape=jax.ShapeDtypeStruct((m, d), BF16),
        compiler_params=_params("arbitrary"),
        name="rmsnorm",
    )(x, g_all)


def _matmul_kernel(*refs, nk, has_res):
    if has_res:
        a_ref, w_ref, r_ref, o_ref, *scratch = refs
    else:
        a_ref, w_ref, o_ref, *scratch = refs
        r_ref = None
    part = jnp.dot(a_ref[...], w_ref[...].astype(BF16), preferred_element_type=F32)
    if nk == 1:
        if has_res:
            part = r_ref[...] + part
        o_ref[...] = part.astype(o_ref.dtype)
        return
    (acc_ref,) = scratch
    k = pl.program_id(2)

    @pl.when(k == 0)
    def _():
        acc_ref[...] = part

    @pl.when(k > 0)
    def _():
        acc_ref[...] += part

    @pl.when(k == nk - 1)
    def _():
        out = acc_ref[...]
        if has_res:
            out = r_ref[...] + out
        o_ref[...] = out.astype(o_ref.dtype)


def _matmul(a, w_all, layer, n_out, *, tm, tn, tk, out_dtype, res=None, name):
    m, k_dim = a.shape
    nk = k_dim // tk
    in_specs = [
        pl.BlockSpec((tm, tk), lambda i, j, k: (i, k)),
        pl.BlockSpec((None, tk, tn), lambda i, j, k: (layer, k, j)),
    ]
    args = [a, w_all]
    if res is not None:
        in_specs.append(pl.BlockSpec((tm, tn), lambda i, j, k: (i, j)))
        args.append(res)
    return pl.pallas_call(
        functools.partial(_matmul_kernel, nk=nk, has_res=res is not None),
        grid=(m // tm, n_out // tn, nk),
        in_specs=in_specs,
        out_specs=pl.BlockSpec((tm, tn), lambda i, j, k: (i, j)),
        out_shape=jax.ShapeDtypeStruct((m, n_out), out_dtype),
        scratch_shapes=[pltpu.VMEM((tm, tn), F32)] if nk > 1 else [],
        compiler_params=_params("arbitrary", "arbitrary", "arbitrary"),
        name=name,
    )(*args)


def _sb_tile(q, k_blk, v_blk, upper, carry, causal):
    scale = 1.0 / math.sqrt(HEAD_DIM)
    raw = lax.dot_general(q, k_blk, (((1,), (1,)), ((), ())), preferred_element_type=F32)
    z = raw * scale
    sp = jnp.maximum(z, 0.0) + jnp.log(1.0 + jnp.exp2(jnp.abs(raw) * (-scale * LOG2E)))
    if causal is not None:
        sp = jnp.where(causal, sp, 0.0)
    tail = jnp.dot(sp.astype(BF16), upper, preferred_element_type=F32)
    a = jnp.exp(z - (tail + carry))
    if causal is not None:
        a = jnp.where(causal, a, 0.0)
    out = jnp.dot(a.astype(BF16), v_blk, preferred_element_type=F32)
    return out, carry + jnp.sum(sp, axis=1, keepdims=True)


def _sb_attn_kernel(q_ref, k_ref, v_ref, o_ref, *, tq, tk, heads):
    i = pl.program_id(2)
    per_q = tq // tk
    r2 = lax.broadcasted_iota(jnp.int32, (tk, tk), 0)
    c2 = lax.broadcasted_iota(jnp.int32, (tk, tk), 1)
    upper = jnp.where(r2 >= c2, 1.0, 0.0).astype(BF16)
    rows = lax.broadcasted_iota(jnp.int32, (tq, tk), 0)
    cols = lax.broadcasted_iota(jnp.int32, (tq, tk), 1)
    lanes = [slice(h * HEAD_DIM, (h + 1) * HEAD_DIM) for h in range(heads)]

    def block(j, carries, causal):
        start = pl.multiple_of(j * tk, tk)
        outs = [_sb_tile(q_ref[0, :, lanes[h]], k_ref[0, pl.ds(start, tk), lanes[h]],
                         v_ref[0, pl.ds(start, tk), lanes[h]], upper, carries[h], causal)
                for h in range(heads)]
        return [o for o, _ in outs], [c for _, c in outs]

    carries = [jnp.zeros((tq, 1), F32)] * heads
    accs = [jnp.zeros((tq, HEAD_DIM), F32)] * heads
    for d in reversed(range(per_q)):
        outs, carries = block(i * per_q + d, carries, cols + d * tk < rows)
        accs = [acc + o for acc, o in zip(accs, outs)]

    def body(step, state):
        accs, carries = state
        for d in range(per_q):
            outs, carries = block((i - step) * per_q - 1 - d, carries, None)
            accs = [acc + o for acc, o in zip(accs, outs)]
        return accs, carries

    accs, _ = lax.fori_loop(0, i, body, (accs, carries))
    for h in range(heads):
        o_ref[0, :, lanes[h]] = accs[h].astype(o_ref.dtype)


def _sb_attention(qkv, *, tq=512, tk=256, heads=4):
    b, s, _ = qkv.shape
    groups = N_HEADS // heads
    width = heads * HEAD_DIM
    return pl.pallas_call(
        functools.partial(_sb_attn_kernel, tq=tq, tk=tk, heads=heads),
        grid=(b, groups, s // tq),
        in_specs=[
            pl.BlockSpec((1, tq, width), lambda bi, g, i: (bi, i, g)),
            pl.BlockSpec((1, s, width), lambda bi, g, i: (bi, 0, groups + g)),
            pl.BlockSpec((1, s, width), lambda bi, g, i: (bi, 0, 2 * groups + g)),
        ],
        out_specs=pl.BlockSpec((1, tq, width), lambda bi, g, i: (bi, i, g)),
        out_shape=jax.ShapeDtypeStruct((b, s, N_HEADS * HEAD_DIM), BF16),
        compiler_params=_params("arbitrary", "arbitrary", "arbitrary"),
        name="sb_attention",
    )(qkv, qkv, qkv)


def _conv_silu_gate(u_g, u_v, cw_g, cw_v, cb_g, cb_v, prev_g, prev_v):
    def conv(u, cw, cb, prev):
        full = cb + cw[2:3] * u + cw[1:2] * pltpu.roll(u, 1, 0) + cw[0:1] * pltpu.roll(u, 2, 0)
        r8 = lax.broadcasted_iota(jnp.int32, (8, u.shape[1]), 0)
        top = u[0:8]
        back1 = jnp.where(r8 == 0, prev[7:8], pltpu.roll(top, 1, 0))
        back2 = jnp.where(r8 == 0, prev[6:7], jnp.where(r8 == 1, prev[7:8], pltpu.roll(top, 2, 0)))
        head = cb + cw[2:3] * top + cw[1:2] * back1 + cw[0:1] * back2
        return full, head

    g_full, g_head = conv(u_g, cw_g, cb_g, prev_g)
    v_full, v_head = conv(u_v, cw_v, cb_v, prev_v)
    return jax.nn.silu(g_full) * v_full, jax.nn.silu(g_head) * v_head


def _ffn_up_kernel(h_ref, wg_ref, wv_ref, cwg_ref, cwv_ref, cbg_ref, cbv_ref, o_ref,
                   wg_bf, wv_bf, prev_g, prev_v, *, tiles_per_seq):
    i = pl.program_id(1)

    @pl.when(i == 0)
    def _():
        wg_bf[...] = wg_ref[...].astype(BF16)
        wv_bf[...] = wv_ref[...].astype(BF16)

    @pl.when(i % tiles_per_seq == 0)
    def _():
        prev_g[...] = jnp.zeros_like(prev_g)
        prev_v[...] = jnp.zeros_like(prev_v)

    h = h_ref[...]
    u_g = jnp.dot(h, wg_bf[...], preferred_element_type=F32)
    u_v = jnp.dot(h, wv_bf[...], preferred_element_type=F32)
    full, head = _conv_silu_gate(u_g, u_v, cwg_ref[...], cwv_ref[...], cbg_ref[0], cbv_ref[0],
                                 prev_g[...], prev_v[...])
    tm = u_g.shape[0]
    prev_g[...] = u_g[tm - 8:tm]
    prev_v[...] = u_v[tm - 8:tm]
    o_ref[...] = full.astype(o_ref.dtype)
    o_ref[0:8, :] = head.astype(o_ref.dtype)


def _ffn_up(h, w_up_all, conv_w_all, conv_b_all, layer, seq_len, *, tm=512, tn=512):
    m, d = h.shape
    nj = D_FF // tn
    return pl.pallas_call(
        functools.partial(_ffn_up_kernel, tiles_per_seq=seq_len // tm),
        grid=(nj, m // tm),
        in_specs=[
            pl.BlockSpec((tm, d), lambda j, i: (i, 0)),
            pl.BlockSpec((None, d, tn), lambda j, i: (layer, 0, j)),
            pl.BlockSpec((None, d, tn), lambda j, i: (layer, 0, nj + j)),
            pl.BlockSpec((None, CONV_WIDTH, tn), lambda j, i: (layer, 0, j)),
            pl.BlockSpec((None, CONV_WIDTH, tn), lambda j, i: (layer, 0, nj + j)),
            pl.BlockSpec((1, 1, tn), lambda j, i: (layer, 0, j)),
            pl.BlockSpec((1, 1, tn), lambda j, i: (layer, 0, nj + j)),
        ],
        out_specs=pl.BlockSpec((tm, tn), lambda j, i: (i, j)),
        out_shape=jax.ShapeDtypeStruct((m, D_FF), BF16),
        scratch_shapes=[
            pltpu.VMEM((d, tn), BF16),
            pltpu.VMEM((d, tn), BF16),
            pltpu.VMEM((8, tn), F32),
            pltpu.VMEM((8, tn), F32),
        ],
        compiler_params=_params("arbitrary", "arbitrary"),
        name="ffn_up_conv_gate",
    )(h, w_up_all, w_up_all, conv_w_all, conv_w_all, conv_b_all, conv_b_all)


def _head_norm_rope(x, gain, cos, sin_signed):
    if gain is not None:
        ms = jnp.mean(x * x, axis=-1, keepdims=True)
        x = (x * lax.rsqrt(ms + NORM_EPS)) * gain
    return x * cos + pltpu.roll(x, HEAD_DIM // 2, 1) * sin_signed


def _dsa_prep_kernel(p_ref, t_ref, cos_ref, sin_ref, qg_ref, kg_ref, ig_ref,
                     q_ref, k_ref, v_ref, qi_ref, ki_ref):
    cos = cos_ref[...]
    sin = sin_ref[...]
    o_k = N_HEADS * HEAD_DIM
    o_v = o_k + DSA_KV_HEADS * HEAD_DIM
    o_qi = o_v + DSA_KV_HEADS * HEAD_DIM
    for h in range(N_HEADS):
        sl = slice(h * HEAD_DIM, (h + 1) * HEAD_DIM)
        q_ref[:, sl] = _head_norm_rope(p_ref[:, sl], qg_ref[...], cos, sin).astype(BF16)
        qi_ref[:, sl] = _head_norm_rope(
            p_ref[:, o_qi + h * IDX_DIM:o_qi + (h + 1) * IDX_DIM], None, cos, sin).astype(BF16)
    for g in range(DSA_KV_HEADS):
        sl = slice(g * HEAD_DIM, (g + 1) * HEAD_DIM)
        k_ref[:, sl] = _head_norm_rope(
            p_ref[:, o_k + g * HEAD_DIM:o_k + (g + 1) * HEAD_DIM], kg_ref[...], cos, sin).astype(BF16)
    v_ref[...] = p_ref[:, o_v:o_qi].astype(BF16)
    ki_ref[...] = _head_norm_rope(t_ref[:, 0:IDX_DIM], ig_ref[...], cos, sin).astype(BF16)


def _dsa_prep(proj, tail, cos, sin_signed, q_gain, k_gain, ik_gain, layer, seq_len, *, tm=256):
    m = proj.shape[0]
    n_main = proj.shape[1]
    n_tail = tail.shape[1]
    tiles_per_seq = seq_len // tm
    hq = N_HEADS * HEAD_DIM
    hk = DSA_KV_HEADS * HEAD_DIM
    row = lambda i: (i, 0)
    gain_spec = pl.BlockSpec((None, 1, HEAD_DIM), lambda i: (layer, 0, 0))
    tab_spec = pl.BlockSpec((tm, HEAD_DIM), lambda i: (i % tiles_per_seq, 0))
    return pl.pallas_call(
        _dsa_prep_kernel,
        grid=(m // tm,),
        in_specs=[
            pl.BlockSpec((tm, n_main), row),
            pl.BlockSpec((tm, n_tail), row),
            tab_spec, tab_spec, gain_spec, gain_spec, gain_spec,
        ],
        out_specs=[
            pl.BlockSpec((tm, hq), row),
            pl.BlockSpec((tm, hk), row),
            pl.BlockSpec((tm, hk), row),
            pl.BlockSpec((tm, IDX_HEADS * IDX_DIM), row),
            pl.BlockSpec((tm, IDX_DIM), row),
        ],
        out_shape=[
            jax.ShapeDtypeStruct((m, hq), BF16),
            jax.ShapeDtypeStruct((m, hk), BF16),
            jax.ShapeDtypeStruct((m, hk), BF16),
            jax.ShapeDtypeStruct((m, IDX_HEADS * IDX_DIM), BF16),
            jax.ShapeDtypeStruct((m, IDX_DIM), BF16),
        ],
        compiler_params=_params("arbitrary"),
        name="dsa_norm_rope",
    )(proj, tail, cos, sin_signed, q_gain, k_gain, ik_gain)


def _key_to_f32(key):
    bits = jnp.where(key < 0, key ^ INT32_MIN, ~key)
    return lax.bitcast_convert_type(bits, F32)


def _indexer_kernel(qi_ref, ki_ref, t_ref, bias_ref, sc_ref, *, tq, tkc, nkc, topk):
    i = pl.program_id(1)
    t0 = i * tq
    idx_scale = 1.0 / math.sqrt(IDX_DIM)
    head_w_scale = 1.0 / math.sqrt(IDX_HEADS)
    rows = lax.broadcasted_iota(jnp.int32, (tq, tkc), 0) + t0
    cols = lax.broadcasted_iota(jnp.int32, (tq, tkc), 1)

    for c in range(nkc):
        lanes = slice(c * tkc, (c + 1) * tkc)

        @pl.when(c * tkc < t0 + tq)
        def _():
            k_blk = ki_ref[lanes, :]
            acc = jnp.zeros((tq, tkc), F32)
            for h in range(IDX_HEADS):
                logit = lax.dot_general(
                    qi_ref[:, h * IDX_DIM:(h + 1) * IDX_DIM], k_blk, (((1,), (1,)), ((), ())),
                    preferred_element_type=F32) * idx_scale
                acc = acc + jnp.maximum(logit, 0.0) * t_ref[:, IDX_DIM + h:IDX_DIM + h + 1]
            score = acc * head_w_scale
            sc_ref[:, lanes] = jnp.where(cols + c * tkc <= rows, score, -jnp.inf)

        @pl.when(c * tkc >= t0 + tq)
        def _():
            sc_ref[:, lanes] = jnp.full((tq, tkc), -jnp.inf, F32)

    def count_ge(thr):
        return jnp.sum(jnp.where(sc_ref[...] >= thr, 1.0, 0.0), axis=1, keepdims=True)

    def write_bias(select_fn):
        for c in range(nkc):
            lanes = slice(c * tkc, (c + 1) * tkc)
            bias_ref[0, 0, c] = jnp.where(select_fn(sc_ref[:, lanes], c), 0.0, MASK_BIAS)

    @pl.when(t0 + tq <= topk)
    def _():
        write_bias(lambda sc, c: sc >= F32_LOWEST)

    @pl.when(t0 + tq > topk)
    def _():
        def bit_step(step, key):
            trial = key | lax.shift_left(jnp.int32(1), 31 - step)
            keep = count_ge(_key_to_f32(trial)) >= topk
            return jnp.where(keep, trial, key)

        key = lax.fori_loop(0, 32, bit_step, jnp.zeros((tq, 1), jnp.int32))
        thr = _key_to_f32(key)
        write_bias(lambda sc, c: sc >= thr)

        n_ge = count_ge(thr)

        @pl.when(jnp.max(n_ge) > topk)
        def _():
            n_gt = jnp.sum(jnp.where(sc_ref[...] > thr, 1.0, 0.0), axis=1, keepdims=True)
            room = topk - n_gt
            r2 = lax.broadcasted_iota(jnp.int32, (tkc, tkc), 0)
            c2 = lax.broadcasted_iota(jnp.int32, (tkc, tkc), 1)
            before = jnp.where(r2 < c2, 1.0, 0.0).astype(BF16)
            seen = jnp.zeros((tq, 1), F32)
            for c in range(nkc):
                lanes = slice(c * tkc, (c + 1) * tkc)
                sc = sc_ref[:, lanes]
                tied = jnp.where(sc == thr, 1.0, 0.0)
                rank = seen + jnp.dot(tied.astype(BF16), before, preferred_element_type=F32)
                take = jnp.where(sc > thr, 1.0, jnp.where(rank < room, tied, 0.0))
                bias_ref[0, 0, c] = jnp.where(take > 0.0, 0.0, MASK_BIAS)
                seen = seen + jnp.sum(tied, axis=1, keepdims=True)


def _dsa_indexer(qi, ki, tail, batch, seq_len, topk, *, tq=128, tkc=512):
    nq = seq_len // tq
    nkc = seq_len // tkc
    return pl.pallas_call(
        functools.partial(_indexer_kernel, tq=tq, tkc=tkc, nkc=nkc, topk=topk),
        grid=(batch, nq),
        in_specs=[
            pl.BlockSpec((tq, IDX_HEADS * IDX_DIM), lambda b, i: (b * nq + i, 0)),
            pl.BlockSpec((seq_len, IDX_DIM), lambda b, i: (b, 0)),
            pl.BlockSpec((tq, tail.shape[1]), lambda b, i: (b * nq + i, 0)),
        ],
        out_specs=pl.BlockSpec((1, 1, nkc, tq, tkc), lambda b, i: (b, i, 0, 0, 0)),
        out_shape=jax.ShapeDtypeStruct((batch, nq, nkc, tq, tkc), F32),
        scratch_shapes=[pltpu.VMEM((tq, seq_len), F32)],
        compiler_params=_params("arbitrary", "arbitrary"),
        name="dsa_indexer_topk",
    )(qi, ki, tail)


def _dsa_attn_kernel(q_ref, k_ref, v_ref, bias_ref, o_ref, *, tq, tkc, groups):
    i = pl.program_id(1)
    scale = 1.0 / math.sqrt(HEAD_DIM)
    n_chunks = ((i + 1) * tq + tkc - 1) // tkc
    rows = DSA_GROUP * tq

    def head_lanes(g, r):
        h = g * DSA_GROUP + r
        return slice(h * HEAD_DIM, (h + 1) * HEAD_DIM)

    qs = [jnp.concatenate([q_ref[:, head_lanes(g, r)] for r in range(DSA_GROUP)], axis=0)
          for g in range(groups)]

    def chunk(g, c, state):
        m_run, l_run, acc = state
        start = pl.multiple_of(c * tkc, tkc)
        kv_lanes = slice(g * HEAD_DIM, (g + 1) * HEAD_DIM)
        k_blk = k_ref[pl.ds(start, tkc), kv_lanes]
        v_blk = v_ref[pl.ds(start, tkc), kv_lanes]
        s = lax.dot_general(qs[g], k_blk, (((1,), (1,)), ((), ())), preferred_element_type=F32) * scale
        s = (s.reshape(DSA_GROUP, tq, tkc) + bias_ref[0, 0, c][None]).reshape(rows, tkc)
        m_new = jnp.maximum(m_run, jnp.max(s, axis=1, keepdims=True))
        alpha = jnp.exp(m_run - m_new)
        p = jnp.exp(s - m_new)
        l_run = alpha * l_run + jnp.sum(p, axis=1, keepdims=True)
        acc = alpha * acc + jnp.dot(p.astype(BF16), v_blk, preferred_element_type=F32)
        return m_new, l_run, acc

    def body(c, states):
        return [chunk(g, c, states[g]) for g in range(groups)]

    init = (jnp.full((rows, 1), -jnp.inf, F32), jnp.zeros((rows, 1), F32),
            jnp.zeros((rows, HEAD_DIM), F32))
    states = lax.fori_loop(0, n_chunks, body, [init] * groups)
    for g in range(groups):
        _, l_run, acc = states[g]
        out = acc / l_run
        for r in range(DSA_GROUP):
            o_ref[:, head_lanes(g, r)] = out[r * tq:(r + 1) * tq].astype(o_ref.dtype)


def _dsa_attention(q, k, v, bias, batch, seq_len, *, tq=128, tkc=512, groups=2):
    nq = seq_len // tq
    gw = groups * DSA_GROUP * HEAD_DIM
    kw = groups * HEAD_DIM
    return pl.pallas_call(
        functools.partial(_dsa_attn_kernel, tq=tq, tkc=tkc, groups=groups),
        grid=(batch, nq, DSA_KV_HEADS // groups),
        in_specs=[
            pl.BlockSpec((tq, gw), lambda b, i, g: (b * nq + i, g)),
            pl.BlockSpec((seq_len, kw), lambda b, i, g: (b, g)),
            pl.BlockSpec((seq_len, kw), lambda b, i, g: (b, g)),
            pl.BlockSpec((1, 1, seq_len // tkc, tq, tkc), lambda b, i, g: (b, i, 0, 0, 0)),
        ],
        out_specs=pl.BlockSpec((tq, gw), lambda b, i, g: (b * nq + i, g)),
        out_shape=jax.ShapeDtypeStruct((batch * seq_len, N_HEADS * HEAD_DIM), BF16),
        compiler_params=_params("arbitrary", "arbitrary", "arbitrary"),
        name="dsa_attention",
    )(q, k, v, bias)


def _rope_tables(seq_len, dim):
    inv_freq = 1.0 / (ROPE_THETA ** (jnp.arange(0, dim, 2, dtype=F32) / dim))
    ang = jnp.arange(seq_len, dtype=F32)[:, None] * inv_freq[None, :]
    cos, sin = jnp.cos(ang), jnp.sin(ang)
    return jnp.concatenate([cos, cos], axis=-1), jnp.concatenate([-sin, sin], axis=-1)


def _conv_ffn(x, norm_g, w_up, conv_w, conv_b, w_down, layer, seq_len):
    h = _rmsnorm(x, norm_g, layer)
    gated = _ffn_up(h, w_up, conv_w, conv_b, layer, seq_len)
    return _matmul(gated, w_down, layer, x.shape[1], tm=1024, tn=512, tk=D_FF // 2,
                   out_dtype=F32, res=x, name="ffn_down_residual")


def kernel(x, attn_norm_g, ffn_norm_g, sb_w_qkv, sb_w_o, dsa_w_in, dsa_q_norm_g, dsa_k_norm_g,
           dsa_ik_norm_g, dsa_w_o, ffn_w_up, ffn_conv_w, ffn_conv_b, ffn_w_down):
    batch, seq_len, d_model = x.shape
    m = batch * seq_len
    hd = N_HEADS * HEAD_DIM
    x = x.reshape(m, d_model)
    attn_g = attn_norm_g[:, None, :]
    ffn_g = ffn_norm_g[:, None, :]
    conv_b = ffn_conv_b[:, None, :]

    h = _rmsnorm(x, attn_g, 0)
    qkv = _matmul(h, sb_w_qkv, 0, 3 * hd, tm=1024, tn=512, tk=d_model, out_dtype=BF16, name="sb_qkv_proj")
    mixed = _sb_attention(qkv.reshape(batch, seq_len, 3 * hd)).reshape(m, hd)
    x = _matmul(mixed, sb_w_o, 0, d_model, tm=1024, tn=512, tk=hd, out_dtype=F32, res=x,
                name="sb_out_residual")
    x = _conv_ffn(x, ffn_g, ffn_w_up, ffn_conv_w, conv_b, ffn_w_down, 0, seq_len)

    n_main = 2 * hd + 2 * DSA_KV_HEADS * HEAD_DIM
    h = _rmsnorm(x, attn_g, 1)
    proj = _matmul(h, dsa_w_in, 0, n_main, tm=1024, tn=512, tk=d_model, out_dtype=F32, name="dsa_in_proj")
    w_tail = dsa_w_in[:, :, n_main:]
    tail = _matmul(h, w_tail, 0, w_tail.shape[2], tm=1024, tn=w_tail.shape[2], tk=d_model,
                   out_dtype=F32, name="dsa_in_proj_tail")
    cos, sin_signed = _rope_tables(seq_len, HEAD_DIM)
    q, k, v, qi, ki = _dsa_prep(proj, tail, cos, sin_signed, dsa_q_norm_g[:, None, :],
                                dsa_k_norm_g[:, None, :], dsa_ik_norm_g[:, None, :], 0, seq_len)
    topk = min(TOPK_MAX, seq_len // 4)
    bias = _dsa_indexer(qi, ki, tail, batch, seq_len, topk)
    mixed = _dsa_attention(q, k, v, bias, batch, seq_len)
    x = _matmul(mixed, dsa_w_o, 0, d_model, tm=1024, tn=512, tk=hd, out_dtype=F32, res=x,
                name="dsa_out_residual")
    x = _conv_ffn(x, ffn_g, ffn_w_up, ffn_conv_w, conv_b, ffn_w_down, 1, seq_len)
    return x.reshape(batch, seq_len, d_model)
```

```python
import functools
import math

import jax
import jax.numpy as jnp
from jax import lax
from jax.experimental import pallas as pl
from jax.experimental.pallas import tpu as pltpu

N_HEADS = 16
HEAD_DIM = 128
DSA_KV_HEADS = 4
DSA_GROUP = N_HEADS // DSA_KV_HEADS
IDX_HEADS = 16
IDX_DIM = 128
TOPK_MAX = 256
D_FF = 5632
CONV_WIDTH = 3
ROPE_THETA = 10000.0
NORM_EPS = 1e-6

V7X_VMEM_BYTES = 64 * 1024 * 1024
VMEM_LIMIT_BYTES = V7X_VMEM_BYTES - 8 * 1024 * 1024

F32 = jnp.float32
BF16 = jnp.bfloat16
MASK_BIAS = -1e30
F32_LOWEST = float(jnp.finfo(jnp.float32).min)
INT32_MIN = -(2 ** 31)
LOG2E = math.log2(math.e)


def _params(*sem, flags=None):
    return pltpu.CompilerParams(dimension_semantics=sem, vmem_limit_bytes=VMEM_LIMIT_BYTES, flags=flags)


def _rmsnorm_kernel(x_ref, g_ref, o_ref):
    x = x_ref[...]
    ms = jnp.mean(x * x, axis=-1, keepdims=True)
    o_ref[...] = ((x * lax.rsqrt(ms + NORM_EPS)) * g_ref[0]).astype(o_ref.dtype)


def _rmsnorm(x, g_all, layer, *, tm=512):
    m, d = x.shape
    return pl.pallas_call(
        _rmsnorm_kernel,
        grid=(m // tm,),
        in_specs=[
            pl.BlockSpec((tm, d), lambda i: (i, 0)),
            pl.BlockSpec((1, 1, d), lambda i: (layer, 0, 0)),
        ],
        out_specs=pl.BlockSpec((tm, d), lambda i: (i, 0)),
        out_shape=jax.ShapeDtypeStruct((m, d), BF16),
        compiler_params=_params("arbitrary"),
        name="rmsnorm",
    )(x, g_all)


def _matmul_kernel(*refs, has_res, cast_w):
    a_ref, w_ref, *rest = refs
    r_ref = rest.pop(0) if has_res else None
    o_ref = rest.pop(0)
    if cast_w:
        (w_bf,) = rest

        @pl.when(pl.program_id(1) == 0)
        def _():
            w_bf[...] = w_ref[...].astype(BF16)

        w = w_bf[...]
    else:
        w = w_ref[...]
    out = jnp.dot(a_ref[...], w, preferred_element_type=F32)
    if has_res:
        out = r_ref[...] + out
    o_ref[...] = out.astype(o_ref.dtype)


def _matmul(a, w_all, layer, n_out, *, tm, tn, out_dtype, res=None, name):
    m, k_dim = a.shape
    cast_w = w_all.dtype != BF16
    in_specs = [
        pl.BlockSpec((tm, k_dim), lambda j, i: (i, 0)),
        pl.BlockSpec((None, k_dim, tn), lambda j, i: (layer, 0, j)),
    ]
    args = [a, w_all]
    if res is not None:
        in_specs.append(pl.BlockSpec((tm, tn), lambda j, i: (i, j)))
        args.append(res)
    return pl.pallas_call(
        functools.partial(_matmul_kernel, has_res=res is not None, cast_w=cast_w),
        grid=(n_out // tn, m // tm),
        in_specs=in_specs,
        out_specs=pl.BlockSpec((tm, tn), lambda j, i: (i, j)),
        out_shape=jax.ShapeDtypeStruct((m, n_out), out_dtype),
        scratch_shapes=[pltpu.VMEM((k_dim, tn), BF16)] if cast_w else [],
        compiler_params=_params("arbitrary", "arbitrary"),
        name=name,
    )(*args)


def _cast_kernel(x_ref, o_ref):
    o_ref[...] = x_ref[...].astype(o_ref.dtype)


def _to_bf16(w_all, *, rows=512):
    n_layers, k_dim, n = w_all.shape
    return pl.pallas_call(
        _cast_kernel,
        grid=(n_layers, k_dim // rows),
        in_specs=[pl.BlockSpec((1, rows, n), lambda l, r: (l, r, 0))],
        out_specs=pl.BlockSpec((1, rows, n), lambda l, r: (l, r, 0)),
        out_shape=jax.ShapeDtypeStruct(w_all.shape, BF16),
        compiler_params=_params("arbitrary", "arbitrary"),
        name="weights_to_bf16",
    )(w_all)


def _sb_tile(q, k_blk, v_blk, upper, carry, causal):
    scale = 1.0 / math.sqrt(HEAD_DIM)
    raw = lax.dot_general(q, k_blk, (((1,), (1,)), ((), ())), preferred_element_type=F32)
    z = raw * scale
    sp = jnp.maximum(z, 0.0) + jnp.log(1.0 + jnp.exp2(jnp.abs(raw) * (-scale * LOG2E)))
    if causal is not None:
        sp = jnp.where(causal, sp, 0.0)
    tail = jnp.dot(sp.astype(BF16), upper, preferred_element_type=F32)
    a = jnp.exp(z - (tail + carry))
    if causal is not None:
        a = jnp.where(causal, a, 0.0)
    out = jnp.dot(a.astype(BF16), v_blk, preferred_element_type=F32)
    return out, carry + jnp.sum(sp, axis=1, keepdims=True)


def _sb_attn_kernel(q_ref, k_ref, v_ref, o_ref, *, tq, tk, heads):
    i = pl.program_id(2)
    per_q = tq // tk
    r2 = lax.broadcasted_iota(jnp.int32, (tk, tk), 0)
    c2 = lax.broadcasted_iota(jnp.int32, (tk, tk), 1)
    upper = jnp.where(r2 >= c2, 1.0, 0.0).astype(BF16)
    rows = lax.broadcasted_iota(jnp.int32, (tq, tk), 0)
    cols = lax.broadcasted_iota(jnp.int32, (tq, tk), 1)
    lanes = [slice(h * HEAD_DIM, (h + 1) * HEAD_DIM) for h in range(heads)]

    def block(j, carries, causal):
        start = pl.multiple_of(j * tk, tk)
        outs = [_sb_tile(q_ref[0, :, lanes[h]], k_ref[0, pl.ds(start, tk), lanes[h]],
                         v_ref[0, pl.ds(start, tk), lanes[h]], upper, carries[h], causal)
                for h in range(heads)]
        return [o for o, _ in outs], [c for _, c in outs]

    carries = [jnp.zeros((tq, 1), F32)] * heads
    accs = [jnp.zeros((tq, HEAD_DIM), F32)] * heads
    for d in reversed(range(per_q)):
        outs, carries = block(i * per_q + d, carries, cols + d * tk < rows)
        accs = [acc + o for acc, o in zip(accs, outs)]

    def body(step, state):
        accs, carries = state
        for d in range(per_q):
            outs, carries = block((i - step) * per_q - 1 - d, carries, None)
            accs = [acc + o for acc, o in zip(accs, outs)]
        return accs, carries

    accs, _ = lax.fori_loop(0, i, body, (accs, carries))
    for h in range(heads):
        o_ref[0, :, lanes[h]] = accs[h].astype(o_ref.dtype)


def _sb_attention(qkv, *, tq=512, tk=256, heads=4):
    b, s, _ = qkv.shape
    groups = N_HEADS // heads
    width = heads * HEAD_DIM
    return pl.pallas_call(
        functools.partial(_sb_attn_kernel, tq=tq, tk=tk, heads=heads),
        grid=(b, groups, s // tq),
        in_specs=[
            pl.BlockSpec((1, tq, width), lambda bi, g, i: (bi, i, g)),
            pl.BlockSpec((1, s, width), lambda bi, g, i: (bi, 0, groups + g)),
            pl.BlockSpec((1, s, width), lambda bi, g, i: (bi, 0, 2 * groups + g)),
        ],
        out_specs=pl.BlockSpec((1, tq, width), lambda bi, g, i: (bi, i, g)),
        out_shape=jax.ShapeDtypeStruct((b, s, N_HEADS * HEAD_DIM), BF16),
        compiler_params=_params("arbitrary", "arbitrary", "arbitrary"),
        name="sb_attention",
    )(qkv, qkv, qkv)


HISTORY_ROWS = 8


def _causal_conv(u, hist, cw, cb):
    r0 = HISTORY_ROWS
    rows = u.shape[0]
    ext = jnp.concatenate([hist, u], axis=0)
    return cb + cw[2:3] * u + cw[1:2] * ext[r0 - 1:r0 - 1 + rows] + cw[0:1] * ext[r0 - 2:r0 - 2 + rows]


def _ffn_up_kernel(h_ref, wg_ref, wv_ref, cwg_ref, cwv_ref, cbg_ref, cbv_ref, o_ref,
                   wg_bf, wv_bf, hist_g, hist_v, *, tiles_per_seq, chunk_rows, chunk_cols):
    i = pl.program_id(1)
    tm, tn = o_ref.shape
    r0 = HISTORY_ROWS

    @pl.when(i == 0)
    def _():
        wg_bf[...] = wg_ref[...].astype(BF16)
        wv_bf[...] = wv_ref[...].astype(BF16)

    @pl.when(i % tiles_per_seq == 0)
    def _():
        hist_g[...] = jnp.zeros_like(hist_g)
        hist_v[...] = jnp.zeros_like(hist_v)

    for c0 in range(0, tn, chunk_cols):
        cols = slice(c0, c0 + chunk_cols)
        prev_g = hist_g[:, cols]
        prev_v = hist_v[:, cols]
        for t0 in range(0, tm, chunk_rows):
            h = h_ref[t0:t0 + chunk_rows, :]
            u_g = jnp.dot(h, wg_bf[:, cols], preferred_element_type=F32)
            u_v = jnp.dot(h, wv_bf[:, cols], preferred_element_type=F32)
            gate = _causal_conv(u_g, prev_g, cwg_ref[:, cols], cbg_ref[0, :, cols])
            val = _causal_conv(u_v, prev_v, cwv_ref[:, cols], cbv_ref[0, :, cols])
            o_ref[t0:t0 + chunk_rows, cols] = (jax.nn.silu(gate) * val).astype(o_ref.dtype)
            prev_g = u_g[chunk_rows - r0:]
            prev_v = u_v[chunk_rows - r0:]
        hist_g[:, cols] = prev_g
        hist_v[:, cols] = prev_v


def _ffn_up(h, w_up_all, conv_w_all, conv_b_all, layer, seq_len, *, tm=1024, tn=512,
            chunk_rows=512, chunk_cols=256):
    m, d = h.shape
    nj = D_FF // tn
    return pl.pallas_call(
        functools.partial(_ffn_up_kernel, tiles_per_seq=seq_len // tm, chunk_rows=chunk_rows,
                          chunk_cols=chunk_cols),
        grid=(nj, m // tm),
        in_specs=[
            pl.BlockSpec((tm, d), lambda j, i: (i, 0)),
            pl.BlockSpec((None, d, tn), lambda j, i: (layer, 0, j)),
            pl.BlockSpec((None, d, tn), lambda j, i: (layer, 0, nj + j)),
            pl.BlockSpec((None, CONV_WIDTH, tn), lambda j, i: (layer, 0, j)),
            pl.BlockSpec((None, CONV_WIDTH, tn), lambda j, i: (layer, 0, nj + j)),
            pl.BlockSpec((1, 1, tn), lambda j, i: (layer, 0, j)),
            pl.BlockSpec((1, 1, tn), lambda j, i: (layer, 0, nj + j)),
        ],
        out_specs=pl.BlockSpec((tm, tn), lambda j, i: (i, j)),
        out_shape=jax.ShapeDtypeStruct((m, D_FF), BF16),
        scratch_shapes=[
            pltpu.VMEM((d, tn), BF16),
            pltpu.VMEM((d, tn), BF16),
            pltpu.VMEM((HISTORY_ROWS, tn), F32),
            pltpu.VMEM((HISTORY_ROWS, tn), F32),
        ],
        compiler_params=_params("arbitrary", "arbitrary"),
        name="ffn_up_conv_gate",
    )(h, w_up_all, w_up_all, conv_w_all, conv_w_all, conv_b_all, conv_b_all)


def _head_norm_rope(x, gain, cos, sin_signed):
    if gain is not None:
        ms = jnp.mean(x * x, axis=-1, keepdims=True)
        x = (x * lax.rsqrt(ms + NORM_EPS)) * gain
    return x * cos + pltpu.roll(x, HEAD_DIM // 2, 1) * sin_signed


def _dsa_prep_kernel(p_ref, t_ref, cos_ref, sin_ref, qg_ref, kg_ref, ig_ref,
                     q_ref, k_ref, v_ref, qi_ref, ki_ref):
    cos = cos_ref[...]
    sin = sin_ref[...]
    o_k = N_HEADS * HEAD_DIM
    o_v = o_k + DSA_KV_HEADS * HEAD_DIM
    o_qi = o_v + DSA_KV_HEADS * HEAD_DIM
    for h in range(N_HEADS):
        sl = slice(h * HEAD_DIM, (h + 1) * HEAD_DIM)
        q_ref[:, sl] = _head_norm_rope(p_ref[:, sl], qg_ref[...], cos, sin).astype(BF16)
        qi_ref[:, sl] = _head_norm_rope(
            p_ref[:, o_qi + h * IDX_DIM:o_qi + (h + 1) * IDX_DIM], None, cos, sin).astype(BF16)
    for g in range(DSA_KV_HEADS):
        sl = slice(g * HEAD_DIM, (g + 1) * HEAD_DIM)
        k_ref[:, sl] = _head_norm_rope(
            p_ref[:, o_k + g * HEAD_DIM:o_k + (g + 1) * HEAD_DIM], kg_ref[...], cos, sin).astype(BF16)
    v_ref[...] = p_ref[:, o_v:o_qi].astype(BF16)
    ki_ref[...] = _head_norm_rope(t_ref[:, 0:IDX_DIM], ig_ref[...], cos, sin).astype(BF16)


def _dsa_prep(proj, tail, cos, sin_signed, q_gain, k_gain, ik_gain, layer, seq_len, *, tm=256):
    m = proj.shape[0]
    n_main = proj.shape[1]
    n_tail = tail.shape[1]
    tiles_per_seq = seq_len // tm
    hq = N_HEADS * HEAD_DIM
    hk = DSA_KV_HEADS * HEAD_DIM
    row = lambda i: (i, 0)
    gain_spec = pl.BlockSpec((None, 1, HEAD_DIM), lambda i: (layer, 0, 0))
    tab_spec = pl.BlockSpec((tm, HEAD_DIM), lambda i: (i % tiles_per_seq, 0))
    return pl.pallas_call(
        _dsa_prep_kernel,
        grid=(m // tm,),
        in_specs=[
            pl.BlockSpec((tm, n_main), row),
            pl.BlockSpec((tm, n_tail), row),
            tab_spec, tab_spec, gain_spec, gain_spec, gain_spec,
        ],
        out_specs=[
            pl.BlockSpec((tm, hq), row),
            pl.BlockSpec((tm, hk), row),
            pl.BlockSpec((tm, hk), row),
            pl.BlockSpec((tm, IDX_HEADS * IDX_DIM), row),
            pl.BlockSpec((tm, IDX_DIM), row),
        ],
        out_shape=[
            jax.ShapeDtypeStruct((m, hq), BF16),
            jax.ShapeDtypeStruct((m, hk), BF16),
            jax.ShapeDtypeStruct((m, hk), BF16),
            jax.ShapeDtypeStruct((m, IDX_HEADS * IDX_DIM), BF16),
            jax.ShapeDtypeStruct((m, IDX_DIM), BF16),
        ],
        compiler_params=_params("arbitrary"),
        name="dsa_norm_rope",
    )(proj, tail, cos, sin_signed, q_gain, k_gain, ik_gain)


def _key_to_f32(key):
    bits = jnp.where(key < 0, key ^ INT32_MIN, ~key)
    return lax.bitcast_convert_type(bits, F32)


def _indexer_kernel(qi_ref, ki_ref, t_ref, bias_ref, sc_ref, cnt_ref, *, tq, tkc, nkc, topk):
    i = pl.program_id(1)
    t0 = i * tq
    w = t_ref[:, IDX_DIM:IDX_DIM + IDX_HEADS] * (1.0 / math.sqrt(IDX_DIM) / math.sqrt(IDX_HEADS))
    rows = lax.broadcasted_iota(jnp.int32, (tq, tkc), 0) + t0
    cols = lax.broadcasted_iota(jnp.int32, (tq, tkc), 1)

    def chunk_in_range(c):
        return c * tkc < t0 + tq

    for c in range(nkc):

        @pl.when(chunk_in_range(c))
        def _():
            k_blk = ki_ref[c * tkc:(c + 1) * tkc, :]
            acc = jnp.zeros((tq, tkc), F32)
            for h in range(IDX_HEADS):
                logit = lax.dot_general(
                    qi_ref[:, h * IDX_DIM:(h + 1) * IDX_DIM], k_blk, (((1,), (1,)), ((), ())),
                    preferred_element_type=F32)
                acc = acc + jnp.maximum(logit, 0.0) * w[:, h:h + 1]
            sc_ref[c] = jnp.where(cols + c * tkc <= rows, acc, -jnp.inf)

        @pl.when(jnp.logical_not(chunk_in_range(c)))
        def _():
            sc_ref[c] = jnp.full((tq, tkc), -jnp.inf, F32)

    def lane_partial(hit):
        return functools.reduce(lambda a, b: a + b, [hit[:, l:l + 128] for l in range(0, tkc, 128)])

    def count_ge(thr):
        cnt_ref[...] = lane_partial(jnp.where(sc_ref[0] >= thr, 1.0, 0.0))
        for c in range(1, nkc):

            @pl.when(chunk_in_range(c))
            def _():
                cnt_ref[...] += lane_partial(jnp.where(sc_ref[c] >= thr, 1.0, 0.0))

        return jnp.sum(cnt_ref[...], axis=1, keepdims=True)

    def write_bias(select_fn):
        for c in range(nkc):
            bias_ref[0, 0, c] = jnp.where(select_fn(sc_ref[c], c), 0.0, MASK_BIAS)

    @pl.when(t0 + tq <= topk)
    def _():
        write_bias(lambda sc, c: sc >= F32_LOWEST)

    @pl.when(t0 + tq > topk)
    def _():
        def bit_step(step, key):
            trial = key | lax.shift_left(jnp.int32(1), 31 - step)
            keep = count_ge(_key_to_f32(trial)) >= topk
            return jnp.where(keep, trial, key)

        key = lax.fori_loop(0, 32, bit_step, jnp.zeros((tq, 1), jnp.int32))
        thr = _key_to_f32(key)
        write_bias(lambda sc, c: sc >= thr)

        n_ge = count_ge(thr)

        @pl.when(jnp.max(n_ge) > topk)
        def _():
            n_gt = sum(jnp.sum(jnp.where(sc_ref[c] > thr, 1.0, 0.0), axis=1, keepdims=True)
                       for c in range(nkc))
            room = topk - n_gt
            r2 = lax.broadcasted_iota(jnp.int32, (tkc, tkc), 0)
            c2 = lax.broadcasted_iota(jnp.int32, (tkc, tkc), 1)
            before = jnp.where(r2 < c2, 1.0, 0.0).astype(BF16)
            seen = jnp.zeros((tq, 1), F32)
            for c in range(nkc):
                sc = sc_ref[c]
                tied = jnp.where(sc == thr, 1.0, 0.0)
                rank = seen + jnp.dot(tied.astype(BF16), before, preferred_element_type=F32)
                take = jnp.where(sc > thr, 1.0, jnp.where(rank < room, tied, 0.0))
                bias_ref[0, 0, c] = jnp.where(take > 0.0, 0.0, MASK_BIAS)
                seen = seen + jnp.sum(tied, axis=1, keepdims=True)


def _dsa_indexer(qi, ki, tail, batch, seq_len, topk, *, tq=128, tkc=512):
    nq = seq_len // tq
    nkc = seq_len // tkc
    return pl.pallas_call(
        functools.partial(_indexer_kernel, tq=tq, tkc=tkc, nkc=nkc, topk=topk),
        grid=(batch, nq),
        in_specs=[
            pl.BlockSpec((tq, IDX_HEADS * IDX_DIM), lambda b, i: (b * nq + i, 0)),
            pl.BlockSpec((seq_len, IDX_DIM), lambda b, i: (b, 0)),
            pl.BlockSpec((tq, tail.shape[1]), lambda b, i: (b * nq + i, 0)),
        ],
        out_specs=pl.BlockSpec((1, 1, nkc, tq, tkc), lambda b, i: (b, i, 0, 0, 0)),
        out_shape=jax.ShapeDtypeStruct((batch, nq, nkc, tq, tkc), F32),
        scratch_shapes=[pltpu.VMEM((nkc, tq, tkc), F32), pltpu.VMEM((tq, 128), F32)],
        compiler_params=_params("arbitrary", "arbitrary"),
        name="dsa_indexer_topk",
    )(qi, ki, tail)


def _dsa_attn_kernel(q_ref, k_ref, v_ref, bias_ref, o_ref, *, tq, tkc, groups):
    i = pl.program_id(1)
    scale = 1.0 / math.sqrt(HEAD_DIM)
    n_chunks = ((i + 1) * tq + tkc - 1) // tkc
    rows = DSA_GROUP * tq

    def head_lanes(g, r):
        h = g * DSA_GROUP + r
        return slice(h * HEAD_DIM, (h + 1) * HEAD_DIM)

    qs = [jnp.concatenate([q_ref[:, head_lanes(g, r)] for r in range(DSA_GROUP)], axis=0)
          for g in range(groups)]

    def chunk(g, c, state):
        m_run, l_run, acc = state
        start = pl.multiple_of(c * tkc, tkc)
        kv_lanes = slice(g * HEAD_DIM, (g + 1) * HEAD_DIM)
        k_blk = k_ref[pl.ds(start, tkc), kv_lanes]
        v_blk = v_ref[pl.ds(start, tkc), kv_lanes]
        s = lax.dot_general(qs[g], k_blk, (((1,), (1,)), ((), ())), preferred_element_type=F32) * scale
        s = (s.reshape(DSA_GROUP, tq, tkc) + bias_ref[0, 0, c][None]).reshape(rows, tkc)
        m_new = jnp.maximum(m_run, jnp.max(s, axis=1, keepdims=True))
        alpha = jnp.exp(m_run - m_new)
        p = jnp.exp(s - m_new)
        l_run = alpha * l_run + jnp.sum(p, axis=1, keepdims=True)
        acc = alpha * acc + jnp.dot(p.astype(BF16), v_blk, preferred_element_type=F32)
        return m_new, l_run, acc

    def body(c, states):
        return [chunk(g, c, states[g]) for g in range(groups)]

    init = (jnp.full((rows, 1), -jnp.inf, F32), jnp.zeros((rows, 1), F32),
            jnp.zeros((rows, HEAD_DIM), F32))
    states = lax.fori_loop(0, n_chunks, body, [init] * groups)
    for g in range(groups):
        _, l_run, acc = states[g]
        out = acc / l_run
        for r in range(DSA_GROUP):
            o_ref[:, head_lanes(g, r)] = out[r * tq:(r + 1) * tq].astype(o_ref.dtype)


def _dsa_attention(q, k, v, bias, batch, seq_len, *, tq=128, tkc=512, groups=2):
    nq = seq_len // tq
    gw = groups * DSA_GROUP * HEAD_DIM
    kw = groups * HEAD_DIM
    return pl.pallas_call(
        functools.partial(_dsa_attn_kernel, tq=tq, tkc=tkc, groups=groups),
        grid=(batch, nq, DSA_KV_HEADS // groups),
        in_specs=[
            pl.BlockSpec((tq, gw), lambda b, i, g: (b * nq + i, g)),
            pl.BlockSpec((seq_len, kw), lambda b, i, g: (b, g)),
            pl.BlockSpec((seq_len, kw), lambda b, i, g: (b, g)),
            pl.BlockSpec((1, 1, seq_len // tkc, tq, tkc), lambda b, i, g: (b, i, 0, 0, 0)),
        ],
        out_specs=pl.BlockSpec((tq, gw), lambda b, i, g: (b * nq + i, g)),
        out_shape=jax.ShapeDtypeStruct((batch * seq_len, N_HEADS * HEAD_DIM), BF16),
        compiler_params=_params("arbitrary", "arbitrary", "arbitrary"),
        name="dsa_attention",
    )(q, k, v, bias)


def _rope_tables(seq_len, dim):
    inv_freq = 1.0 / (ROPE_THETA ** (jnp.arange(0, dim, 2, dtype=F32) / dim))
    ang = jnp.arange(seq_len, dtype=F32)[:, None] * inv_freq[None, :]
    cos, sin = jnp.cos(ang), jnp.sin(ang)
    return jnp.concatenate([cos, cos], axis=-1), jnp.concatenate([-sin, sin], axis=-1)


def _conv_ffn(x, norm_g, w_up, conv_w, conv_b, w_down, layer, seq_len):
    h = _rmsnorm(x, norm_g, layer)
    gated = _ffn_up(h, w_up, conv_w, conv_b, layer, seq_len)
    return _matmul(gated, w_down, layer, x.shape[1], tm=512, tn=1024, out_dtype=F32, res=x,
                   name="ffn_down_residual")


def kernel(x, attn_norm_g, ffn_norm_g, sb_w_qkv, sb_w_o, dsa_w_in, dsa_q_norm_g, dsa_k_norm_g,
           dsa_ik_norm_g, dsa_w_o, ffn_w_up, ffn_conv_w, ffn_conv_b, ffn_w_down):
    batch, seq_len, d_model = x.shape
    m = batch * seq_len
    hd = N_HEADS * HEAD_DIM
    x = x.reshape(m, d_model)
    attn_g = attn_norm_g[:, None, :]
    ffn_g = ffn_norm_g[:, None, :]
    conv_b = ffn_conv_b[:, None, :]
    w_down = _to_bf16(ffn_w_down)

    h = _rmsnorm(x, attn_g, 0)
    qkv = _matmul(h, sb_w_qkv, 0, 3 * hd, tm=1024, tn=1024, out_dtype=BF16, name="sb_qkv_proj")
    mixed = _sb_attention(qkv.reshape(batch, seq_len, 3 * hd)).reshape(m, hd)
    x = _matmul(mixed, sb_w_o, 0, d_model, tm=1024, tn=1024, out_dtype=F32, res=x, name="sb_out_residual")
    x = _conv_ffn(x, ffn_g, ffn_w_up, ffn_conv_w, conv_b, w_down, 0, seq_len)

    n_main = 2 * hd + 2 * DSA_KV_HEADS * HEAD_DIM
    h = _rmsnorm(x, attn_g, 1)
    proj = _matmul(h, dsa_w_in, 0, n_main, tm=1024, tn=1024, out_dtype=F32, name="dsa_in_proj")
    w_tail = dsa_w_in[:, :, n_main:]
    tail = _matmul(h, w_tail, 0, w_tail.shape[2], tm=1024, tn=w_tail.shape[2], out_dtype=F32,
                   name="dsa_in_proj_tail")
    cos, sin_signed = _rope_tables(seq_len, HEAD_DIM)
    q, k, v, qi, ki = _dsa_prep(proj, tail, cos, sin_signed, dsa_q_norm_g[:, None, :],
                                dsa_k_norm_g[:, None, :], dsa_ik_norm_g[:, None, :], 0, seq_len)
    topk = min(TOPK_MAX, seq_len // 4)
    bias = _dsa_indexer(qi, ki, tail, batch, seq_len, topk)
    mixed = _dsa_attention(q, k, v, bias, batch, seq_len)
    x = _matmul(mixed, dsa_w_o, 0, d_model, tm=1024, tn=1024, out_dtype=F32, res=x, name="dsa_out_residual")
    x = _conv_ffn(x, ffn_g, ffn_w_up, ffn_conv_w, conv_b, w_down, 1, seq_len)
    return x.reshape(batch, seq_len, d_model)
```

```python
import functools
import math

import jax
import jax.numpy as jnp
from jax import lax
from jax.experimental import pallas as pl
from jax.experimental.pallas import tpu as pltpu

N_HEADS = 16
HEAD_DIM = 128
DSA_KV_HEADS = 4
DSA_GROUP = N_HEADS // DSA_KV_HEADS
IDX_HEADS = 16
IDX_DIM = 128
TOPK_MAX = 256
D_FF = 5632
CONV_WIDTH = 3
ROPE_THETA = 10000.0
NORM_EPS = 1e-6

V7X_VMEM_BYTES = 64 * 1024 * 1024
VMEM_LIMIT_BYTES = V7X_VMEM_BYTES - 8 * 1024 * 1024

F32 = jnp.float32
BF16 = jnp.bfloat16
MASK_BIAS = -1e30
F32_LOWEST = float(jnp.finfo(jnp.float32).min)
INT32_MIN = -(2 ** 31)
LOG2E = math.log2(math.e)


def _params(*sem, flags=None):
    return pltpu.CompilerParams(dimension_semantics=sem, vmem_limit_bytes=VMEM_LIMIT_BYTES, flags=flags)


def _rmsnorm_kernel(x_ref, g_ref, o_ref):
    x = x_ref[...]
    ms = jnp.mean(x * x, axis=-1, keepdims=True)
    o_ref[...] = ((x * lax.rsqrt(ms + NORM_EPS)) * g_ref[0]).astype(o_ref.dtype)


def _rmsnorm(x, g_all, layer, *, tm=512):
    m, d = x.shape
    return pl.pallas_call(
        _rmsnorm_kernel,
        grid=(m // tm,),
        in_specs=[
            pl.BlockSpec((tm, d), lambda i: (i, 0)),
            pl.BlockSpec((1, 1, d), lambda i: (layer, 0, 0)),
        ],
        out_specs=pl.BlockSpec((tm, d), lambda i: (i, 0)),
        out_shape=jax.ShapeDtypeStruct((m, d), BF16),
        compiler_params=_params("arbitrary"),
        name="rmsnorm",
    )(x, g_all)


def _matmul_kernel(*refs, has_res, cast_w):
    a_ref, w_ref, *rest = refs
    r_ref = rest.pop(0) if has_res else None
    o_ref = rest.pop(0)
    if cast_w:
        (w_bf,) = rest

        @pl.when(pl.program_id(1) == 0)
        def _():
            w_bf[...] = w_ref[...].astype(BF16)

        w = w_bf[...]
    else:
        w = w_ref[...]
    out = jnp.dot(a_ref[...], w, preferred_element_type=F32)
    if has_res:
        out = r_ref[...] + out
    o_ref[...] = out.astype(o_ref.dtype)


def _matmul(a, w_all, layer, n_out, *, tm, tn, out_dtype, res=None, name):
    m, k_dim = a.shape
    cast_w = w_all.dtype != BF16
    in_specs = [
        pl.BlockSpec((tm, k_dim), lambda j, i: (i, 0)),
        pl.BlockSpec((None, k_dim, tn), lambda j, i: (layer, 0, j)),
    ]
    args = [a, w_all]
    if res is not None:
        in_specs.append(pl.BlockSpec((tm, tn), lambda j, i: (i, j)))
        args.append(res)
    return pl.pallas_call(
        functools.partial(_matmul_kernel, has_res=res is not None, cast_w=cast_w),
        grid=(n_out // tn, m // tm),
        in_specs=in_specs,
        out_specs=pl.BlockSpec((tm, tn), lambda j, i: (i, j)),
        out_shape=jax.ShapeDtypeStruct((m, n_out), out_dtype),
        scratch_shapes=[pltpu.VMEM((k_dim, tn), BF16)] if cast_w else [],
        compiler_params=_params("arbitrary", "arbitrary"),
        name=name,
    )(*args)


def _cast_kernel(x_ref, o_ref):
    o_ref[...] = x_ref[...].astype(o_ref.dtype)


def _to_bf16(w_all, *, rows=512):
    n_layers, k_dim, n = w_all.shape
    return pl.pallas_call(
        _cast_kernel,
        grid=(n_layers, k_dim // rows),
        in_specs=[pl.BlockSpec((1, rows, n), lambda l, r: (l, r, 0))],
        out_specs=pl.BlockSpec((1, rows, n), lambda l, r: (l, r, 0)),
        out_shape=jax.ShapeDtypeStruct(w_all.shape, BF16),
        compiler_params=_params("arbitrary", "arbitrary"),
        name="weights_to_bf16",
    )(w_all)


def _sb_tile(q, k_blk, v_blk, upper, carry, causal):
    scale = 1.0 / math.sqrt(HEAD_DIM)
    raw = lax.dot_general(q, k_blk, (((1,), (1,)), ((), ())), preferred_element_type=F32)
    z = raw * scale
    sp = jnp.maximum(z, 0.0) + jnp.log(1.0 + jnp.exp2(jnp.abs(raw) * (-scale * LOG2E)))
    if causal is not None:
        sp = jnp.where(causal, sp, 0.0)
    tail = jnp.dot(sp.astype(BF16), upper, preferred_element_type=F32)
    a = jnp.exp(z - (tail + carry))
    if causal is not None:
        a = jnp.where(causal, a, 0.0)
    out = jnp.dot(a.astype(BF16), v_blk, preferred_element_type=F32)
    return out, carry + jnp.sum(sp, axis=1, keepdims=True)


def _sb_attn_kernel(q_ref, k_ref, v_ref, o_ref, *, tq, tk, heads):
    i = pl.program_id(2)
    per_q = tq // tk
    r2 = lax.broadcasted_iota(jnp.int32, (tk, tk), 0)
    c2 = lax.broadcasted_iota(jnp.int32, (tk, tk), 1)
    upper = jnp.where(r2 >= c2, 1.0, 0.0).astype(BF16)
    rows = lax.broadcasted_iota(jnp.int32, (tq, tk), 0)
    cols = lax.broadcasted_iota(jnp.int32, (tq, tk), 1)
    lanes = [slice(h * HEAD_DIM, (h + 1) * HEAD_DIM) for h in range(heads)]

    def block(j, carries, causal):
        start = pl.multiple_of(j * tk, tk)
        outs = [_sb_tile(q_ref[0, :, lanes[h]], k_ref[0, pl.ds(start, tk), lanes[h]],
                         v_ref[0, pl.ds(start, tk), lanes[h]], upper, carries[h], causal)
                for h in range(heads)]
        return [o for o, _ in outs], [c for _, c in outs]

    carries = [jnp.zeros((tq, 1), F32)] * heads
    accs = [jnp.zeros((tq, HEAD_DIM), F32)] * heads
    for d in reversed(range(per_q)):
        outs, carries = block(i * per_q + d, carries, cols + d * tk < rows)
        accs = [acc + o for acc, o in zip(accs, outs)]

    def body(step, state):
        accs, carries = state
        for d in range(per_q):
            outs, carries = block((i - step) * per_q - 1 - d, carries, None)
            accs = [acc + o for acc, o in zip(accs, outs)]
        return accs, carries

    accs, _ = lax.fori_loop(0, i, body, (accs, carries))
    for h in range(heads):
        o_ref[0, :, lanes[h]] = accs[h].astype(o_ref.dtype)


def _sb_attention(qkv, *, tq=512, tk=256, heads=4):
    b, s, _ = qkv.shape
    groups = N_HEADS // heads
    width = heads * HEAD_DIM
    return pl.pallas_call(
        functools.partial(_sb_attn_kernel, tq=tq, tk=tk, heads=heads),
        grid=(b, groups, s // tq),
        in_specs=[
            pl.BlockSpec((1, tq, width), lambda bi, g, i: (bi, i, g)),
            pl.BlockSpec((1, s, width), lambda bi, g, i: (bi, 0, groups + g)),
            pl.BlockSpec((1, s, width), lambda bi, g, i: (bi, 0, 2 * groups + g)),
        ],
        out_specs=pl.BlockSpec((1, tq, width), lambda bi, g, i: (bi, i, g)),
        out_shape=jax.ShapeDtypeStruct((b, s, N_HEADS * HEAD_DIM), BF16),
        compiler_params=_params("arbitrary", "arbitrary", "arbitrary"),
        name="sb_attention",
    )(qkv, qkv, qkv)


HISTORY_ROWS = 8


def _causal_conv(u_ref, cw, cb):
    r0 = HISTORY_ROWS
    rows = u_ref.shape[0] - r0
    return (cb + cw[2:3] * u_ref[r0:r0 + rows] + cw[1:2] * u_ref[r0 - 1:r0 - 1 + rows]
            + cw[0:1] * u_ref[r0 - 2:r0 - 2 + rows])


def _ffn_up_kernel(h_ref, wg_ref, wv_ref, cwg_ref, cwv_ref, cbg_ref, cbv_ref, o_ref,
                   wg_bf, wv_bf, hist_g, hist_v, *stage, tiles_per_seq, chunk_rows, chunk_cols):
    i = pl.program_id(1)
    tm, tn = o_ref.shape
    r0 = HISTORY_ROWS

    @pl.when(i == 0)
    def _():
        wg_bf[...] = wg_ref[...].astype(BF16)
        wv_bf[...] = wv_ref[...].astype(BF16)

    @pl.when(i % tiles_per_seq == 0)
    def _():
        hist_g[...] = jnp.zeros_like(hist_g)
        hist_v[...] = jnp.zeros_like(hist_v)

    piece = 0
    for c0 in range(0, tn, chunk_cols):
        cols = slice(c0, c0 + chunk_cols)
        prev_g = hist_g[:, cols]
        prev_v = hist_v[:, cols]
        for t0 in range(0, tm, chunk_rows):
            ug_s, uv_s = stage[2 * (piece % 2)], stage[2 * (piece % 2) + 1]
            piece += 1
            h = h_ref[t0:t0 + chunk_rows, :]
            ug_s[0:r0] = prev_g
            uv_s[0:r0] = prev_v
            ug_s[r0:] = jnp.dot(h, wg_bf[:, cols], preferred_element_type=F32)
            uv_s[r0:] = jnp.dot(h, wv_bf[:, cols], preferred_element_type=F32)
            gate = _causal_conv(ug_s, cwg_ref[:, cols], cbg_ref[0, :, cols])
            val = _causal_conv(uv_s, cwv_ref[:, cols], cbv_ref[0, :, cols])
            o_ref[t0:t0 + chunk_rows, cols] = (jax.nn.silu(gate) * val).astype(o_ref.dtype)
            prev_g = ug_s[chunk_rows:]
            prev_v = uv_s[chunk_rows:]
        hist_g[:, cols] = prev_g
        hist_v[:, cols] = prev_v


def _ffn_up(h, w_up_all, conv_w_all, conv_b_all, layer, seq_len, *, tm=1024, tn=512,
            chunk_rows=512, chunk_cols=256):
    m, d = h.shape
    nj = D_FF // tn
    return pl.pallas_call(
        functools.partial(_ffn_up_kernel, tiles_per_seq=seq_len // tm, chunk_rows=chunk_rows,
                          chunk_cols=chunk_cols),
        grid=(nj, m // tm),
        in_specs=[
            pl.BlockSpec((tm, d), lambda j, i: (i, 0)),
            pl.BlockSpec((None, d, tn), lambda j, i: (layer, 0, j)),
            pl.BlockSpec((None, d, tn), lambda j, i: (layer, 0, nj + j)),
            pl.BlockSpec((None, CONV_WIDTH, tn), lambda j, i: (layer, 0, j)),
            pl.BlockSpec((None, CONV_WIDTH, tn), lambda j, i: (layer, 0, nj + j)),
            pl.BlockSpec((1, 1, tn), lambda j, i: (layer, 0, j)),
            pl.BlockSpec((1, 1, tn), lambda j, i: (layer, 0, nj + j)),
        ],
        out_specs=pl.BlockSpec((tm, tn), lambda j, i: (i, j)),
        out_shape=jax.ShapeDtypeStruct((m, D_FF), BF16),
        scratch_shapes=[
            pltpu.VMEM((d, tn), BF16),
            pltpu.VMEM((d, tn), BF16),
            pltpu.VMEM((HISTORY_ROWS, tn), F32),
            pltpu.VMEM((HISTORY_ROWS, tn), F32),
        ] + [pltpu.VMEM((HISTORY_ROWS + chunk_rows, chunk_cols), F32)] * 4,
        compiler_params=_params("arbitrary", "arbitrary"),
        name="ffn_up_conv_gate",
    )(h, w_up_all, w_up_all, conv_w_all, conv_w_all, conv_b_all, conv_b_all)


def _head_norm_rope(x, gain, cos, sin_signed):
    if gain is not None:
        ms = jnp.mean(x * x, axis=-1, keepdims=True)
        x = (x * lax.rsqrt(ms + NORM_EPS)) * gain
    return x * cos + pltpu.roll(x, HEAD_DIM // 2, 1) * sin_signed


def _dsa_prep_kernel(p_ref, t_ref, cos_ref, sin_ref, qg_ref, kg_ref, ig_ref,
                     q_ref, k_ref, v_ref, qi_ref, ki_ref):
    cos = cos_ref[...]
    sin = sin_ref[...]
    o_k = N_HEADS * HEAD_DIM
    o_v = o_k + DSA_KV_HEADS * HEAD_DIM
    o_qi = o_v + DSA_KV_HEADS * HEAD_DIM
    for h in range(N_HEADS):
        sl = slice(h * HEAD_DIM, (h + 1) * HEAD_DIM)
        q_ref[:, sl] = _head_norm_rope(p_ref[:, sl], qg_ref[...], cos, sin).astype(BF16)
        qi_ref[:, sl] = _head_norm_rope(
            p_ref[:, o_qi + h * IDX_DIM:o_qi + (h + 1) * IDX_DIM], None, cos, sin).astype(BF16)
    for g in range(DSA_KV_HEADS):
        sl = slice(g * HEAD_DIM, (g + 1) * HEAD_DIM)
        k_ref[:, sl] = _head_norm_rope(
            p_ref[:, o_k + g * HEAD_DIM:o_k + (g + 1) * HEAD_DIM], kg_ref[...], cos, sin).astype(BF16)
    v_ref[...] = p_ref[:, o_v:o_qi].T.astype(BF16)
    ki_ref[...] = _head_norm_rope(t_ref[:, 0:IDX_DIM], ig_ref[...], cos, sin).astype(BF16)


def _dsa_prep(proj, tail, cos, sin_signed, q_gain, k_gain, ik_gain, layer, batch, seq_len, *,
              tm=256, tkc=512):
    m = proj.shape[0]
    n_main = proj.shape[1]
    n_tail = tail.shape[1]
    tiles_per_seq = seq_len // tm
    tiles_per_chunk = tkc // tm
    hq = N_HEADS * HEAD_DIM
    hk = DSA_KV_HEADS * HEAD_DIM
    row = lambda i: (i, 0)

    def v_block(i):
        pos_tile = i % tiles_per_seq
        return i // tiles_per_seq, pos_tile // tiles_per_chunk, 0, pos_tile % tiles_per_chunk
    gain_spec = pl.BlockSpec((None, 1, HEAD_DIM), lambda i: (layer, 0, 0))
    tab_spec = pl.BlockSpec((tm, HEAD_DIM), lambda i: (i % tiles_per_seq, 0))
    return pl.pallas_call(
        _dsa_prep_kernel,
        grid=(m // tm,),
        in_specs=[
            pl.BlockSpec((tm, n_main), row),
            pl.BlockSpec((tm, n_tail), row),
            tab_spec, tab_spec, gain_spec, gain_spec, gain_spec,
        ],
        out_specs=[
            pl.BlockSpec((tm, hq), row),
            pl.BlockSpec((tm, hk), row),
            pl.BlockSpec((None, None, hk, tm), v_block),
            pl.BlockSpec((tm, IDX_HEADS * IDX_DIM), row),
            pl.BlockSpec((tm, IDX_DIM), row),
        ],
        out_shape=[
            jax.ShapeDtypeStruct((m, hq), BF16),
            jax.ShapeDtypeStruct((m, hk), BF16),
            jax.ShapeDtypeStruct((batch, seq_len // tkc, hk, tkc), BF16),
            jax.ShapeDtypeStruct((m, IDX_HEADS * IDX_DIM), BF16),
            jax.ShapeDtypeStruct((m, IDX_DIM), BF16),
        ],
        compiler_params=_params("arbitrary"),
        name="dsa_norm_rope",
    )(proj, tail, cos, sin_signed, q_gain, k_gain, ik_gain)


def _key_to_f32(key):
    bits = jnp.where(key < 0, key ^ INT32_MIN, ~key)
    return lax.bitcast_convert_type(bits, F32)


def _indexer_kernel(qi_ref, ki_ref, w_ref, bias_ref, sc_ref, cnt_ref, *, tq, tkc, tkb, nkc, topk):
    i = pl.program_id(1)
    t0 = i * tq
    w = w_ref[...] * (1.0 / math.sqrt(IDX_DIM) / math.sqrt(IDX_HEADS))
    key_pos = lax.broadcasted_iota(jnp.int32, (tkc, tq), 0)
    qry_pos = lax.broadcasted_iota(jnp.int32, (tkc, tq), 1) + t0

    def chunk_in_range(c):
        return c * tkc < t0 + tq

    for c in range(nkc):

        @pl.when(chunk_in_range(c))
        def _():
            k_blk = ki_ref[c * tkc:(c + 1) * tkc, :]
            acc = jnp.zeros((tkc, tq), F32)
            for h in range(IDX_HEADS):
                logit = lax.dot_general(
                    k_blk, qi_ref[:, h * IDX_DIM:(h + 1) * IDX_DIM], (((1,), (1,)), ((), ())),
                    preferred_element_type=F32)
                acc = acc + jnp.maximum(logit, 0.0) * w[h:h + 1, :]
            sc_ref[c] = jnp.where(key_pos + c * tkc <= qry_pos, acc, -jnp.inf)

        @pl.when(jnp.logical_not(chunk_in_range(c)))
        def _():
            sc_ref[c] = jnp.full((tkc, tq), -jnp.inf, F32)

    def sublane_partial(hit):
        return jnp.sum(hit.reshape(tkc // 8, 8, tq), axis=0)

    def count_ge(thr):
        cnt_ref[...] = sublane_partial(jnp.where(sc_ref[0] >= thr, 1.0, 0.0))
        for c in range(1, nkc):

            @pl.when(chunk_in_range(c))
            def _():
                cnt_ref[...] += sublane_partial(jnp.where(sc_ref[c] >= thr, 1.0, 0.0))

        return jnp.sum(cnt_ref[...], axis=0, keepdims=True)

    def store_bias(c, bias):
        for part in range(tkc // tkb):
            bias_ref[0, 0, c * (tkc // tkb) + part] = bias[part * tkb:(part + 1) * tkb]

    def write_bias(select_fn):
        for c in range(nkc):
            store_bias(c, jnp.where(select_fn(sc_ref[c]), 0.0, MASK_BIAS))

    @pl.when(t0 + tq <= topk)
    def _():
        write_bias(lambda sc: sc >= F32_LOWEST)

    @pl.when(t0 + tq > topk)
    def _():
        def bit_step(step, key):
            trial = key | lax.shift_left(jnp.int32(1), 31 - step)
            keep = count_ge(_key_to_f32(trial)) >= topk
            return jnp.where(keep, trial, key)

        key = lax.fori_loop(0, 32, bit_step, jnp.zeros((1, tq), jnp.int32))
        few = lax.broadcasted_iota(jnp.int32, (1, tq), 1) + t0 < topk - 1
        thr = jnp.where(few, F32_LOWEST, _key_to_f32(key))
        write_bias(lambda sc: sc >= thr)

        n_ge = count_ge(thr)

        @pl.when(jnp.max(jnp.where(few, 0.0, n_ge)) > topk)
        def _():
            n_gt = sum(jnp.sum(jnp.where(sc_ref[c] > thr, 1.0, 0.0), axis=0, keepdims=True)
                       for c in range(nkc))
            room = jnp.where(few, float(tkc * nkc), topk - n_gt)
            r2 = lax.broadcasted_iota(jnp.int32, (tkc, tkc), 0)
            c2 = lax.broadcasted_iota(jnp.int32, (tkc, tkc), 1)
            before = jnp.where(c2 < r2, 1.0, 0.0).astype(BF16)
            seen = jnp.zeros((1, tq), F32)
            for c in range(nkc):
                sc = sc_ref[c]
                tied = jnp.where(sc == thr, 1.0, 0.0)
                rank = seen + jnp.dot(before, tied.astype(BF16), preferred_element_type=F32)
                take = jnp.where(sc > thr, 1.0, jnp.where(rank < room, tied, 0.0))
                store_bias(c, jnp.where(take > 0.0, 0.0, MASK_BIAS))
                seen = seen + jnp.sum(tied, axis=0, keepdims=True)


def _dsa_indexer(qi, ki, w_t, batch, seq_len, topk, *, tq, tkb, tkc=512):
    nq = seq_len // tq
    nkc = seq_len // tkc
    nkb = seq_len // tkb
    return pl.pallas_call(
        functools.partial(_indexer_kernel, tq=tq, tkc=tkc, tkb=tkb, nkc=nkc, topk=topk),
        grid=(batch, nq),
        in_specs=[
            pl.BlockSpec((tq, IDX_HEADS * IDX_DIM), lambda b, i: (b * nq + i, 0)),
            pl.BlockSpec((seq_len, IDX_DIM), lambda b, i: (b, 0)),
            pl.BlockSpec((IDX_HEADS, tq), lambda b, i: (0, b * nq + i)),
        ],
        out_specs=pl.BlockSpec((1, 1, nkb, tkb, tq), lambda b, i: (b, i, 0, 0, 0)),
        out_shape=jax.ShapeDtypeStruct((batch, nq, nkb, tkb, tq), F32),
        scratch_shapes=[pltpu.VMEM((nkc, tkc, tq), F32), pltpu.VMEM((8, tq), F32)],
        compiler_params=_params("arbitrary", "arbitrary"),
        name="dsa_indexer_topk",
    )(qi, ki, w_t)


def _dsa_attn_kernel(q_ref, k_ref, vt_ref, bias_ref, o_ref, *, tq, tkb, groups, chain_heads):
    i = pl.program_id(1)
    scale = 1.0 / math.sqrt(HEAD_DIM)
    n_chunks = ((i + 1) * tq + tkb - 1) // tkb
    cols = chain_heads * tq
    chains = [(g, r0) for g in range(groups) for r0 in range(0, DSA_GROUP, chain_heads)]

    def head_lanes(g, r):
        h = g * DSA_GROUP + r
        return slice(h * HEAD_DIM, (h + 1) * HEAD_DIM)

    qs = [jnp.concatenate([q_ref[:, head_lanes(g, r0 + r)] for r in range(chain_heads)], axis=0)
          for g, r0 in chains]

    def over_keys(x, op):
        return op(op(x.reshape(tkb // 8, 8, cols), axis=0), axis=0, keepdims=True)

    def kv_lanes(n):
        g, _ = chains[n]
        return slice(g * HEAD_DIM, (g + 1) * HEAD_DIM)

    def scores(n, c):
        k_blk = k_ref[pl.ds(pl.multiple_of(c * tkb, tkb), tkb), kv_lanes(n)]
        return lax.dot_general(k_blk, qs[n], (((1,), (1,)), ((), ())), preferred_element_type=F32)

    def softmax_step(n, c, state, raw, bias):
        m_run, l_run, acc = state
        s = raw * scale + bias
        m_new = jnp.maximum(m_run, over_keys(s, jnp.max))
        alpha = jnp.exp(m_run - m_new)
        p = jnp.exp(s - m_new)
        l_run = alpha * l_run + over_keys(p, jnp.sum)
        acc = alpha * acc + jnp.dot(vt_ref[0, c, kv_lanes(n), :], p.astype(BF16),
                                    preferred_element_type=F32)
        return m_new, l_run, acc

    def body(c, states):
        bias = jnp.concatenate([bias_ref[0, 0, c]] * chain_heads, axis=1)
        raws = [scores(n, c) for n in range(len(chains))]
        return [softmax_step(n, c, states[n], raws[n], bias) for n in range(len(chains))]

    init = (jnp.full((1, cols), -jnp.inf, F32), jnp.zeros((1, cols), F32),
            jnp.zeros((HEAD_DIM, cols), F32))
    states = lax.fori_loop(0, n_chunks, body, [init] * len(chains))
    for (g, r0), (_, l_run, acc) in zip(chains, states):
        out_t = acc / l_run
        for r in range(chain_heads):
            o_ref[:, head_lanes(g, r0 + r)] = out_t[:, r * tq:(r + 1) * tq].T.astype(o_ref.dtype)


def _dsa_attention(q, k, v_t, bias, batch, seq_len, *, tq, tkb, groups=2, chain_heads=2):
    nq = seq_len // tq
    gw = groups * DSA_GROUP * HEAD_DIM
    kw = groups * HEAD_DIM
    nkb = seq_len // tkb
    return pl.pallas_call(
        functools.partial(_dsa_attn_kernel, tq=tq, tkb=tkb, groups=groups, chain_heads=chain_heads),
        grid=(batch, nq, DSA_KV_HEADS // groups),
        in_specs=[
            pl.BlockSpec((tq, gw), lambda b, i, g: (b * nq + i, g)),
            pl.BlockSpec((seq_len, kw), lambda b, i, g: (b, g)),
            pl.BlockSpec((1, nkb, kw, tkb), lambda b, i, g: (b, 0, g, 0)),
            pl.BlockSpec((1, 1, nkb, tkb, tq), lambda b, i, g: (b, i, 0, 0, 0)),
        ],
        out_specs=pl.BlockSpec((tq, gw), lambda b, i, g: (b * nq + i, g)),
        out_shape=jax.ShapeDtypeStruct((batch * seq_len, N_HEADS * HEAD_DIM), BF16),
        compiler_params=_params("arbitrary", "arbitrary", "arbitrary"),
        name="dsa_attention",
    )(q, k, v_t, bias)


def _rope_tables(seq_len, dim):
    inv_freq = 1.0 / (ROPE_THETA ** (jnp.arange(0, dim, 2, dtype=F32) / dim))
    ang = jnp.arange(seq_len, dtype=F32)[:, None] * inv_freq[None, :]
    cos, sin = jnp.cos(ang), jnp.sin(ang)
    return jnp.concatenate([cos, cos], axis=-1), jnp.concatenate([-sin, sin], axis=-1)


def _conv_ffn(x, norm_g, w_up, conv_w, conv_b, w_down, layer, seq_len):
    h = _rmsnorm(x, norm_g, layer)
    gated = _ffn_up(h, w_up, conv_w, conv_b, layer, seq_len)
    return _matmul(gated, w_down, layer, x.shape[1], tm=512, tn=1024, out_dtype=F32, res=x,
                   name="ffn_down_residual")


def kernel(x, attn_norm_g, ffn_norm_g, sb_w_qkv, sb_w_o, dsa_w_in, dsa_q_norm_g, dsa_k_norm_g,
           dsa_ik_norm_g, dsa_w_o, ffn_w_up, ffn_conv_w, ffn_conv_b, ffn_w_down):
    batch, seq_len, d_model = x.shape
    m = batch * seq_len
    hd = N_HEADS * HEAD_DIM
    x = x.reshape(m, d_model)
    attn_g = attn_norm_g[:, None, :]
    ffn_g = ffn_norm_g[:, None, :]
    conv_b = ffn_conv_b[:, None, :]
    w_down = _to_bf16(ffn_w_down)

    h = _rmsnorm(x, attn_g, 0)
    qkv = _matmul(h, sb_w_qkv, 0, 3 * hd, tm=1024, tn=1024, out_dtype=BF16, name="sb_qkv_proj")
    mixed = _sb_attention(qkv.reshape(batch, seq_len, 3 * hd)).reshape(m, hd)
    x = _matmul(mixed, sb_w_o, 0, d_model, tm=1024, tn=1024, out_dtype=F32, res=x, name="sb_out_residual")
    x = _conv_ffn(x, ffn_g, ffn_w_up, ffn_conv_w, conv_b, w_down, 0, seq_len)

    n_main = 2 * hd + 2 * DSA_KV_HEADS * HEAD_DIM
    h = _rmsnorm(x, attn_g, 1)
    proj = _matmul(h, dsa_w_in, 0, n_main, tm=1024, tn=1024, out_dtype=F32, name="dsa_in_proj")
    w_tail = dsa_w_in[:, :, n_main:]
    tail = _matmul(h, w_tail, 0, w_tail.shape[2], tm=1024, tn=w_tail.shape[2], out_dtype=F32,
                   name="dsa_in_proj_tail")
    cos, sin_signed = _rope_tables(seq_len, HEAD_DIM)
    tq, tkb = 256, 256
    q, k, v_t, qi, ki = _dsa_prep(proj, tail, cos, sin_signed, dsa_q_norm_g[:, None, :],
                                  dsa_k_norm_g[:, None, :], dsa_ik_norm_g[:, None, :], 0, batch, seq_len,
                                  tkc=tkb)
    topk = min(TOPK_MAX, seq_len // 4)
    w_t = tail[:, IDX_DIM:].T
    bias = _dsa_indexer(qi, ki, w_t, batch, seq_len, topk, tq=tq, tkb=tkb)
    mixed = _dsa_attention(q, k, v_t, bias, batch, seq_len, tq=tq, tkb=tkb)
    x = _matmul(mixed, dsa_w_o, 0, d_model, tm=1024, tn=1024, out_dtype=F32, res=x, name="dsa_out_residual")
    x = _conv_ffn(x, ffn_g, ffn_w_up, ffn_conv_w, conv_b, w_down, 1, seq_len)
    return x.reshape(batch, seq_len, d_model)
```

```python
import functools
import math

import jax
import jax.numpy as jnp
from jax import lax
from jax.experimental import pallas as pl
from jax.experimental.pallas import tpu as pltpu

N_HEADS = 16
HEAD_DIM = 128
DSA_KV_HEADS = 4
DSA_GROUP = N_HEADS // DSA_KV_HEADS
IDX_HEADS = 16
IDX_DIM = 128
TOPK_MAX = 256
D_FF = 5632
CONV_WIDTH = 3
ROPE_THETA = 10000.0
NORM_EPS = 1e-6

V7X_VMEM_BYTES = 64 * 1024 * 1024
VMEM_LIMIT_BYTES = V7X_VMEM_BYTES - 8 * 1024 * 1024

F32 = jnp.float32
BF16 = jnp.bfloat16
MASK_BIAS = -1e30
F32_LOWEST = float(jnp.finfo(jnp.float32).min)
INT32_MIN = -(2 ** 31)
LOG2E = math.log2(math.e)
DSA_Q_SCALE = LOG2E / math.sqrt(HEAD_DIM)


def _params(*sem, flags=None):
    return pltpu.CompilerParams(dimension_semantics=sem, vmem_limit_bytes=VMEM_LIMIT_BYTES, flags=flags)


def _rmsnorm_kernel(x_ref, g_ref, o_ref):
    x = x_ref[...]
    ms = jnp.mean(x * x, axis=-1, keepdims=True)
    o_ref[...] = ((x * lax.rsqrt(ms + NORM_EPS)) * g_ref[0]).astype(o_ref.dtype)


def _rmsnorm(x, g_all, layer, *, tm=512):
    m, d = x.shape
    return pl.pallas_call(
        _rmsnorm_kernel,
        grid=(m // tm,),
        in_specs=[
            pl.BlockSpec((tm, d), lambda i: (i, 0)),
            pl.BlockSpec((1, 1, d), lambda i: (layer, 0, 0)),
        ],
        out_specs=pl.BlockSpec((tm, d), lambda i: (i, 0)),
        out_shape=jax.ShapeDtypeStruct((m, d), BF16),
        compiler_params=_params("arbitrary"),
        name="rmsnorm",
    )(x, g_all)


def _matmul_kernel(*refs, has_res, cast_w, w_is_transposed, scaled_blocks, block_scale):
    a_ref, w_ref, *rest = refs
    r_ref = rest.pop(0) if has_res else None
    o_ref = rest.pop(0)
    if cast_w:
        (w_bf,) = rest

        @pl.when(pl.program_id(1) == 0)
        def _():
            w = w_ref[...]
            w_bf[...] = (w.T if w_is_transposed else w).astype(BF16)

        w = w_bf[...]
    else:
        w = w_ref[...]
    out = jnp.dot(a_ref[...], w, preferred_element_type=F32)
    if scaled_blocks:
        out = out * jnp.where(pl.program_id(0) < scaled_blocks, block_scale, 1.0)
    if has_res:
        out = r_ref[...] + out
    o_ref[...] = out.astype(o_ref.dtype)


def _matmul(a, w_all, layer, n_out, *, tm, tn, out_dtype, res=None, w_is_transposed=False,
            scaled_cols=0, col_scale=1.0, name):
    m, k_dim = a.shape
    cast_w = w_all.dtype != BF16
    assert cast_w or not w_is_transposed
    assert scaled_cols % tn == 0
    if w_is_transposed:
        w_spec = pl.BlockSpec((None, tn, k_dim), lambda j, i: (layer, j, 0))
    else:
        w_spec = pl.BlockSpec((None, k_dim, tn), lambda j, i: (layer, 0, j))
    in_specs = [pl.BlockSpec((tm, k_dim), lambda j, i: (i, 0)), w_spec]
    args = [a, w_all]
    if res is not None:
        in_specs.append(pl.BlockSpec((tm, tn), lambda j, i: (i, j)))
        args.append(res)
    return pl.pallas_call(
        functools.partial(_matmul_kernel, has_res=res is not None, cast_w=cast_w,
                          w_is_transposed=w_is_transposed, scaled_blocks=scaled_cols // tn,
                          block_scale=col_scale),
        grid=(n_out // tn, m // tm),
        in_specs=in_specs,
        out_specs=pl.BlockSpec((tm, tn), lambda j, i: (i, j)),
        out_shape=jax.ShapeDtypeStruct((m, n_out), out_dtype),
        scratch_shapes=[pltpu.VMEM((k_dim, tn), BF16)] if cast_w else [],
        compiler_params=_params("arbitrary", "arbitrary"),
        name=name,
    )(*args)


def _cast_kernel(x_ref, o_ref):
    o_ref[...] = x_ref[...].astype(o_ref.dtype)


def _to_bf16(w_all, *, rows=512):
    n_layers, k_dim, n = w_all.shape
    return pl.pallas_call(
        _cast_kernel,
        grid=(n_layers, k_dim // rows),
        in_specs=[pl.BlockSpec((1, rows, n), lambda l, r: (l, r, 0))],
        out_specs=pl.BlockSpec((1, rows, n), lambda l, r: (l, r, 0)),
        out_shape=jax.ShapeDtypeStruct(w_all.shape, BF16),
        compiler_params=_params("arbitrary", "arbitrary"),
        name="weights_to_bf16",
    )(w_all)


def _sb_tile(q, k_blk, v_blk, upper, carry, causal):
    z = lax.dot_general(q, k_blk, (((1,), (1,)), ((), ())), preferred_element_type=F32)
    sp = jnp.maximum(z, 0.0) + jnp.log(1.0 + jnp.exp2(jnp.abs(z) * (-LOG2E)))
    if causal is not None:
        sp = jnp.where(causal, sp, 0.0)
    tail = jnp.dot(sp.astype(BF16), upper, preferred_element_type=F32)
    a = jnp.exp(z - (tail + carry))
    if causal is not None:
        a = jnp.where(causal, a, 0.0)
    out = jnp.dot(a.astype(BF16), v_blk, preferred_element_type=F32)
    return out, carry + jnp.sum(sp, axis=1, keepdims=True)


def _sb_attn_kernel(q_ref, k_ref, v_ref, o_ref, *, tq, tk, heads):
    i = pl.program_id(2)
    per_q = tq // tk
    r2 = lax.broadcasted_iota(jnp.int32, (tk, tk), 0)
    c2 = lax.broadcasted_iota(jnp.int32, (tk, tk), 1)
    upper = jnp.where(r2 >= c2, 1.0, 0.0).astype(BF16)
    rows = lax.broadcasted_iota(jnp.int32, (tq, tk), 0)
    cols = lax.broadcasted_iota(jnp.int32, (tq, tk), 1)
    lanes = [slice(h * HEAD_DIM, (h + 1) * HEAD_DIM) for h in range(heads)]

    def block(j, carries, causal):
        start = pl.multiple_of(j * tk, tk)
        outs = [_sb_tile(q_ref[0, :, lanes[h]], k_ref[0, pl.ds(start, tk), lanes[h]],
                         v_ref[0, pl.ds(start, tk), lanes[h]], upper, carries[h], causal)
                for h in range(heads)]
        return [o for o, _ in outs], [c for _, c in outs]

    carries = [jnp.zeros((tq, 1), F32)] * heads
    accs = [jnp.zeros((tq, HEAD_DIM), F32)] * heads
    for d in reversed(range(per_q)):
        outs, carries = block(i * per_q + d, carries, cols + d * tk < rows)
        accs = [acc + o for acc, o in zip(accs, outs)]

    def body(step, state):
        accs, carries = state
        for d in range(per_q):
            outs, carries = block((i - step) * per_q - 1 - d, carries, None)
            accs = [acc + o for acc, o in zip(accs, outs)]
        return accs, carries

    accs, _ = lax.fori_loop(0, i, body, (accs, carries))
    for h in range(heads):
        o_ref[0, :, lanes[h]] = accs[h].astype(o_ref.dtype)


def _sb_attention(qkv, *, tq=512, tk=256, heads=4):
    b, s, _ = qkv.shape
    groups = N_HEADS // heads
    width = heads * HEAD_DIM
    return pl.pallas_call(
        functools.partial(_sb_attn_kernel, tq=tq, tk=tk, heads=heads),
        grid=(b, groups, s // tq),
        in_specs=[
            pl.BlockSpec((1, tq, width), lambda bi, g, i: (bi, i, g)),
            pl.BlockSpec((1, s, width), lambda bi, g, i: (bi, 0, groups + g)),
            pl.BlockSpec((1, s, width), lambda bi, g, i: (bi, 0, 2 * groups + g)),
        ],
        out_specs=pl.BlockSpec((1, tq, width), lambda bi, g, i: (bi, i, g)),
        out_shape=jax.ShapeDtypeStruct((b, s, N_HEADS * HEAD_DIM), BF16),
        compiler_params=_params("arbitrary", "arbitrary", "arbitrary"),
        name="sb_attention",
    )(qkv, qkv, qkv)


HISTORY_ROWS = 8


def _causal_conv(u_ref, cw, cb):
    r0 = HISTORY_ROWS
    rows = u_ref.shape[0] - r0
    return (cb + cw[2:3] * u_ref[r0:r0 + rows] + cw[1:2] * u_ref[r0 - 1:r0 - 1 + rows]
            + cw[0:1] * u_ref[r0 - 2:r0 - 2 + rows])


def _ffn_up_kernel(h_ref, wg_ref, wv_ref, cwg_ref, cwv_ref, cbg_ref, cbv_ref, o_ref,
                   wg_bf, wv_bf, hist_g, hist_v, *stage, tiles_per_seq, chunk_rows, chunk_cols):
    i = pl.program_id(1)
    tm, tn = o_ref.shape
    r0 = HISTORY_ROWS

    @pl.when(i == 0)
    def _():
        wg_bf[...] = wg_ref[...].astype(BF16)
        wv_bf[...] = wv_ref[...].astype(BF16)

    @pl.when(i % tiles_per_seq == 0)
    def _():
        hist_g[...] = jnp.zeros_like(hist_g)
        hist_v[...] = jnp.zeros_like(hist_v)

    piece = 0
    for c0 in range(0, tn, chunk_cols):
        cols = slice(c0, c0 + chunk_cols)
        prev_g = hist_g[:, cols]
        prev_v = hist_v[:, cols]
        for t0 in range(0, tm, chunk_rows):
            ug_s, uv_s = stage[2 * (piece % 2)], stage[2 * (piece % 2) + 1]
            piece += 1
            h = h_ref[t0:t0 + chunk_rows, :]
            ug_s[0:r0] = prev_g
            uv_s[0:r0] = prev_v
            ug_s[r0:] = jnp.dot(h, wg_bf[:, cols], preferred_element_type=F32)
            uv_s[r0:] = jnp.dot(h, wv_bf[:, cols], preferred_element_type=F32)
            gate = _causal_conv(ug_s, cwg_ref[:, cols], cbg_ref[0, :, cols])
            val = _causal_conv(uv_s, cwv_ref[:, cols], cbv_ref[0, :, cols])
            o_ref[t0:t0 + chunk_rows, cols] = (jax.nn.silu(gate) * val).astype(o_ref.dtype)
            prev_g = ug_s[chunk_rows:]
            prev_v = uv_s[chunk_rows:]
        hist_g[:, cols] = prev_g
        hist_v[:, cols] = prev_v


def _ffn_up(h, w_up_all, conv_w_all, conv_b_all, layer, seq_len, *, tm=2048, tn=512,
            chunk_rows=512, chunk_cols=256):
    m, d = h.shape
    nj = D_FF // tn
    return pl.pallas_call(
        functools.partial(_ffn_up_kernel, tiles_per_seq=seq_len // tm, chunk_rows=chunk_rows,
                          chunk_cols=chunk_cols),
        grid=(nj, m // tm),
        in_specs=[
            pl.BlockSpec((tm, d), lambda j, i: (i, 0)),
            pl.BlockSpec((None, d, tn), lambda j, i: (layer, 0, j)),
            pl.BlockSpec((None, d, tn), lambda j, i: (layer, 0, nj + j)),
            pl.BlockSpec((None, CONV_WIDTH, tn), lambda j, i: (layer, 0, j)),
            pl.BlockSpec((None, CONV_WIDTH, tn), lambda j, i: (layer, 0, nj + j)),
            pl.BlockSpec((1, 1, tn), lambda j, i: (layer, 0, j)),
            pl.BlockSpec((1, 1, tn), lambda j, i: (layer, 0, nj + j)),
        ],
        out_specs=pl.BlockSpec((tm, tn), lambda j, i: (i, j)),
        out_shape=jax.ShapeDtypeStruct((m, D_FF), BF16),
        scratch_shapes=[
            pltpu.VMEM((d, tn), BF16),
            pltpu.VMEM((d, tn), BF16),
            pltpu.VMEM((HISTORY_ROWS, tn), F32),
            pltpu.VMEM((HISTORY_ROWS, tn), F32),
        ] + [pltpu.VMEM((HISTORY_ROWS + chunk_rows, chunk_cols), F32)] * 4,
        compiler_params=_params("arbitrary", "arbitrary"),
        name="ffn_up_conv_gate",
    )(h, w_up_all, w_up_all, conv_w_all, conv_w_all, conv_b_all, conv_b_all)


def _head_norm_rope(x, gain, cos, sin_signed):
    if gain is not None:
        ms = jnp.mean(x * x, axis=-1, keepdims=True)
        x = (x * lax.rsqrt(ms + NORM_EPS)) * gain
    return x * cos + pltpu.roll(x, HEAD_DIM // 2, 1) * sin_signed


def _dsa_prep_kernel(p_ref, t_ref, cos_ref, sin_ref, qg_ref, kg_ref, ig_ref,
                     q_ref, k_ref, v_ref, qi_ref, ki_ref):
    cos = cos_ref[...]
    sin = sin_ref[...]
    o_k = N_HEADS * HEAD_DIM
    o_v = o_k + DSA_KV_HEADS * HEAD_DIM
    o_qi = o_v + DSA_KV_HEADS * HEAD_DIM
    for h in range(N_HEADS):
        sl = slice(h * HEAD_DIM, (h + 1) * HEAD_DIM)
        q_ref[:, sl] = (_head_norm_rope(p_ref[:, sl], qg_ref[...], cos, sin) * DSA_Q_SCALE).astype(BF16)
        qi_ref[:, sl] = _head_norm_rope(
            p_ref[:, o_qi + h * IDX_DIM:o_qi + (h + 1) * IDX_DIM], None, cos, sin).astype(BF16)
    for g in range(DSA_KV_HEADS):
        sl = slice(g * HEAD_DIM, (g + 1) * HEAD_DIM)
        k_ref[:, sl] = _head_norm_rope(
            p_ref[:, o_k + g * HEAD_DIM:o_k + (g + 1) * HEAD_DIM], kg_ref[...], cos, sin).astype(BF16)
    v_ref[...] = p_ref[:, o_v:o_qi].T.astype(BF16)
    ki_ref[...] = _head_norm_rope(t_ref[:, 0:IDX_DIM], ig_ref[...], cos, sin).astype(BF16)


def _dsa_prep(proj, tail, cos, sin_signed, q_gain, k_gain, ik_gain, layer, batch, seq_len, *,
              tm=256, tkc=512):
    m = proj.shape[0]
    n_main = proj.shape[1]
    n_tail = tail.shape[1]
    tiles_per_seq = seq_len // tm
    tiles_per_chunk = tkc // tm
    hq = N_HEADS * HEAD_DIM
    hk = DSA_KV_HEADS * HEAD_DIM
    row = lambda i: (i, 0)

    def v_block(i):
        pos_tile = i % tiles_per_seq
        return i // tiles_per_seq, pos_tile // tiles_per_chunk, 0, pos_tile % tiles_per_chunk
    gain_spec = pl.BlockSpec((None, 1, HEAD_DIM), lambda i: (layer, 0, 0))
    tab_spec = pl.BlockSpec((tm, HEAD_DIM), lambda i: (i % tiles_per_seq, 0))
    return pl.pallas_call(
        _dsa_prep_kernel,
        grid=(m // tm,),
        in_specs=[
            pl.BlockSpec((tm, n_main), row),
            pl.BlockSpec((tm, n_tail), row),
            tab_spec, tab_spec, gain_spec, gain_spec, gain_spec,
        ],
        out_specs=[
            pl.BlockSpec((tm, hq), row),
            pl.BlockSpec((tm, hk), row),
            pl.BlockSpec((None, None, hk, tm), v_block),
            pl.BlockSpec((tm, IDX_HEADS * IDX_DIM), row),
            pl.BlockSpec((tm, IDX_DIM), row),
        ],
        out_shape=[
            jax.ShapeDtypeStruct((m, hq), BF16),
            jax.ShapeDtypeStruct((m, hk), BF16),
            jax.ShapeDtypeStruct((batch, seq_len // tkc, hk, tkc), BF16),
            jax.ShapeDtypeStruct((m, IDX_HEADS * IDX_DIM), BF16),
            jax.ShapeDtypeStruct((m, IDX_DIM), BF16),
        ],
        compiler_params=_params("arbitrary"),
        name="dsa_norm_rope",
    )(proj, tail, cos, sin_signed, q_gain, k_gain, ik_gain)


def _key_to_f32(key):
    bits = jnp.where(key < 0, key ^ INT32_MIN, ~key)
    return lax.bitcast_convert_type(bits, F32)


def _indexer_kernel(qi_ref, ki_ref, w_ref, bias_ref, sc_ref, cnt_ref, *, tq, tkc, tkb, nkc, topk):
    i = pl.program_id(1)
    t0 = i * tq
    w = w_ref[...] * (1.0 / math.sqrt(IDX_DIM) / math.sqrt(IDX_HEADS))
    key_pos = lax.broadcasted_iota(jnp.int32, (tkc, tq), 0)
    qry_pos = lax.broadcasted_iota(jnp.int32, (tkc, tq), 1) + t0

    def chunk_in_range(c):
        return c * tkc < t0 + tq

    for c in range(nkc):

        @pl.when(chunk_in_range(c))
        def _():
            k_blk = ki_ref[c * tkc:(c + 1) * tkc, :]
            acc = jnp.zeros((tkc, tq), F32)
            for h in range(IDX_HEADS):
                logit = lax.dot_general(
                    k_blk, qi_ref[:, h * IDX_DIM:(h + 1) * IDX_DIM], (((1,), (1,)), ((), ())),
                    preferred_element_type=F32)
                acc = acc + jnp.maximum(logit, 0.0) * w[h:h + 1, :]
            sc_ref[c] = jnp.where(key_pos + c * tkc <= qry_pos, acc, -jnp.inf)

        @pl.when(jnp.logical_not(chunk_in_range(c)))
        def _():
            sc_ref[c] = jnp.full((tkc, tq), -jnp.inf, F32)

    def sublane_partial(hit):
        return jnp.sum(hit.reshape(tkc // 8, 8, tq), axis=0)

    def count_ge(thr):
        cnt_ref[...] = sublane_partial(jnp.where(sc_ref[0] >= thr, 1.0, 0.0))
        for c in range(1, nkc):

            @pl.when(chunk_in_range(c))
            def _():
                cnt_ref[...] += sublane_partial(jnp.where(sc_ref[c] >= thr, 1.0, 0.0))

        return jnp.sum(cnt_ref[...], axis=0, keepdims=True)

    def store_bias(c, bias):
        for part in range(tkc // tkb):
            bias_ref[0, 0, c * (tkc // tkb) + part] = bias[part * tkb:(part + 1) * tkb]

    def write_bias(select_fn):
        for c in range(nkc):
            store_bias(c, jnp.where(select_fn(sc_ref[c]), 0.0, MASK_BIAS))

    @pl.when(t0 + tq <= topk)
    def _():
        write_bias(lambda sc: sc >= F32_LOWEST)

    @pl.when(t0 + tq > topk)
    def _():
        def bit_step(step, key):
            trial = key | lax.shift_left(jnp.int32(1), 31 - step)
            keep = count_ge(_key_to_f32(trial)) >= topk
            return jnp.where(keep, trial, key)

        key = lax.fori_loop(0, 32, bit_step, jnp.zeros((1, tq), jnp.int32))
        few = lax.broadcasted_iota(jnp.int32, (1, tq), 1) + t0 < topk - 1
        thr = jnp.where(few, F32_LOWEST, _key_to_f32(key))
        write_bias(lambda sc: sc >= thr)

        n_ge = count_ge(thr)

        @pl.when(jnp.max(jnp.where(few, 0.0, n_ge)) > topk)
        def _():
            n_gt = sum(jnp.sum(jnp.where(sc_ref[c] > thr, 1.0, 0.0), axis=0, keepdims=True)
                       for c in range(nkc))
            room = jnp.where(few, float(tkc * nkc), topk - n_gt)
            r2 = lax.broadcasted_iota(jnp.int32, (tkc, tkc), 0)
            c2 = lax.broadcasted_iota(jnp.int32, (tkc, tkc), 1)
            before = jnp.where(c2 < r2, 1.0, 0.0).astype(BF16)
            seen = jnp.zeros((1, tq), F32)
            for c in range(nkc):
                sc = sc_ref[c]
                tied = jnp.where(sc == thr, 1.0, 0.0)
                rank = seen + jnp.dot(before, tied.astype(BF16), preferred_element_type=F32)
                take = jnp.where(sc > thr, 1.0, jnp.where(rank < room, tied, 0.0))
                store_bias(c, jnp.where(take > 0.0, 0.0, MASK_BIAS))
                seen = seen + jnp.sum(tied, axis=0, keepdims=True)


def _dsa_indexer(qi, ki, w_t, batch, seq_len, topk, *, tq, tkb, tkc=512):
    nq = seq_len // tq
    nkc = seq_len // tkc
    nkb = seq_len // tkb
    return pl.pallas_call(
        functools.partial(_indexer_kernel, tq=tq, tkc=tkc, tkb=tkb, nkc=nkc, topk=topk),
        grid=(batch, nq),
        in_specs=[
            pl.BlockSpec((tq, IDX_HEADS * IDX_DIM), lambda b, i: (b * nq + i, 0)),
            pl.BlockSpec((seq_len, IDX_DIM), lambda b, i: (b, 0)),
            pl.BlockSpec((IDX_HEADS, tq), lambda b, i: (0, b * nq + i)),
        ],
        out_specs=pl.BlockSpec((1, 1, nkb, tkb, tq), lambda b, i: (b, i, 0, 0, 0)),
        out_shape=jax.ShapeDtypeStruct((batch, nq, nkb, tkb, tq), F32),
        scratch_shapes=[pltpu.VMEM((nkc, tkc, tq), F32), pltpu.VMEM((8, tq), F32)],
        compiler_params=_params("arbitrary", "arbitrary"),
        name="dsa_indexer_topk",
    )(qi, ki, w_t)


def _dsa_attn_kernel(q_ref, k_ref, vt_ref, bias_ref, o_ref, *, tq, tkb, groups, chain_heads):
    i = pl.program_id(1)
    n_chunks = ((i + 1) * tq + tkb - 1) // tkb
    cols = chain_heads * tq
    chains = [(g, r0) for g in range(groups) for r0 in range(0, DSA_GROUP, chain_heads)]

    def head_lanes(g, r):
        h = g * DSA_GROUP + r
        return slice(h * HEAD_DIM, (h + 1) * HEAD_DIM)

    qs = [jnp.concatenate([q_ref[:, head_lanes(g, r0 + r)] for r in range(chain_heads)], axis=0)
          for g, r0 in chains]

    def over_keys(x, op):
        return op(op(x.reshape(tkb // 8, 8, cols), axis=0), axis=0, keepdims=True)

    def kv_lanes(n):
        g, _ = chains[n]
        return slice(g * HEAD_DIM, (g + 1) * HEAD_DIM)

    def scores(n, c):
        k_blk = k_ref[pl.ds(pl.multiple_of(c * tkb, tkb), tkb), kv_lanes(n)]
        return lax.dot_general(k_blk, qs[n], (((1,), (1,)), ((), ())), preferred_element_type=F32)

    def softmax_step(n, c, state, raw, bias):
        m_run, l_run, acc = state
        s = bias + raw
        m_new = jnp.maximum(m_run, over_keys(s, jnp.max))
        alpha = jnp.exp2(m_run - m_new)
        p = jnp.exp2(s - m_new)
        l_run = alpha * l_run + over_keys(p, jnp.sum)
        acc = alpha * acc + jnp.dot(vt_ref[0, c, kv_lanes(n), :], p.astype(BF16),
                                    preferred_element_type=F32)
        return m_new, l_run, acc

    def body(c, states):
        bias = jnp.concatenate([bias_ref[0, 0, c]] * chain_heads, axis=1)
        raws = [scores(n, c) for n in range(len(chains))]
        return [softmax_step(n, c, states[n], raws[n], bias) for n in range(len(chains))]

    init = (jnp.full((1, cols), -jnp.inf, F32), jnp.zeros((1, cols), F32),
            jnp.zeros((HEAD_DIM, cols), F32))
    states = lax.fori_loop(0, n_chunks, body, [init] * len(chains))
    for (g, r0), (_, l_run, acc) in zip(chains, states):
        out_t = acc / l_run
        for r in range(chain_heads):
            o_ref[:, head_lanes(g, r0 + r)] = out_t[:, r * tq:(r + 1) * tq].T.astype(o_ref.dtype)


def _dsa_attention(q, k, v_t, bias, batch, seq_len, *, tq, tkb, groups=2, chain_heads=2):
    nq = seq_len // tq
    gw = groups * DSA_GROUP * HEAD_DIM
    kw = groups * HEAD_DIM
    nkb = seq_len // tkb
    return pl.pallas_call(
        functools.partial(_dsa_attn_kernel, tq=tq, tkb=tkb, groups=groups, chain_heads=chain_heads),
        grid=(batch, nq, DSA_KV_HEADS // groups),
        in_specs=[
            pl.BlockSpec((tq, gw), lambda b, i, g: (b * nq + i, g)),
            pl.BlockSpec((seq_len, kw), lambda b, i, g: (b, g)),
            pl.BlockSpec((1, nkb, kw, tkb), lambda b, i, g: (b, 0, g, 0)),
            pl.BlockSpec((1, 1, nkb, tkb, tq), lambda b, i, g: (b, i, 0, 0, 0)),
        ],
        out_specs=pl.BlockSpec((tq, gw), lambda b, i, g: (b * nq + i, g)),
        out_shape=jax.ShapeDtypeStruct((batch * seq_len, N_HEADS * HEAD_DIM), BF16),
        compiler_params=_params("arbitrary", "arbitrary", "arbitrary"),
        name="dsa_attention",
    )(q, k, v_t, bias)


def _rope_tables(seq_len, dim):
    inv_freq = 1.0 / (ROPE_THETA ** (jnp.arange(0, dim, 2, dtype=F32) / dim))
    ang = jnp.arange(seq_len, dtype=F32)[:, None] * inv_freq[None, :]
    cos, sin = jnp.cos(ang), jnp.sin(ang)
    return jnp.concatenate([cos, cos], axis=-1), jnp.concatenate([-sin, sin], axis=-1)


def _conv_ffn(x, norm_g, w_up, conv_w, conv_b, w_down, layer, seq_len):
    h = _rmsnorm(x, norm_g, layer)
    gated = _ffn_up(h, w_up, conv_w, conv_b, layer, seq_len)
    return _matmul(gated, w_down, layer, x.shape[1], tm=512, tn=1024, out_dtype=F32, res=x,
                   name="ffn_down_residual")


def kernel(x, attn_norm_g, ffn_norm_g, sb_w_qkv, sb_w_o, dsa_w_in, dsa_q_norm_g, dsa_k_norm_g,
           dsa_ik_norm_g, dsa_w_o, ffn_w_up, ffn_conv_w, ffn_conv_b, ffn_w_down):
    batch, seq_len, d_model = x.shape
    m = batch * seq_len
    hd = N_HEADS * HEAD_DIM
    x = x.reshape(m, d_model)
    attn_g = attn_norm_g[:, None, :]
    ffn_g = ffn_norm_g[:, None, :]
    conv_b = ffn_conv_b[:, None, :]
    w_down = _to_bf16(ffn_w_down)

    h = _rmsnorm(x, attn_g, 0)
    qkv = _matmul(h, sb_w_qkv, 0, 3 * hd, tm=1024, tn=1024, out_dtype=BF16, scaled_cols=hd,
                  col_scale=1.0 / math.sqrt(HEAD_DIM), name="sb_qkv_proj")
    mixed = _sb_attention(qkv.reshape(batch, seq_len, 3 * hd)).reshape(m, hd)
    x = _matmul(mixed, sb_w_o, 0, d_model, tm=1024, tn=1024, out_dtype=F32, res=x, name="sb_out_residual")
    x = _conv_ffn(x, ffn_g, ffn_w_up, ffn_conv_w, conv_b, w_down, 0, seq_len)

    n_main = 2 * hd + 2 * DSA_KV_HEADS * HEAD_DIM
    h = _rmsnorm(x, attn_g, 1)
    w_in_t = jnp.swapaxes(dsa_w_in, 1, 2)
    proj = _matmul(h, w_in_t, 0, n_main, tm=1024, tn=1024, out_dtype=F32, w_is_transposed=True,
                   name="dsa_in_proj")
    w_tail_t = w_in_t[:, n_main:, :]
    tail = _matmul(h, w_tail_t, 0, w_tail_t.shape[1], tm=1024, tn=w_tail_t.shape[1], out_dtype=F32,
                   w_is_transposed=True, name="dsa_in_proj_tail")
    cos, sin_signed = _rope_tables(seq_len, HEAD_DIM)
    tq, tkb = 256, 512
    q, k, v_t, qi, ki = _dsa_prep(proj, tail, cos, sin_signed, dsa_q_norm_g[:, None, :],
                                  dsa_k_norm_g[:, None, :], dsa_ik_norm_g[:, None, :], 0, batch, seq_len,
                                  tkc=tkb)
    topk = min(TOPK_MAX, seq_len // 4)
    w_t = tail[:, IDX_DIM:].T
    bias = _dsa_indexer(qi, ki, w_t, batch, seq_len, topk, tq=tq, tkb=tkb)
    mixed = _dsa_attention(q, k, v_t, bias, batch, seq_len, tq=tq, tkb=tkb)
    x = _matmul(mixed, dsa_w_o, 0, d_model, tm=1024, tn=1024, out_dtype=F32, res=x, name="dsa_out_residual")
    x = _conv_ffn(x, ffn_g, ffn_w_up, ffn_conv_w, conv_b, w_down, 1, seq_len)
    return x.reshape(batch, seq_len, d_model)
```

```python
import functools
import math

import jax
import jax.numpy as jnp
from jax import lax
from jax.experimental import pallas as pl
from jax.experimental.pallas import tpu as pltpu

N_HEADS = 16
HEAD_DIM = 128
DSA_KV_HEADS = 4
DSA_GROUP = N_HEADS // DSA_KV_HEADS
IDX_HEADS = 16
IDX_DIM = 128
TOPK_MAX = 256
D_FF = 5632
CONV_WIDTH = 3
ROPE_THETA = 10000.0
NORM_EPS = 1e-6

V7X_VMEM_BYTES = 64 * 1024 * 1024
VMEM_LIMIT_BYTES = V7X_VMEM_BYTES - 8 * 1024 * 1024

F32 = jnp.float32
BF16 = jnp.bfloat16
MASK_BIAS = -1e30
F32_LOWEST = float(jnp.finfo(jnp.float32).min)
INT32_MIN = -(2 ** 31)
LOG2E = math.log2(math.e)
DSA_Q_SCALE = LOG2E / math.sqrt(HEAD_DIM)


def _params(*sem, flags=None):
    return pltpu.CompilerParams(dimension_semantics=sem, vmem_limit_bytes=VMEM_LIMIT_BYTES, flags=flags)


def _rmsnorm_kernel(x_ref, g_ref, o_ref):
    x = x_ref[...]
    ms = jnp.mean(x * x, axis=-1, keepdims=True)
    o_ref[...] = ((x * lax.rsqrt(ms + NORM_EPS)) * g_ref[0]).astype(o_ref.dtype)


def _rmsnorm(x, g_all, layer, *, tm=512):
    m, d = x.shape
    return pl.pallas_call(
        _rmsnorm_kernel,
        grid=(m // tm,),
        in_specs=[
            pl.BlockSpec((tm, d), lambda i: (i, 0)),
            pl.BlockSpec((1, 1, d), lambda i: (layer, 0, 0)),
        ],
        out_specs=pl.BlockSpec((tm, d), lambda i: (i, 0)),
        out_shape=jax.ShapeDtypeStruct((m, d), BF16),
        compiler_params=_params("arbitrary"),
        name="rmsnorm",
    )(x, g_all)


def _matmul_kernel(*refs, has_res, cast_w, w_is_transposed, scaled_blocks, block_scale):
    a_ref, w_ref, *rest = refs
    r_ref = rest.pop(0) if has_res else None
    o_ref = rest.pop(0)
    if cast_w:
        (w_bf,) = rest

        @pl.when(pl.program_id(1) == 0)
        def _():
            w = w_ref[...]
            w_bf[...] = (w.T if w_is_transposed else w).astype(BF16)

        w = w_bf[...]
    else:
        w = w_ref[...]
    out = jnp.dot(a_ref[...], w, preferred_element_type=F32)
    if scaled_blocks:
        out = out * jnp.where(pl.program_id(0) < scaled_blocks, block_scale, 1.0)
    if has_res:
        out = r_ref[...] + out
    o_ref[...] = out.astype(o_ref.dtype)


def _matmul(a, w_all, layer, n_out, *, tm, tn, out_dtype, res=None, w_is_transposed=False,
            scaled_cols=0, col_scale=1.0, name):
    m, k_dim = a.shape
    cast_w = w_all.dtype != BF16
    assert cast_w or not w_is_transposed
    assert scaled_cols % tn == 0
    if w_is_transposed:
        w_spec = pl.BlockSpec((None, tn, k_dim), lambda j, i: (layer, j, 0))
    else:
        w_spec = pl.BlockSpec((None, k_dim, tn), lambda j, i: (layer, 0, j))
    in_specs = [pl.BlockSpec((tm, k_dim), lambda j, i: (i, 0)), w_spec]
    args = [a, w_all]
    if res is not None:
        in_specs.append(pl.BlockSpec((tm, tn), lambda j, i: (i, j)))
        args.append(res)
    return pl.pallas_call(
        functools.partial(_matmul_kernel, has_res=res is not None, cast_w=cast_w,
                          w_is_transposed=w_is_transposed, scaled_blocks=scaled_cols // tn,
                          block_scale=col_scale),
        grid=(n_out // tn, m // tm),
        in_specs=in_specs,
        out_specs=pl.BlockSpec((tm, tn), lambda j, i: (i, j)),
        out_shape=jax.ShapeDtypeStruct((m, n_out), out_dtype),
        scratch_shapes=[pltpu.VMEM((k_dim, tn), BF16)] if cast_w else [],
        compiler_params=_params("arbitrary", "arbitrary"),
        name=name,
    )(*args)


def _cast_kernel(x_ref, o_ref):
    o_ref[...] = x_ref[...].astype(o_ref.dtype)


def _to_bf16(w_all, *, rows=512):
    n_layers, k_dim, n = w_all.shape
    return pl.pallas_call(
        _cast_kernel,
        grid=(n_layers, k_dim // rows),
        in_specs=[pl.BlockSpec((1, rows, n), lambda l, r: (l, r, 0))],
        out_specs=pl.BlockSpec((1, rows, n), lambda l, r: (l, r, 0)),
        out_shape=jax.ShapeDtypeStruct(w_all.shape, BF16),
        compiler_params=_params("arbitrary", "arbitrary"),
        name="weights_to_bf16",
    )(w_all)


def _sb_tile(q, k_blk, v_blk, upper, carry, causal):
    z = lax.dot_general(q, k_blk, (((1,), (1,)), ((), ())), preferred_element_type=F32)
    sp = jnp.maximum(z, 0.0) + jnp.log(1.0 + jnp.exp2(jnp.abs(z) * (-LOG2E)))
    if causal is not None:
        sp = jnp.where(causal, sp, 0.0)
    tail = jnp.dot(sp.astype(BF16), upper, preferred_element_type=F32)
    a = jnp.exp(z - (tail + carry))
    if causal is not None:
        a = jnp.where(causal, a, 0.0)
    out = jnp.dot(a.astype(BF16), v_blk, preferred_element_type=F32)
    return out, carry + jnp.sum(sp, axis=1, keepdims=True)


def _sb_attn_kernel(q_ref, k_ref, v_ref, o_ref, *, tq, tk, heads):
    i = pl.program_id(2)
    per_q = tq // tk
    r2 = lax.broadcasted_iota(jnp.int32, (tk, tk), 0)
    c2 = lax.broadcasted_iota(jnp.int32, (tk, tk), 1)
    upper = jnp.where(r2 >= c2, 1.0, 0.0).astype(BF16)
    rows = lax.broadcasted_iota(jnp.int32, (tq, tk), 0)
    cols = lax.broadcasted_iota(jnp.int32, (tq, tk), 1)
    lanes = [slice(h * HEAD_DIM, (h + 1) * HEAD_DIM) for h in range(heads)]

    def block(j, carries, causal, first_row=0):
        start = pl.multiple_of(j * tk, tk)
        outs = [_sb_tile(q_ref[0, first_row:, lanes[h]], k_ref[0, pl.ds(start, tk), lanes[h]],
                         v_ref[0, pl.ds(start, tk), lanes[h]], upper, carries[h], causal)
                for h in range(heads)]
        return [o for o, _ in outs], [c for _, c in outs]

    carries = [jnp.zeros((tq, 1), F32)] * heads
    accs = [jnp.zeros((tq, HEAD_DIM), F32)] * heads
    for d in reversed(range(per_q)):
        r0 = d * tk
        outs, low = block(i * per_q + d, [c[r0:] for c in carries], (cols + r0 < rows)[r0:], r0)
        if r0:
            carries = [jnp.concatenate([c[:r0], lo], axis=0) for c, lo in zip(carries, low)]
            accs = [jnp.concatenate([acc[:r0], acc[r0:] + o], axis=0) for acc, o in zip(accs, outs)]
        else:
            carries = low
            accs = [acc + o for acc, o in zip(accs, outs)]

    def body(step, state):
        accs, carries = state
        for d in range(per_q):
            outs, carries = block((i - step) * per_q - 1 - d, carries, None)
            accs = [acc + o for acc, o in zip(accs, outs)]
        return accs, carries

    accs, _ = lax.fori_loop(0, i, body, (accs, carries))
    for h in range(heads):
        o_ref[0, :, lanes[h]] = accs[h].astype(o_ref.dtype)


def _sb_attention(qkv, *, tq=512, tk=256, heads=4):
    b, s, _ = qkv.shape
    groups = N_HEADS // heads
    width = heads * HEAD_DIM
    return pl.pallas_call(
        functools.partial(_sb_attn_kernel, tq=tq, tk=tk, heads=heads),
        grid=(b, groups, s // tq),
        in_specs=[
            pl.BlockSpec((1, tq, width), lambda bi, g, i: (bi, i, g)),
            pl.BlockSpec((1, s, width), lambda bi, g, i: (bi, 0, groups + g)),
            pl.BlockSpec((1, s, width), lambda bi, g, i: (bi, 0, 2 * groups + g)),
        ],
        out_specs=pl.BlockSpec((1, tq, width), lambda bi, g, i: (bi, i, g)),
        out_shape=jax.ShapeDtypeStruct((b, s, N_HEADS * HEAD_DIM), BF16),
        compiler_params=_params("arbitrary", "arbitrary", "arbitrary"),
        name="sb_attention",
    )(qkv, qkv, qkv)


HISTORY_ROWS = 8


def _causal_conv(u_ref, cw, cb):
    r0 = HISTORY_ROWS
    rows = u_ref.shape[0] - r0
    return (cb + cw[2:3] * u_ref[r0:r0 + rows] + cw[1:2] * u_ref[r0 - 1:r0 - 1 + rows]
            + cw[0:1] * u_ref[r0 - 2:r0 - 2 + rows])


def _ffn_up_kernel(h_ref, wg_ref, wv_ref, cwg_ref, cwv_ref, cbg_ref, cbv_ref, o_ref,
                   wg_bf, wv_bf, hist_g, hist_v, *stage, tiles_per_seq, chunk_rows, chunk_cols):
    i = pl.program_id(1)
    tm, tn = o_ref.shape
    r0 = HISTORY_ROWS

    @pl.when(i == 0)
    def _():
        wg_bf[...] = wg_ref[...].astype(BF16)
        wv_bf[...] = wv_ref[...].astype(BF16)

    @pl.when(i % tiles_per_seq == 0)
    def _():
        hist_g[...] = jnp.zeros_like(hist_g)
        hist_v[...] = jnp.zeros_like(hist_v)

    piece = 0
    for c0 in range(0, tn, chunk_cols):
        cols = slice(c0, c0 + chunk_cols)
        prev_g = hist_g[:, cols]
        prev_v = hist_v[:, cols]
        for t0 in range(0, tm, chunk_rows):
            ug_s, uv_s = stage[2 * (piece % 2)], stage[2 * (piece % 2) + 1]
            piece += 1
            h = h_ref[t0:t0 + chunk_rows, :]
            ug_s[0:r0] = prev_g
            uv_s[0:r0] = prev_v
            ug_s[r0:] = jnp.dot(h, wg_bf[:, cols], preferred_element_type=F32)
            uv_s[r0:] = jnp.dot(h, wv_bf[:, cols], preferred_element_type=F32)
            gate = _causal_conv(ug_s, cwg_ref[:, cols], cbg_ref[0, :, cols])
            val = _causal_conv(uv_s, cwv_ref[:, cols], cbv_ref[0, :, cols])
            o_ref[t0:t0 + chunk_rows, cols] = (jax.nn.silu(gate) * val).astype(o_ref.dtype)
            prev_g = ug_s[chunk_rows:]
            prev_v = uv_s[chunk_rows:]
        hist_g[:, cols] = prev_g
        hist_v[:, cols] = prev_v


def _ffn_up(h, w_up_all, conv_w_all, conv_b_all, layer, seq_len, *, tm=2048, tn=512,
            chunk_rows=512, chunk_cols=256):
    m, d = h.shape
    nj = D_FF // tn
    return pl.pallas_call(
        functools.partial(_ffn_up_kernel, tiles_per_seq=seq_len // tm, chunk_rows=chunk_rows,
                          chunk_cols=chunk_cols),
        grid=(nj, m // tm),
        in_specs=[
            pl.BlockSpec((tm, d), lambda j, i: (i, 0)),
            pl.BlockSpec((None, d, tn), lambda j, i: (layer, 0, j)),
            pl.BlockSpec((None, d, tn), lambda j, i: (layer, 0, nj + j)),
            pl.BlockSpec((None, CONV_WIDTH, tn), lambda j, i: (layer, 0, j)),
            pl.BlockSpec((None, CONV_WIDTH, tn), lambda j, i: (layer, 0, nj + j)),
            pl.BlockSpec((1, 1, tn), lambda j, i: (layer, 0, j)),
            pl.BlockSpec((1, 1, tn), lambda j, i: (layer, 0, nj + j)),
        ],
        out_specs=pl.BlockSpec((tm, tn), lambda j, i: (i, j)),
        out_shape=jax.ShapeDtypeStruct((m, D_FF), BF16),
        scratch_shapes=[
            pltpu.VMEM((d, tn), BF16),
            pltpu.VMEM((d, tn), BF16),
            pltpu.VMEM((HISTORY_ROWS, tn), F32),
            pltpu.VMEM((HISTORY_ROWS, tn), F32),
        ] + [pltpu.VMEM((HISTORY_ROWS + chunk_rows, chunk_cols), F32)] * 4,
        compiler_params=_params("arbitrary", "arbitrary"),
        name="ffn_up_conv_gate",
    )(h, w_up_all, w_up_all, conv_w_all, conv_w_all, conv_b_all, conv_b_all)


def _head_norm_rope(x, gain, cos, sin_signed):
    if gain is not None:
        ms = jnp.mean(x * x, axis=-1, keepdims=True)
        x = (x * lax.rsqrt(ms + NORM_EPS)) * gain
    return x * cos + pltpu.roll(x, HEAD_DIM // 2, 1) * sin_signed


def _dsa_prep_kernel(p_ref, t_ref, cos_ref, sin_ref, qg_ref, kg_ref, ig_ref,
                     q_ref, k_ref, v_ref, qi_ref, ki_ref):
    cos = cos_ref[...]
    sin = sin_ref[...]
    o_k = N_HEADS * HEAD_DIM
    o_v = o_k + DSA_KV_HEADS * HEAD_DIM
    o_qi = o_v + DSA_KV_HEADS * HEAD_DIM
    for h in range(N_HEADS):
        sl = slice(h * HEAD_DIM, (h + 1) * HEAD_DIM)
        q_ref[:, sl] = (_head_norm_rope(p_ref[:, sl], qg_ref[...], cos, sin) * DSA_Q_SCALE).astype(BF16)
        qi_ref[:, sl] = _head_norm_rope(
            p_ref[:, o_qi + h * IDX_DIM:o_qi + (h + 1) * IDX_DIM], None, cos, sin).astype(BF16)
    for g in range(DSA_KV_HEADS):
        sl = slice(g * HEAD_DIM, (g + 1) * HEAD_DIM)
        k_ref[:, sl] = _head_norm_rope(
            p_ref[:, o_k + g * HEAD_DIM:o_k + (g + 1) * HEAD_DIM], kg_ref[...], cos, sin).astype(BF16)
    v_ref[...] = p_ref[:, o_v:o_qi].T.astype(BF16)
    ki_ref[...] = _head_norm_rope(t_ref[:, 0:IDX_DIM], ig_ref[...], cos, sin).astype(BF16)


def _dsa_prep(proj, tail, cos, sin_signed, q_gain, k_gain, ik_gain, layer, batch, seq_len, *,
              tm=256, tkc=512):
    m = proj.shape[0]
    n_main = proj.shape[1]
    n_tail = tail.shape[1]
    tiles_per_seq = seq_len // tm
    tiles_per_chunk = tkc // tm
    hq = N_HEADS * HEAD_DIM
    hk = DSA_KV_HEADS * HEAD_DIM
    row = lambda i: (i, 0)

    def v_block(i):
        pos_tile = i % tiles_per_seq
        return i // tiles_per_seq, pos_tile // tiles_per_chunk, 0, pos_tile % tiles_per_chunk
    gain_spec = pl.BlockSpec((None, 1, HEAD_DIM), lambda i: (layer, 0, 0))
    tab_spec = pl.BlockSpec((tm, HEAD_DIM), lambda i: (i % tiles_per_seq, 0))
    return pl.pallas_call(
        _dsa_prep_kernel,
        grid=(m // tm,),
        in_specs=[
            pl.BlockSpec((tm, n_main), row),
            pl.BlockSpec((tm, n_tail), row),
            tab_spec, tab_spec, gain_spec, gain_spec, gain_spec,
        ],
        out_specs=[
            pl.BlockSpec((tm, hq), row),
            pl.BlockSpec((tm, hk), row),
            pl.BlockSpec((None, None, hk, tm), v_block),
            pl.BlockSpec((tm, IDX_HEADS * IDX_DIM), row),
            pl.BlockSpec((tm, IDX_DIM), row),
        ],
        out_shape=[
            jax.ShapeDtypeStruct((m, hq), BF16),
            jax.ShapeDtypeStruct((m, hk), BF16),
            jax.ShapeDtypeStruct((batch, seq_len // tkc, hk, tkc), BF16),
            jax.ShapeDtypeStruct((m, IDX_HEADS * IDX_DIM), BF16),
            jax.ShapeDtypeStruct((m, IDX_DIM), BF16),
        ],
        compiler_params=_params("arbitrary"),
        name="dsa_norm_rope",
    )(proj, tail, cos, sin_signed, q_gain, k_gain, ik_gain)


def _key_to_f32(key):
    bits = jnp.where(key < 0, key ^ INT32_MIN, ~key)
    return lax.bitcast_convert_type(bits, F32)


def _high_half(x):
    bits = lax.bitcast_convert_type(x, jnp.int32) & -65536
    return lax.bitcast_convert_type(bits, F32).astype(BF16)


def _indexer_kernel(qi_ref, ki_ref, w_ref, bias_ref, sc_ref, hi_ref, cnt_ref, cnt_hi_ref, *,
                    tq, tkc, tkb, nkc, topk):
    i = pl.program_id(1)
    t0 = i * tq
    w = w_ref[...] * (1.0 / math.sqrt(IDX_DIM) / math.sqrt(IDX_HEADS))
    key_pos = lax.broadcasted_iota(jnp.int32, (tkc, tq), 0)
    qry_pos = lax.broadcasted_iota(jnp.int32, (tkc, tq), 1) + t0

    def chunk_in_range(c):
        return c * tkc < t0 + tq

    for c in range(nkc):

        @pl.when(chunk_in_range(c))
        def _():
            k_blk = ki_ref[c * tkc:(c + 1) * tkc, :]
            acc = jnp.zeros((tkc, tq), F32)
            for h in range(IDX_HEADS):
                logit = lax.dot_general(
                    k_blk, qi_ref[:, h * IDX_DIM:(h + 1) * IDX_DIM], (((1,), (1,)), ((), ())),
                    preferred_element_type=F32)
                acc = acc + jnp.maximum(logit, 0.0) * w[h:h + 1, :]
            score = jnp.where(key_pos + c * tkc <= qry_pos, acc, -jnp.inf)
            sc_ref[c] = score
            hi_ref[c] = _high_half(score)

        @pl.when(jnp.logical_not(chunk_in_range(c)))
        def _():
            sc_ref[c] = jnp.full((tkc, tq), -jnp.inf, F32)
            hi_ref[c] = jnp.full((tkc, tq), -jnp.inf, BF16)

    def sublane_partial(hit):
        return jnp.sum(hit.reshape(tkc // 8, 8, tq), axis=0)

    def count_ge(thr):
        cnt_ref[...] = sublane_partial(jnp.where(sc_ref[0] >= thr, 1.0, 0.0))
        for c in range(1, nkc):

            @pl.when(chunk_in_range(c))
            def _():
                cnt_ref[...] += sublane_partial(jnp.where(sc_ref[c] >= thr, 1.0, 0.0))

        return jnp.sum(cnt_ref[...], axis=0, keepdims=True)

    def count_hi_ge(thr_hi):
        one, zero = jnp.ones((), BF16), jnp.zeros((), BF16)

        def partial(c):
            hit = jnp.where(hi_ref[c] >= thr_hi, one, zero)
            return functools.reduce(lambda a, b: a + b, [hit[r:r + 16] for r in range(0, tkc, 16)])

        cnt_hi_ref[...] = partial(0)
        for c in range(1, nkc):

            @pl.when(chunk_in_range(c))
            def _():
                cnt_hi_ref[...] += partial(c)

        return jnp.sum(cnt_hi_ref[...].astype(F32), axis=0, keepdims=True)

    def store_bias(c, bias):
        for part in range(tkc // tkb):
            bias_ref[0, 0, c * (tkc // tkb) + part] = bias[part * tkb:(part + 1) * tkb]

    def write_bias(select_fn):
        for c in range(nkc):
            store_bias(c, jnp.where(select_fn(sc_ref[c]), 0.0, MASK_BIAS))

    @pl.when(t0 + tq <= topk)
    def _():
        write_bias(lambda sc: sc >= F32_LOWEST)

    @pl.when(t0 + tq > topk)
    def _():
        def bit_step(count_fn, step, key):
            trial = key | lax.shift_left(jnp.int32(1), 31 - step)
            keep = count_fn(_key_to_f32(trial)) >= topk
            return jnp.where(keep, trial, key)

        key = lax.fori_loop(0, 16, functools.partial(bit_step, lambda thr: count_hi_ge(_high_half(thr))),
                            jnp.zeros((1, tq), jnp.int32))
        key = lax.fori_loop(16, 32, functools.partial(bit_step, count_ge), key)
        few = lax.broadcasted_iota(jnp.int32, (1, tq), 1) + t0 < topk - 1
        thr = jnp.where(few, F32_LOWEST, _key_to_f32(key))
        write_bias(lambda sc: sc >= thr)

        n_ge = count_ge(thr)

        @pl.when(jnp.max(jnp.where(few, 0.0, n_ge)) > topk)
        def _():
            n_gt = sum(jnp.sum(jnp.where(sc_ref[c] > thr, 1.0, 0.0), axis=0, keepdims=True)
                       for c in range(nkc))
            room = jnp.where(few, float(tkc * nkc), topk - n_gt)
            r2 = lax.broadcasted_iota(jnp.int32, (tkc, tkc), 0)
            c2 = lax.broadcasted_iota(jnp.int32, (tkc, tkc), 1)
            before = jnp.where(c2 < r2, 1.0, 0.0).astype(BF16)
            seen = jnp.zeros((1, tq), F32)
            for c in range(nkc):
                sc = sc_ref[c]
                tied = jnp.where(sc == thr, 1.0, 0.0)
                rank = seen + jnp.dot(before, tied.astype(BF16), preferred_element_type=F32)
                take = jnp.where(sc > thr, 1.0, jnp.where(rank < room, tied, 0.0))
                store_bias(c, jnp.where(take > 0.0, 0.0, MASK_BIAS))
                seen = seen + jnp.sum(tied, axis=0, keepdims=True)


def _dsa_indexer(qi, ki, w_t, batch, seq_len, topk, *, tq, tkb, tkc=512):
    nq = seq_len // tq
    nkc = seq_len // tkc
    nkb = seq_len // tkb
    return pl.pallas_call(
        functools.partial(_indexer_kernel, tq=tq, tkc=tkc, tkb=tkb, nkc=nkc, topk=topk),
        grid=(batch, nq),
        in_specs=[
            pl.BlockSpec((tq, IDX_HEADS * IDX_DIM), lambda b, i: (b * nq + i, 0)),
            pl.BlockSpec((seq_len, IDX_DIM), lambda b, i: (b, 0)),
            pl.BlockSpec((IDX_HEADS, tq), lambda b, i: (0, b * nq + i)),
        ],
        out_specs=pl.BlockSpec((1, 1, nkb, tkb, tq), lambda b, i: (b, i, 0, 0, 0)),
        out_shape=jax.ShapeDtypeStruct((batch, nq, nkb, tkb, tq), F32),
        scratch_shapes=[pltpu.VMEM((nkc, tkc, tq), F32), pltpu.VMEM((nkc, tkc, tq), BF16),
                        pltpu.VMEM((8, tq), F32), pltpu.VMEM((16, tq), BF16)],
        compiler_params=_params("arbitrary", "arbitrary"),
        name="dsa_indexer_topk",
    )(qi, ki, w_t)


def _dsa_attn_kernel(q_ref, k_ref, vt_ref, bias_ref, o_ref, *, tq, tkb, groups, chain_heads):
    i = pl.program_id(1)
    n_chunks = ((i + 1) * tq + tkb - 1) // tkb
    cols = chain_heads * tq
    chains = [(g, r0) for g in range(groups) for r0 in range(0, DSA_GROUP, chain_heads)]

    def head_lanes(g, r):
        h = g * DSA_GROUP + r
        return slice(h * HEAD_DIM, (h + 1) * HEAD_DIM)

    qs = [jnp.concatenate([q_ref[:, head_lanes(g, r0 + r)] for r in range(chain_heads)], axis=0)
          for g, r0 in chains]

    def over_keys(x, op):
        return op(op(x.reshape(tkb // 8, 8, cols), axis=0), axis=0, keepdims=True)

    def kv_lanes(n):
        g, _ = chains[n]
        return slice(g * HEAD_DIM, (g + 1) * HEAD_DIM)

    def scores(n, c):
        k_blk = k_ref[pl.ds(pl.multiple_of(c * tkb, tkb), tkb), kv_lanes(n)]
        return lax.dot_general(k_blk, qs[n], (((1,), (1,)), ((), ())), preferred_element_type=F32)

    def softmax_step(n, c, state, raw, bias):
        m_run, l_run, acc = state
        s = bias + raw
        m_new = jnp.maximum(m_run, over_keys(s, jnp.max))
        alpha = jnp.exp2(m_run - m_new)
        p = jnp.exp2(s - m_new)
        l_run = alpha * l_run + over_keys(p, jnp.sum)
        acc = alpha * acc + jnp.dot(vt_ref[0, c, kv_lanes(n), :], p.astype(BF16),
                                    preferred_element_type=F32)
        return m_new, l_run, acc

    def body(c, states):
        bias = jnp.concatenate([bias_ref[0, 0, c]] * chain_heads, axis=1)
        raws = [scores(n, c) for n in range(len(chains))]
        return [softmax_step(n, c, states[n], raws[n], bias) for n in range(len(chains))]

    init = (jnp.full((1, cols), -jnp.inf, F32), jnp.zeros((1, cols), F32),
            jnp.zeros((HEAD_DIM, cols), F32))
    states = lax.fori_loop(0, n_chunks, body, [init] * len(chains))
    for (g, r0), (_, l_run, acc) in zip(chains, states):
        out_t = acc / l_run
        for r in range(chain_heads):
            o_ref[:, head_lanes(g, r0 + r)] = out_t[:, r * tq:(r + 1) * tq].T.astype(o_ref.dtype)


def _dsa_attention(q, k, v_t, bias, batch, seq_len, *, tq, tkb, groups=2, chain_heads=2):
    nq = seq_len // tq
    gw = groups * DSA_GROUP * HEAD_DIM
    kw = groups * HEAD_DIM
    nkb = seq_len // tkb
    return pl.pallas_call(
        functools.partial(_dsa_attn_kernel, tq=tq, tkb=tkb, groups=groups, chain_heads=chain_heads),
        grid=(batch, nq, DSA_KV_HEADS // groups),
        in_specs=[
            pl.BlockSpec((tq, gw), lambda b, i, g: (b * nq + i, g)),
            pl.BlockSpec((seq_len, kw), lambda b, i, g: (b, g)),
            pl.BlockSpec((1, nkb, kw, tkb), lambda b, i, g: (b, 0, g, 0)),
            pl.BlockSpec((1, 1, nkb, tkb, tq), lambda b, i, g: (b, i, 0, 0, 0)),
        ],
        out_specs=pl.BlockSpec((tq, gw), lambda b, i, g: (b * nq + i, g)),
        out_shape=jax.ShapeDtypeStruct((batch * seq_len, N_HEADS * HEAD_DIM), BF16),
        compiler_params=_params("arbitrary", "arbitrary", "arbitrary"),
        name="dsa_attention",
    )(q, k, v_t, bias)


def _rope_tables(seq_len, dim):
    inv_freq = 1.0 / (ROPE_THETA ** (jnp.arange(0, dim, 2, dtype=F32) / dim))
    ang = jnp.arange(seq_len, dtype=F32)[:, None] * inv_freq[None, :]
    cos, sin = jnp.cos(ang), jnp.sin(ang)
    return jnp.concatenate([cos, cos], axis=-1), jnp.concatenate([-sin, sin], axis=-1)


def _conv_ffn(x, norm_g, w_up, conv_w, conv_b, w_down, layer, seq_len):
    h = _rmsnorm(x, norm_g, layer)
    gated = _ffn_up(h, w_up, conv_w, conv_b, layer, seq_len)
    return _matmul(gated, w_down, layer, x.shape[1], tm=512, tn=1024, out_dtype=F32, res=x,
                   name="ffn_down_residual")


def kernel(x, attn_norm_g, ffn_norm_g, sb_w_qkv, sb_w_o, dsa_w_in, dsa_q_norm_g, dsa_k_norm_g,
           dsa_ik_norm_g, dsa_w_o, ffn_w_up, ffn_conv_w, ffn_conv_b, ffn_w_down):
    batch, seq_len, d_model = x.shape
    m = batch * seq_len
    hd = N_HEADS * HEAD_DIM
    x = x.reshape(m, d_model)
    attn_g = attn_norm_g[:, None, :]
    ffn_g = ffn_norm_g[:, None, :]
    conv_b = ffn_conv_b[:, None, :]
    w_down = _to_bf16(ffn_w_down)

    h = _rmsnorm(x, attn_g, 0)
    qkv = _matmul(h, sb_w_qkv, 0, 3 * hd, tm=2048, tn=1024, out_dtype=BF16, scaled_cols=hd,
                  col_scale=1.0 / math.sqrt(HEAD_DIM), name="sb_qkv_proj")
    mixed = _sb_attention(qkv.reshape(batch, seq_len, 3 * hd)).reshape(m, hd)
    x = _matmul(mixed, sb_w_o, 0, d_model, tm=1024, tn=1024, out_dtype=F32, res=x, name="sb_out_residual")
    x = _conv_ffn(x, ffn_g, ffn_w_up, ffn_conv_w, conv_b, w_down, 0, seq_len)

    n_main = 2 * hd + 2 * DSA_KV_HEADS * HEAD_DIM
    h = _rmsnorm(x, attn_g, 1)
    w_in_t = jnp.swapaxes(dsa_w_in, 1, 2)
    proj = _matmul(h, w_in_t, 0, n_main, tm=1024, tn=1024, out_dtype=F32, w_is_transposed=True,
                   name="dsa_in_proj")
    w_tail_t = w_in_t[:, n_main:, :]
    tail = _matmul(h, w_tail_t, 0, w_tail_t.shape[1], tm=1024, tn=w_tail_t.shape[1], out_dtype=F32,
                   w_is_transposed=True, name="dsa_in_proj_tail")
    cos, sin_signed = _rope_tables(seq_len, HEAD_DIM)
    tq, tkb = 256, 512
    q, k, v_t, qi, ki = _dsa_prep(proj, tail, cos, sin_signed, dsa_q_norm_g[:, None, :],
                                  dsa_k_norm_g[:, None, :], dsa_ik_norm_g[:, None, :], 0, batch, seq_len,
                                  tkc=tkb)
    topk = min(TOPK_MAX, seq_len // 4)
    w_t = tail[:, IDX_DIM:].T
    bias = _dsa_indexer(qi, ki, w_t, batch, seq_len, topk, tq=tq, tkb=tkb)
    mixed = _dsa_attention(q, k, v_t, bias, batch, seq_len, tq=tq, tkb=tkb)
    x = _matmul(mixed, dsa_w_o, 0, d_model, tm=1024, tn=1024, out_dtype=F32, res=x, name="dsa_out_residual")
    x = _conv_ffn(x, ffn_g, ffn_w_up, ffn_conv_w, conv_b, w_down, 1, seq_len)
    return x.reshape(batch, seq_len, d_model)
```

```python
import functools
import math

import jax
import jax.numpy as jnp
from jax import lax
from jax.experimental import pallas as pl
from jax.experimental.pallas import tpu as pltpu

N_HEADS = 16
HEAD_DIM = 128
DSA_KV_HEADS = 4
DSA_GROUP = N_HEADS // DSA_KV_HEADS
IDX_HEADS = 16
IDX_DIM = 128
TOPK_MAX = 256
D_FF = 5632
CONV_WIDTH = 3
ROPE_THETA = 10000.0
NORM_EPS = 1e-6

V7X_VMEM_BYTES = 64 * 1024 * 1024
VMEM_LIMIT_BYTES = V7X_VMEM_BYTES - 8 * 1024 * 1024

F32 = jnp.float32
BF16 = jnp.bfloat16
MASK_BIAS = -1e30
F32_LOWEST = float(jnp.finfo(jnp.float32).min)
INT32_MIN = -(2 ** 31)
LOG2E = math.log2(math.e)
DSA_Q_SCALE = LOG2E / math.sqrt(HEAD_DIM)


def _params(*sem, flags=None):
    return pltpu.CompilerParams(dimension_semantics=sem, vmem_limit_bytes=VMEM_LIMIT_BYTES, flags=flags)


def _rmsnorm_kernel(x_ref, g_ref, o_ref):
    x = x_ref[...]
    ms = jnp.mean(x * x, axis=-1, keepdims=True)
    o_ref[...] = ((x * lax.rsqrt(ms + NORM_EPS)) * g_ref[0]).astype(o_ref.dtype)


def _rmsnorm(x, g_all, layer, *, tm=512):
    m, d = x.shape
    return pl.pallas_call(
        _rmsnorm_kernel,
        grid=(m // tm,),
        in_specs=[
            pl.BlockSpec((tm, d), lambda i: (i, 0)),
            pl.BlockSpec((1, 1, d), lambda i: (layer, 0, 0)),
        ],
        out_specs=pl.BlockSpec((tm, d), lambda i: (i, 0)),
        out_shape=jax.ShapeDtypeStruct((m, d), BF16),
        compiler_params=_params("arbitrary"),
        name="rmsnorm",
    )(x, g_all)


def _matmul_kernel(*refs, has_res, cast_w, w_is_transposed, scaled_blocks, block_scale):
    a_ref, w_ref, *rest = refs
    r_ref = rest.pop(0) if has_res else None
    o_ref = rest.pop(0)
    if cast_w:
        (w_bf,) = rest

        @pl.when(pl.program_id(1) == 0)
        def _():
            w = w_ref[...]
            w_bf[...] = (w.T if w_is_transposed else w).astype(BF16)

        w = w_bf[...]
    else:
        w = w_ref[...]
    out = jnp.dot(a_ref[...], w, preferred_element_type=F32)
    if scaled_blocks:
        out = out * jnp.where(pl.program_id(0) < scaled_blocks, block_scale, 1.0)
    if has_res:
        out = r_ref[...] + out
    o_ref[...] = out.astype(o_ref.dtype)


def _matmul(a, w_all, layer, n_out, *, tm, tn, out_dtype, res=None, w_is_transposed=False,
            scaled_cols=0, col_scale=1.0, name):
    m, k_dim = a.shape
    cast_w = w_all.dtype != BF16
    assert cast_w or not w_is_transposed
    assert scaled_cols % tn == 0
    if w_is_transposed:
        w_spec = pl.BlockSpec((None, tn, k_dim), lambda j, i: (layer, j, 0))
    else:
        w_spec = pl.BlockSpec((None, k_dim, tn), lambda j, i: (layer, 0, j))
    in_specs = [pl.BlockSpec((tm, k_dim), lambda j, i: (i, 0)), w_spec]
    args = [a, w_all]
    if res is not None:
        in_specs.append(pl.BlockSpec((tm, tn), lambda j, i: (i, j)))
        args.append(res)
    return pl.pallas_call(
        functools.partial(_matmul_kernel, has_res=res is not None, cast_w=cast_w,
                          w_is_transposed=w_is_transposed, scaled_blocks=scaled_cols // tn,
                          block_scale=col_scale),
        grid=(n_out // tn, m // tm),
        in_specs=in_specs,
        out_specs=pl.BlockSpec((tm, tn), lambda j, i: (i, j)),
        out_shape=jax.ShapeDtypeStruct((m, n_out), out_dtype),
        scratch_shapes=[pltpu.VMEM((k_dim, tn), BF16)] if cast_w else [],
        compiler_params=_params("arbitrary", "arbitrary"),
        name=name,
    )(*args)


def _cast_kernel(x_ref, o_ref):
    o_ref[...] = x_ref[...].astype(o_ref.dtype)


def _to_bf16(w_all, *, rows=512):
    n_layers, k_dim, n = w_all.shape
    return pl.pallas_call(
        _cast_kernel,
        grid=(n_layers, k_dim // rows),
        in_specs=[pl.BlockSpec((1, rows, n), lambda l, r: (l, r, 0))],
        out_specs=pl.BlockSpec((1, rows, n), lambda l, r: (l, r, 0)),
        out_shape=jax.ShapeDtypeStruct(w_all.shape, BF16),
        compiler_params=_params("arbitrary", "arbitrary"),
        name="weights_to_bf16",
    )(w_all)


def _sb_tile(q, k_blk, v_blk, upper, carry, causal):
    z = lax.dot_general(q, k_blk, (((1,), (1,)), ((), ())), preferred_element_type=F32)
    sp = jnp.maximum(z, 0.0) + jnp.log(1.0 + jnp.exp2(jnp.abs(z) * (-LOG2E)))
    if causal is not None:
        sp = jnp.where(causal, sp, 0.0)
    tail = jnp.dot(sp.astype(BF16), upper, preferred_element_type=F32)
    a = jnp.exp(z - (tail + carry))
    if causal is not None:
        a = jnp.where(causal, a, 0.0)
    out = jnp.dot(a.astype(BF16), v_blk, preferred_element_type=F32)
    return out, carry + jnp.sum(sp, axis=1, keepdims=True)


def _sb_attn_kernel(q_ref, k_ref, v_ref, o_ref, *, tq, tk, heads):
    i = pl.program_id(2)
    per_q = tq // tk
    r2 = lax.broadcasted_iota(jnp.int32, (tk, tk), 0)
    c2 = lax.broadcasted_iota(jnp.int32, (tk, tk), 1)
    upper = jnp.where(r2 >= c2, 1.0, 0.0).astype(BF16)
    rows = lax.broadcasted_iota(jnp.int32, (tq, tk), 0)
    cols = lax.broadcasted_iota(jnp.int32, (tq, tk), 1)
    lanes = [slice(h * HEAD_DIM, (h + 1) * HEAD_DIM) for h in range(heads)]

    def block(j, carries, causal, first_row=0):
        start = pl.multiple_of(j * tk, tk)
        outs = [_sb_tile(q_ref[0, first_row:, lanes[h]], k_ref[0, pl.ds(start, tk), lanes[h]],
                         v_ref[0, pl.ds(start, tk), lanes[h]], upper, carries[h], causal)
                for h in range(heads)]
        return [o for o, _ in outs], [c for _, c in outs]

    carries = [jnp.zeros((tq, 1), F32)] * heads
    accs = [jnp.zeros((tq, HEAD_DIM), F32)] * heads
    for d in reversed(range(per_q)):
        r0 = d * tk
        outs, low = block(i * per_q + d, [c[r0:] for c in carries], (cols + r0 < rows)[r0:], r0)
        if r0:
            carries = [jnp.concatenate([c[:r0], lo], axis=0) for c, lo in zip(carries, low)]
            accs = [jnp.concatenate([acc[:r0], acc[r0:] + o], axis=0) for acc, o in zip(accs, outs)]
        else:
            carries = low
            accs = [acc + o for acc, o in zip(accs, outs)]

    def body(step, state):
        accs, carries = state
        for d in range(per_q):
            outs, carries = block((i - step) * per_q - 1 - d, carries, None)
            accs = [acc + o for acc, o in zip(accs, outs)]
        return accs, carries

    accs, _ = lax.fori_loop(0, i, body, (accs, carries))
    for h in range(heads):
        o_ref[0, :, lanes[h]] = accs[h].astype(o_ref.dtype)


def _sb_attention(qkv, *, tq=512, tk=256, heads=4):
    b, s, _ = qkv.shape
    groups = N_HEADS // heads
    width = heads * HEAD_DIM
    return pl.pallas_call(
        functools.partial(_sb_attn_kernel, tq=tq, tk=tk, heads=heads),
        grid=(b, groups, s // tq),
        in_specs=[
            pl.BlockSpec((1, tq, width), lambda bi, g, i: (bi, i, g)),
            pl.BlockSpec((1, s, width), lambda bi, g, i: (bi, 0, groups + g)),
            pl.BlockSpec((1, s, width), lambda bi, g, i: (bi, 0, 2 * groups + g)),
        ],
        out_specs=pl.BlockSpec((1, tq, width), lambda bi, g, i: (bi, i, g)),
        out_shape=jax.ShapeDtypeStruct((b, s, N_HEADS * HEAD_DIM), BF16),
        compiler_params=_params("arbitrary", "arbitrary", "arbitrary"),
        name="sb_attention",
    )(qkv, qkv, qkv)


HISTORY_ROWS = 8


def _causal_conv(u_ref, cw, cb):
    r0 = HISTORY_ROWS
    rows = u_ref.shape[0] - r0
    return (cb + cw[2:3] * u_ref[r0:r0 + rows] + cw[1:2] * u_ref[r0 - 1:r0 - 1 + rows]
            + cw[0:1] * u_ref[r0 - 2:r0 - 2 + rows])


def _ffn_up_kernel(h_ref, wg_ref, wv_ref, cwg_ref, cwv_ref, cbg_ref, cbv_ref, o_ref,
                   wg_bf, wv_bf, hist_g, hist_v, *stage, tiles_per_seq, chunk_rows, chunk_cols):
    i = pl.program_id(1)
    tm, tn = o_ref.shape
    r0 = HISTORY_ROWS

    @pl.when(i == 0)
    def _():
        wg_bf[...] = wg_ref[...].astype(BF16)
        wv_bf[...] = wv_ref[...].astype(BF16)

    @pl.when(i % tiles_per_seq == 0)
    def _():
        hist_g[...] = jnp.zeros_like(hist_g)
        hist_v[...] = jnp.zeros_like(hist_v)

    piece = 0
    for c0 in range(0, tn, chunk_cols):
        cols = slice(c0, c0 + chunk_cols)
        prev_g = hist_g[:, cols]
        prev_v = hist_v[:, cols]
        for t0 in range(0, tm, chunk_rows):
            ug_s, uv_s = stage[2 * (piece % 2)], stage[2 * (piece % 2) + 1]
            piece += 1
            h = h_ref[t0:t0 + chunk_rows, :]
            ug_s[0:r0] = prev_g
            uv_s[0:r0] = prev_v
            ug_s[r0:] = jnp.dot(h, wg_bf[:, cols], preferred_element_type=F32)
            uv_s[r0:] = jnp.dot(h, wv_bf[:, cols], preferred_element_type=F32)
            gate = _causal_conv(ug_s, cwg_ref[:, cols], cbg_ref[0, :, cols])
            val = _causal_conv(uv_s, cwv_ref[:, cols], cbv_ref[0, :, cols])
            o_ref[t0:t0 + chunk_rows, cols] = (jax.nn.silu(gate) * val).astype(o_ref.dtype)
            prev_g = ug_s[chunk_rows:]
            prev_v = uv_s[chunk_rows:]
        hist_g[:, cols] = prev_g
        hist_v[:, cols] = prev_v


def _ffn_up(h, w_up_all, conv_w_all, conv_b_all, layer, seq_len, *, tm=2048, tn=512,
            chunk_rows=512, chunk_cols=256):
    m, d = h.shape
    nj = D_FF // tn
    return pl.pallas_call(
        functools.partial(_ffn_up_kernel, tiles_per_seq=seq_len // tm, chunk_rows=chunk_rows,
                          chunk_cols=chunk_cols),
        grid=(nj, m // tm),
        in_specs=[
            pl.BlockSpec((tm, d), lambda j, i: (i, 0)),
            pl.BlockSpec((None, d, tn), lambda j, i: (layer, 0, j)),
            pl.BlockSpec((None, d, tn), lambda j, i: (layer, 0, nj + j)),
            pl.BlockSpec((None, CONV_WIDTH, tn), lambda j, i: (layer, 0, j)),
            pl.BlockSpec((None, CONV_WIDTH, tn), lambda j, i: (layer, 0, nj + j)),
            pl.BlockSpec((1, 1, tn), lambda j, i: (layer, 0, j)),
            pl.BlockSpec((1, 1, tn), lambda j, i: (layer, 0, nj + j)),
        ],
        out_specs=pl.BlockSpec((tm, tn), lambda j, i: (i, j)),
        out_shape=jax.ShapeDtypeStruct((m, D_FF), BF16),
        scratch_shapes=[
            pltpu.VMEM((d, tn), BF16),
            pltpu.VMEM((d, tn), BF16),
            pltpu.VMEM((HISTORY_ROWS, tn), F32),
            pltpu.VMEM((HISTORY_ROWS, tn), F32),
        ] + [pltpu.VMEM((HISTORY_ROWS + chunk_rows, chunk_cols), F32)] * 4,
        compiler_params=_params("arbitrary", "arbitrary"),
        name="ffn_up_conv_gate",
    )(h, w_up_all, w_up_all, conv_w_all, conv_w_all, conv_b_all, conv_b_all)


def _head_norm_rope(x, gain, cos, sin_signed):
    if gain is not None:
        ms = jnp.mean(x * x, axis=-1, keepdims=True)
        x = (x * lax.rsqrt(ms + NORM_EPS)) * gain
    return x * cos + pltpu.roll(x, HEAD_DIM // 2, 1) * sin_signed


def _dsa_prep_kernel(p_ref, t_ref, cos_ref, sin_ref, qg_ref, kg_ref, ig_ref,
                     q_ref, k_ref, v_ref, qi_ref, ki_ref):
    cos = cos_ref[...]
    sin = sin_ref[...]
    o_k = N_HEADS * HEAD_DIM
    o_v = o_k + DSA_KV_HEADS * HEAD_DIM
    o_qi = o_v + DSA_KV_HEADS * HEAD_DIM
    for h in range(N_HEADS):
        sl = slice(h * HEAD_DIM, (h + 1) * HEAD_DIM)
        q_ref[:, sl] = (_head_norm_rope(p_ref[:, sl], qg_ref[...], cos, sin) * DSA_Q_SCALE).astype(BF16)
        qi_ref[:, sl] = _head_norm_rope(
            p_ref[:, o_qi + h * IDX_DIM:o_qi + (h + 1) * IDX_DIM], None, cos, sin).astype(BF16)
    for g in range(DSA_KV_HEADS):
        sl = slice(g * HEAD_DIM, (g + 1) * HEAD_DIM)
        k_ref[:, sl] = _head_norm_rope(
            p_ref[:, o_k + g * HEAD_DIM:o_k + (g + 1) * HEAD_DIM], kg_ref[...], cos, sin).astype(BF16)
    v_ref[...] = p_ref[:, o_v:o_qi].T.astype(BF16)
    ki_ref[...] = _head_norm_rope(t_ref[:, 0:IDX_DIM], ig_ref[...], cos, sin).astype(BF16)


def _dsa_prep(proj, tail, cos, sin_signed, q_gain, k_gain, ik_gain, layer, batch, seq_len, *,
              tm=256, tkc=512):
    m = proj.shape[0]
    n_main = proj.shape[1]
    n_tail = tail.shape[1]
    tiles_per_seq = seq_len // tm
    tiles_per_chunk = tkc // tm
    hq = N_HEADS * HEAD_DIM
    hk = DSA_KV_HEADS * HEAD_DIM
    row = lambda i: (i, 0)

    def v_block(i):
        pos_tile = i % tiles_per_seq
        return i // tiles_per_seq, pos_tile // tiles_per_chunk, 0, pos_tile % tiles_per_chunk
    gain_spec = pl.BlockSpec((None, 1, HEAD_DIM), lambda i: (layer, 0, 0))
    tab_spec = pl.BlockSpec((tm, HEAD_DIM), lambda i: (i % tiles_per_seq, 0))
    return pl.pallas_call(
        _dsa_prep_kernel,
        grid=(m // tm,),
        in_specs=[
            pl.BlockSpec((tm, n_main), row),
            pl.BlockSpec((tm, n_tail), row),
            tab_spec, tab_spec, gain_spec, gain_spec, gain_spec,
        ],
        out_specs=[
            pl.BlockSpec((tm, hq), row),
            pl.BlockSpec((tm, hk), row),
            pl.BlockSpec((None, None, hk, tm), v_block),
            pl.BlockSpec((tm, IDX_HEADS * IDX_DIM), row),
            pl.BlockSpec((tm, IDX_DIM), row),
        ],
        out_shape=[
            jax.ShapeDtypeStruct((m, hq), BF16),
            jax.ShapeDtypeStruct((m, hk), BF16),
            jax.ShapeDtypeStruct((batch, seq_len // tkc, hk, tkc), BF16),
            jax.ShapeDtypeStruct((m, IDX_HEADS * IDX_DIM), BF16),
            jax.ShapeDtypeStruct((m, IDX_DIM), BF16),
        ],
        compiler_params=_params("arbitrary"),
        name="dsa_norm_rope",
    )(proj, tail, cos, sin_signed, q_gain, k_gain, ik_gain)


def _key_to_f32(key):
    bits = jnp.where(key < 0, key ^ INT32_MIN, ~key)
    return lax.bitcast_convert_type(bits, F32)


def _key16_to_bf16(key):
    bits = jnp.where(key < 0, key ^ INT32_MIN, ~key) & -65536
    return lax.bitcast_convert_type(bits, F32).astype(BF16)


def _indexer_kernel(qi_ref, ki_ref, w_ref, bias_ref, sc_ref, hi_ref, cnt_ref, cnt_hi_ref, *,
                    tq, tkc, tkb, nkc, topk):
    i = pl.program_id(1)
    t0 = i * tq
    w = w_ref[...] * (1.0 / math.sqrt(IDX_DIM) / math.sqrt(IDX_HEADS))
    key_pos = lax.broadcasted_iota(jnp.int32, (tkc, tq), 0)
    qry_pos = lax.broadcasted_iota(jnp.int32, (tkc, tq), 1) + t0

    def chunk_in_range(c):
        return c * tkc < t0 + tq

    for c in range(nkc):

        @pl.when(chunk_in_range(c))
        def _():
            k_blk = ki_ref[c * tkc:(c + 1) * tkc, :]
            acc = jnp.zeros((tkc, tq), F32)
            for h in range(IDX_HEADS):
                logit = lax.dot_general(
                    k_blk, qi_ref[:, h * IDX_DIM:(h + 1) * IDX_DIM], (((1,), (1,)), ((), ())),
                    preferred_element_type=F32)
                acc = acc + jnp.maximum(logit, 0.0) * w[h:h + 1, :]
            score = jnp.where(key_pos + c * tkc <= qry_pos, acc, -jnp.inf)
            sc_ref[c] = score
            hi_ref[c] = score.astype(BF16)

        @pl.when(jnp.logical_not(chunk_in_range(c)))
        def _():
            sc_ref[c] = jnp.full((tkc, tq), -jnp.inf, F32)
            hi_ref[c] = jnp.full((tkc, tq), -jnp.inf, BF16)

    def sublane_partial(hit):
        return jnp.sum(hit.reshape(tkc // 8, 8, tq), axis=0)

    def count_ge(thr):
        cnt_ref[...] = sublane_partial(jnp.where(sc_ref[0] >= thr, 1.0, 0.0))
        for c in range(1, nkc):

            @pl.when(chunk_in_range(c))
            def _():
                cnt_ref[...] += sublane_partial(jnp.where(sc_ref[c] >= thr, 1.0, 0.0))

        return jnp.sum(cnt_ref[...], axis=0, keepdims=True)

    def count_hi_ge(thr_hi):
        one, zero = jnp.ones((), BF16), jnp.zeros((), BF16)

        def partial(c):
            hit = jnp.where(hi_ref[c] >= thr_hi, one, zero)
            return functools.reduce(lambda a, b: a + b, [hit[r:r + 16] for r in range(0, tkc, 16)])

        cnt_hi_ref[...] = partial(0)
        for c in range(1, nkc):

            @pl.when(chunk_in_range(c))
            def _():
                cnt_hi_ref[...] += partial(c)

        return jnp.sum(cnt_hi_ref[...].astype(F32), axis=0, keepdims=True)

    def store_bias(c, bias):
        for part in range(tkc // tkb):
            bias_ref[0, 0, c * (tkc // tkb) + part] = bias[part * tkb:(part + 1) * tkb]

    def write_bias(select_fn):
        for c in range(nkc):
            store_bias(c, jnp.where(select_fn(sc_ref[c]), 0.0, MASK_BIAS))

    @pl.when(t0 + tq <= topk)
    def _():
        write_bias(lambda sc: sc >= F32_LOWEST)

    @pl.when(t0 + tq > topk)
    def _():
        def coarse_step(step, key):
            trial = key | lax.shift_left(jnp.int32(1), 31 - step)
            keep = count_hi_ge(_key16_to_bf16(trial)) >= topk
            return jnp.where(keep, trial, key)

        coarse = lax.fori_loop(0, 16, coarse_step, jnp.zeros((1, tq), jnp.int32))
        coarse_key = coarse | jnp.where(coarse < 0, 0, 0xFFFF)
        base = coarse_key - 0x8000

        def fine_step(step, offset):
            trial = offset | lax.shift_left(jnp.int32(1), 16 - step)
            keep = count_ge(_key_to_f32(base + trial)) >= topk
            return jnp.where(keep, trial, offset)

        key = base + lax.fori_loop(0, 17, fine_step, jnp.zeros((1, tq), jnp.int32))
        few = lax.broadcasted_iota(jnp.int32, (1, tq), 1) + t0 < topk - 1
        thr = jnp.where(few, F32_LOWEST, _key_to_f32(key))
        write_bias(lambda sc: sc >= thr)

        n_ge = count_ge(thr)

        @pl.when(jnp.max(jnp.where(few, 0.0, n_ge)) > topk)
        def _():
            n_gt = sum(jnp.sum(jnp.where(sc_ref[c] > thr, 1.0, 0.0), axis=0, keepdims=True)
                       for c in range(nkc))
            room = jnp.where(few, float(tkc * nkc), topk - n_gt)
            r2 = lax.broadcasted_iota(jnp.int32, (tkc, tkc), 0)
            c2 = lax.broadcasted_iota(jnp.int32, (tkc, tkc), 1)
            before = jnp.where(c2 < r2, 1.0, 0.0).astype(BF16)
            seen = jnp.zeros((1, tq), F32)
            for c in range(nkc):
                sc = sc_ref[c]
                tied = jnp.where(sc == thr, 1.0, 0.0)
                rank = seen + jnp.dot(before, tied.astype(BF16), preferred_element_type=F32)
                take = jnp.where(sc > thr, 1.0, jnp.where(rank < room, tied, 0.0))
                store_bias(c, jnp.where(take > 0.0, 0.0, MASK_BIAS))
                seen = seen + jnp.sum(tied, axis=0, keepdims=True)


def _dsa_indexer(qi, ki, w_t, batch, seq_len, topk, *, tq, tkb, tkc=512):
    nq = seq_len // tq
    nkc = seq_len // tkc
    nkb = seq_len // tkb
    return pl.pallas_call(
        functools.partial(_indexer_kernel, tq=tq, tkc=tkc, tkb=tkb, nkc=nkc, topk=topk),
        grid=(batch, nq),
        in_specs=[
            pl.BlockSpec((tq, IDX_HEADS * IDX_DIM), lambda b, i: (b * nq + i, 0)),
            pl.BlockSpec((seq_len, IDX_DIM), lambda b, i: (b, 0)),
            pl.BlockSpec((IDX_HEADS, tq), lambda b, i: (0, b * nq + i)),
        ],
        out_specs=pl.BlockSpec((1, 1, nkb, tkb, tq), lambda b, i: (b, i, 0, 0, 0)),
        out_shape=jax.ShapeDtypeStruct((batch, nq, nkb, tkb, tq), F32),
        scratch_shapes=[pltpu.VMEM((nkc, tkc, tq), F32), pltpu.VMEM((nkc, tkc, tq), BF16),
                        pltpu.VMEM((8, tq), F32), pltpu.VMEM((16, tq), BF16)],
        compiler_params=_params("arbitrary", "arbitrary"),
        name="dsa_indexer_topk",
    )(qi, ki, w_t)


def _dsa_attn_kernel(q_ref, k_ref, vt_ref, bias_ref, o_ref, *, tq, tkb, groups, chain_heads):
    i = pl.program_id(1)
    n_chunks = ((i + 1) * tq + tkb - 1) // tkb
    cols = chain_heads * tq
    chains = [(g, r0) for g in range(groups) for r0 in range(0, DSA_GROUP, chain_heads)]

    def head_lanes(g, r):
        h = g * DSA_GROUP + r
        return slice(h * HEAD_DIM, (h + 1) * HEAD_DIM)

    qs = [jnp.concatenate([q_ref[:, head_lanes(g, r0 + r)] for r in range(chain_heads)], axis=0)
          for g, r0 in chains]

    def over_keys(x, op):
        return op(op(x.reshape(tkb // 8, 8, cols), axis=0), axis=0, keepdims=True)

    def kv_lanes(n):
        g, _ = chains[n]
        return slice(g * HEAD_DIM, (g + 1) * HEAD_DIM)

    def scores(n, c):
        k_blk = k_ref[pl.ds(pl.multiple_of(c * tkb, tkb), tkb), kv_lanes(n)]
        return lax.dot_general(k_blk, qs[n], (((1,), (1,)), ((), ())), preferred_element_type=F32)

    def softmax_step(n, c, state, raw, bias):
        m_run, l_run, acc = state
        s = bias + raw
        m_new = jnp.maximum(m_run, over_keys(s, jnp.max))
        alpha = jnp.exp2(m_run - m_new)
        p = jnp.exp2(s - m_new)
        l_run = alpha * l_run + over_keys(p, jnp.sum)
        acc = alpha * acc + jnp.dot(vt_ref[0, c, kv_lanes(n), :], p.astype(BF16),
                                    preferred_element_type=F32)
        return m_new, l_run, acc

    def body(c, states):
        bias = jnp.concatenate([bias_ref[0, 0, c]] * chain_heads, axis=1)
        raws = [scores(n, c) for n in range(len(chains))]
        return [softmax_step(n, c, states[n], raws[n], bias) for n in range(len(chains))]

    init = (jnp.full((1, cols), -jnp.inf, F32), jnp.zeros((1, cols), F32),
            jnp.zeros((HEAD_DIM, cols), F32))
    states = lax.fori_loop(0, n_chunks, body, [init] * len(chains))
    for (g, r0), (_, l_run, acc) in zip(chains, states):
        out_t = acc / l_run
        for r in range(chain_heads):
            o_ref[:, head_lanes(g, r0 + r)] = out_t[:, r * tq:(r + 1) * tq].T.astype(o_ref.dtype)


def _dsa_attention(q, k, v_t, bias, batch, seq_len, *, tq, tkb, groups=2, chain_heads=2):
    nq = seq_len // tq
    gw = groups * DSA_GROUP * HEAD_DIM
    kw = groups * HEAD_DIM
    nkb = seq_len // tkb
    return pl.pallas_call(
        functools.partial(_dsa_attn_kernel, tq=tq, tkb=tkb, groups=groups, chain_heads=chain_heads),
        grid=(batch, nq, DSA_KV_HEADS // groups),
        in_specs=[
            pl.BlockSpec((tq, gw), lambda b, i, g: (b * nq + i, g)),
            pl.BlockSpec((seq_len, kw), lambda b, i, g: (b, g)),
            pl.BlockSpec((1, nkb, kw, tkb), lambda b, i, g: (b, 0, g, 0)),
            pl.BlockSpec((1, 1, nkb, tkb, tq), lambda b, i, g: (b, i, 0, 0, 0)),
        ],
        out_specs=pl.BlockSpec((tq, gw), lambda b, i, g: (b * nq + i, g)),
        out_shape=jax.ShapeDtypeStruct((batch * seq_len, N_HEADS * HEAD_DIM), BF16),
        compiler_params=_params("arbitrary", "arbitrary", "arbitrary"),
        name="dsa_attention",
    )(q, k, v_t, bias)


def _rope_tables(seq_len, dim):
    inv_freq = 1.0 / (ROPE_THETA ** (jnp.arange(0, dim, 2, dtype=F32) / dim))
    ang = jnp.arange(seq_len, dtype=F32)[:, None] * inv_freq[None, :]
    cos, sin = jnp.cos(ang), jnp.sin(ang)
    return jnp.concatenate([cos, cos], axis=-1), jnp.concatenate([-sin, sin], axis=-1)


def _conv_ffn(x, norm_g, w_up, conv_w, conv_b, w_down, layer, seq_len):
    h = _rmsnorm(x, norm_g, layer)
    gated = _ffn_up(h, w_up, conv_w, conv_b, layer, seq_len)
    return _matmul(gated, w_down, layer, x.shape[1], tm=512, tn=1024, out_dtype=F32, res=x,
                   name="ffn_down_residual")


def kernel(x, attn_norm_g, ffn_norm_g, sb_w_qkv, sb_w_o, dsa_w_in, dsa_q_norm_g, dsa_k_norm_g,
           dsa_ik_norm_g, dsa_w_o, ffn_w_up, ffn_conv_w, ffn_conv_b, ffn_w_down):
    batch, seq_len, d_model = x.shape
    m = batch * seq_len
    hd = N_HEADS * HEAD_DIM
    x = x.reshape(m, d_model)
    attn_g = attn_norm_g[:, None, :]
    ffn_g = ffn_norm_g[:, None, :]
    conv_b = ffn_conv_b[:, None, :]
    w_down = _to_bf16(ffn_w_down)

    h = _rmsnorm(x, attn_g, 0)
    qkv = _matmul(h, sb_w_qkv, 0, 3 * hd, tm=2048, tn=1024, out_dtype=BF16, scaled_cols=hd,
                  col_scale=1.0 / math.sqrt(HEAD_DIM), name="sb_qkv_proj")
    mixed = _sb_attention(qkv.reshape(batch, seq_len, 3 * hd)).reshape(m, hd)
    x = _matmul(mixed, sb_w_o, 0, d_model, tm=1024, tn=1024, out_dtype=F32, res=x, name="sb_out_residual")
    x = _conv_ffn(x, ffn_g, ffn_w_up, ffn_conv_w, conv_b, w_down, 0, seq_len)

    n_main = 2 * hd + 2 * DSA_KV_HEADS * HEAD_DIM
    h = _rmsnorm(x, attn_g, 1)
    w_in_t = jnp.swapaxes(dsa_w_in, 1, 2)
    proj = _matmul(h, w_in_t, 0, n_main, tm=1024, tn=1024, out_dtype=F32, w_is_transposed=True,
                   name="dsa_in_proj")
    w_tail_t = w_in_t[:, n_main:, :]
    tail = _matmul(h, w_tail_t, 0, w_tail_t.shape[1], tm=1024, tn=w_tail_t.shape[1], out_dtype=F32,
                   w_is_transposed=True, name="dsa_in_proj_tail")
    cos, sin_signed = _rope_tables(seq_len, HEAD_DIM)
    tq, tkb = 256, 512
    q, k, v_t, qi, ki = _dsa_prep(proj, tail, cos, sin_signed, dsa_q_norm_g[:, None, :],
                                  dsa_k_norm_g[:, None, :], dsa_ik_norm_g[:, None, :], 0, batch, seq_len,
                                  tkc=tkb)
    topk = min(TOPK_MAX, seq_len // 4)
    w_t = tail[:, IDX_DIM:].T
    bias = _dsa_indexer(qi, ki, w_t, batch, seq_len, topk, tq=tq, tkb=tkb)
    mixed = _dsa_attention(q, k, v_t, bias, batch, seq_len, tq=tq, tkb=tkb)
    x = _matmul(mixed, dsa_w_o, 0, d_model, tm=1024, tn=1024, out_dtype=F32, res=x, name="dsa_out_residual")
    x = _conv_ffn(x, ffn_g, ffn_w_up, ffn_conv_w, conv_b, w_down, 1, seq_len)
    return x.reshape(batch, seq_len, d_model)
```

```python
import functools
import math

import jax
import jax.numpy as jnp
from jax import lax
from jax.experimental import pallas as pl
from jax.experimental.pallas import tpu as pltpu

N_HEADS = 16
HEAD_DIM = 128
DSA_KV_HEADS = 4
DSA_GROUP = N_HEADS // DSA_KV_HEADS
IDX_HEADS = 16
IDX_DIM = 128
TOPK_MAX = 256
D_FF = 5632
CONV_WIDTH = 3
ROPE_THETA = 10000.0
NORM_EPS = 1e-6

V7X_VMEM_BYTES = 64 * 1024 * 1024
VMEM_LIMIT_BYTES = V7X_VMEM_BYTES - 8 * 1024 * 1024

F32 = jnp.float32
BF16 = jnp.bfloat16
MASK_BIAS = -1e30
F32_LOWEST = float(jnp.finfo(jnp.float32).min)
INT32_MIN = -(2 ** 31)
LOG2E = math.log2(math.e)
DSA_Q_SCALE = LOG2E / math.sqrt(HEAD_DIM)


def _params(*sem, flags=None):
    return pltpu.CompilerParams(dimension_semantics=sem, vmem_limit_bytes=VMEM_LIMIT_BYTES, flags=flags)


def _rmsnorm_kernel(x_ref, g_ref, o_ref):
    x = x_ref[...]
    ms = jnp.mean(x * x, axis=-1, keepdims=True)
    o_ref[...] = ((x * lax.rsqrt(ms + NORM_EPS)) * g_ref[0]).astype(o_ref.dtype)


def _rmsnorm(x, g_all, layer, *, tm=512):
    m, d = x.shape
    return pl.pallas_call(
        _rmsnorm_kernel,
        grid=(m // tm,),
        in_specs=[
            pl.BlockSpec((tm, d), lambda i: (i, 0)),
            pl.BlockSpec((1, 1, d), lambda i: (layer, 0, 0)),
        ],
        out_specs=pl.BlockSpec((tm, d), lambda i: (i, 0)),
        out_shape=jax.ShapeDtypeStruct((m, d), BF16),
        compiler_params=_params("arbitrary"),
        name="rmsnorm",
    )(x, g_all)


def _matmul_kernel(*refs, has_res, cast_w, w_is_transposed, scaled_blocks, block_scale):
    a_ref, w_ref, *rest = refs
    r_ref = rest.pop(0) if has_res else None
    o_ref = rest.pop(0)
    if cast_w:
        (w_bf,) = rest

        @pl.when(pl.program_id(1) == 0)
        def _():
            w = w_ref[...]
            w_bf[...] = (w.T if w_is_transposed else w).astype(BF16)

        w = w_bf[...]
    else:
        w = w_ref[...]
    out = jnp.dot(a_ref[...], w, preferred_element_type=F32)
    if scaled_blocks:
        out = out * jnp.where(pl.program_id(0) < scaled_blocks, block_scale, 1.0)
    if has_res:
        out = r_ref[...] + out
    o_ref[...] = out.astype(o_ref.dtype)


def _matmul(a, w_all, layer, n_out, *, tm, tn, out_dtype, res=None, w_is_transposed=False,
            scaled_cols=0, col_scale=1.0, name):
    m, k_dim = a.shape
    cast_w = w_all.dtype != BF16
    assert cast_w or not w_is_transposed
    assert scaled_cols % tn == 0
    if w_is_transposed:
        w_spec = pl.BlockSpec((None, tn, k_dim), lambda j, i: (layer, j, 0))
    else:
        w_spec = pl.BlockSpec((None, k_dim, tn), lambda j, i: (layer, 0, j))
    in_specs = [pl.BlockSpec((tm, k_dim), lambda j, i: (i, 0)), w_spec]
    args = [a, w_all]
    if res is not None:
        in_specs.append(pl.BlockSpec((tm, tn), lambda j, i: (i, j)))
        args.append(res)
    return pl.pallas_call(
        functools.partial(_matmul_kernel, has_res=res is not None, cast_w=cast_w,
                          w_is_transposed=w_is_transposed, scaled_blocks=scaled_cols // tn,
                          block_scale=col_scale),
        grid=(n_out // tn, m // tm),
        in_specs=in_specs,
        out_specs=pl.BlockSpec((tm, tn), lambda j, i: (i, j)),
        out_shape=jax.ShapeDtypeStruct((m, n_out), out_dtype),
        scratch_shapes=[pltpu.VMEM((k_dim, tn), BF16)] if cast_w else [],
        compiler_params=_params("arbitrary", "arbitrary"),
        name=name,
    )(*args)


def _cast_kernel(x_ref, o_ref):
    o_ref[...] = x_ref[...].astype(o_ref.dtype)


def _to_bf16(w_all, *, rows=512):
    n_layers, k_dim, n = w_all.shape
    return pl.pallas_call(
        _cast_kernel,
        grid=(n_layers, k_dim // rows),
        in_specs=[pl.BlockSpec((1, rows, n), lambda l, r: (l, r, 0))],
        out_specs=pl.BlockSpec((1, rows, n), lambda l, r: (l, r, 0)),
        out_shape=jax.ShapeDtypeStruct(w_all.shape, BF16),
        compiler_params=_params("arbitrary", "arbitrary"),
        name="weights_to_bf16",
    )(w_all)


def _sb_tile(q, k_blk, v_blk, upper, carry, causal):
    z = lax.dot_general(q, k_blk, (((1,), (1,)), ((), ())), preferred_element_type=F32)
    sp = jnp.maximum(z, 0.0) + jnp.log(1.0 + jnp.exp2(jnp.abs(z) * (-LOG2E)))
    if causal is not None:
        sp = jnp.where(causal, sp, 0.0)
    tail = jnp.dot(sp.astype(BF16), upper, preferred_element_type=F32)
    a = jnp.exp(z - (tail + carry))
    if causal is not None:
        a = jnp.where(causal, a, 0.0)
    out = jnp.dot(a.astype(BF16), v_blk, preferred_element_type=F32)
    return out, carry + jnp.sum(sp, axis=1, keepdims=True)


def _sb_attn_kernel(q_ref, k_ref, v_ref, o_ref, *, tq, tk, heads):
    i = pl.program_id(2)
    per_q = tq // tk
    r2 = lax.broadcasted_iota(jnp.int32, (tk, tk), 0)
    c2 = lax.broadcasted_iota(jnp.int32, (tk, tk), 1)
    upper = jnp.where(r2 >= c2, 1.0, 0.0).astype(BF16)
    rows = lax.broadcasted_iota(jnp.int32, (tq, tk), 0)
    cols = lax.broadcasted_iota(jnp.int32, (tq, tk), 1)
    lanes = [slice(h * HEAD_DIM, (h + 1) * HEAD_DIM) for h in range(heads)]

    def block(j, carries, causal, first_row=0):
        start = pl.multiple_of(j * tk, tk)
        outs = [_sb_tile(q_ref[0, first_row:, lanes[h]], k_ref[0, pl.ds(start, tk), lanes[h]],
                         v_ref[0, pl.ds(start, tk), lanes[h]], upper, carries[h], causal)
                for h in range(heads)]
        return [o for o, _ in outs], [c for _, c in outs]

    carries = [jnp.zeros((tq, 1), F32)] * heads
    accs = [jnp.zeros((tq, HEAD_DIM), F32)] * heads
    for d in reversed(range(per_q)):
        r0 = d * tk
        outs, low = block(i * per_q + d, [c[r0:] for c in carries], (cols + r0 < rows)[r0:], r0)
        if r0:
            carries = [jnp.concatenate([c[:r0], lo], axis=0) for c, lo in zip(carries, low)]
            accs = [jnp.concatenate([acc[:r0], acc[r0:] + o], axis=0) for acc, o in zip(accs, outs)]
        else:
            carries = low
            accs = [acc + o for acc, o in zip(accs, outs)]

    def body(step, state):
        accs, carries = state
        for d in range(per_q):
            outs, carries = block((i - step) * per_q - 1 - d, carries, None)
            accs = [acc + o for acc, o in zip(accs, outs)]
        return accs, carries

    accs, _ = lax.fori_loop(0, i, body, (accs, carries))
    for h in range(heads):
        o_ref[0, :, lanes[h]] = accs[h].astype(o_ref.dtype)


def _sb_attention(qkv, *, tq=512, tk=256, heads=4):
    b, s, _ = qkv.shape
    groups = N_HEADS // heads
    width = heads * HEAD_DIM
    return pl.pallas_call(
        functools.partial(_sb_attn_kernel, tq=tq, tk=tk, heads=heads),
        grid=(b, groups, s // tq),
        in_specs=[
            pl.BlockSpec((1, tq, width), lambda bi, g, i: (bi, i, g)),
            pl.BlockSpec((1, s, width), lambda bi, g, i: (bi, 0, groups + g)),
            pl.BlockSpec((1, s, width), lambda bi, g, i: (bi, 0, 2 * groups + g)),
        ],
        out_specs=pl.BlockSpec((1, tq, width), lambda bi, g, i: (bi, i, g)),
        out_shape=jax.ShapeDtypeStruct((b, s, N_HEADS * HEAD_DIM), BF16),
        compiler_params=_params("arbitrary", "arbitrary", "arbitrary"),
        name="sb_attention",
    )(qkv, qkv, qkv)


HISTORY_ROWS = 8


def _causal_conv(u_ref, cw, cb):
    r0 = HISTORY_ROWS
    rows = u_ref.shape[0] - r0
    return (cb + cw[2:3] * u_ref[r0:r0 + rows] + cw[1:2] * u_ref[r0 - 1:r0 - 1 + rows]
            + cw[0:1] * u_ref[r0 - 2:r0 - 2 + rows])


def _ffn_up_kernel(h_ref, wg_ref, wv_ref, cwg_ref, cwv_ref, cbg_ref, cbv_ref, o_ref,
                   wg_bf, wv_bf, hist_g, hist_v, *stage, tiles_per_seq, chunk_rows, chunk_cols):
    i = pl.program_id(1)
    tm, tn = o_ref.shape
    r0 = HISTORY_ROWS

    @pl.when(i == 0)
    def _():
        wg_bf[...] = wg_ref[...].astype(BF16)
        wv_bf[...] = wv_ref[...].astype(BF16)

    @pl.when(i % tiles_per_seq == 0)
    def _():
        hist_g[...] = jnp.zeros_like(hist_g)
        hist_v[...] = jnp.zeros_like(hist_v)

    piece = 0
    for c0 in range(0, tn, chunk_cols):
        cols = slice(c0, c0 + chunk_cols)
        prev_g = hist_g[:, cols]
        prev_v = hist_v[:, cols]
        for t0 in range(0, tm, chunk_rows):
            ug_s, uv_s = stage[2 * (piece % 2)], stage[2 * (piece % 2) + 1]
            piece += 1
            h = h_ref[t0:t0 + chunk_rows, :]
            ug_s[0:r0] = prev_g
            uv_s[0:r0] = prev_v
            ug_s[r0:] = jnp.dot(h, wg_bf[:, cols], preferred_element_type=F32)
            uv_s[r0:] = jnp.dot(h, wv_bf[:, cols], preferred_element_type=F32)
            gate = _causal_conv(ug_s, cwg_ref[:, cols], cbg_ref[0, :, cols])
            val = _causal_conv(uv_s, cwv_ref[:, cols], cbv_ref[0, :, cols])
            o_ref[t0:t0 + chunk_rows, cols] = (jax.nn.silu(gate) * val).astype(o_ref.dtype)
            prev_g = ug_s[chunk_rows:]
            prev_v = uv_s[chunk_rows:]
        hist_g[:, cols] = prev_g
        hist_v[:, cols] = prev_v


def _ffn_up(h, w_up_all, conv_w_all, conv_b_all, layer, seq_len, *, tm=2048, tn=512,
            chunk_rows=512, chunk_cols=256):
    m, d = h.shape
    nj = D_FF // tn
    return pl.pallas_call(
        functools.partial(_ffn_up_kernel, tiles_per_seq=seq_len // tm, chunk_rows=chunk_rows,
                          chunk_cols=chunk_cols),
        grid=(nj, m // tm),
        in_specs=[
            pl.BlockSpec((tm, d), lambda j, i: (i, 0)),
            pl.BlockSpec((None, d, tn), lambda j, i: (layer, 0, j)),
            pl.BlockSpec((None, d, tn), lambda j, i: (layer, 0, nj + j)),
            pl.BlockSpec((None, CONV_WIDTH, tn), lambda j, i: (layer, 0, j)),
            pl.BlockSpec((None, CONV_WIDTH, tn), lambda j, i: (layer, 0, nj + j)),
            pl.BlockSpec((1, 1, tn), lambda j, i: (layer, 0, j)),
            pl.BlockSpec((1, 1, tn), lambda j, i: (layer, 0, nj + j)),
        ],
        out_specs=pl.BlockSpec((tm, tn), lambda j, i: (i, j)),
        out_shape=jax.ShapeDtypeStruct((m, D_FF), BF16),
        scratch_shapes=[
            pltpu.VMEM((d, tn), BF16),
            pltpu.VMEM((d, tn), BF16),
            pltpu.VMEM((HISTORY_ROWS, tn), F32),
            pltpu.VMEM((HISTORY_ROWS, tn), F32),
        ] + [pltpu.VMEM((HISTORY_ROWS + chunk_rows, chunk_cols), F32)] * 4,
        compiler_params=_params("arbitrary", "arbitrary"),
        name="ffn_up_conv_gate",
    )(h, w_up_all, w_up_all, conv_w_all, conv_w_all, conv_b_all, conv_b_all)


def _head_norm_rope_t(x, gain, cos_t, sin_t):
    if gain is not None:
        ms = jnp.mean(x * x, axis=0, keepdims=True)
        x = (x * lax.rsqrt(ms + NORM_EPS)) * gain
    return x * cos_t + pltpu.roll(x, HEAD_DIM // 2, 0) * sin_t


DSA_K_ROW = N_HEADS * HEAD_DIM
DSA_V_ROW = DSA_K_ROW + DSA_KV_HEADS * HEAD_DIM
DSA_QI_ROW = DSA_V_ROW + DSA_KV_HEADS * HEAD_DIM
DSA_MAIN_ROWS = DSA_QI_ROW + IDX_HEADS * IDX_DIM


def _dsa_in_kernel(h_ref, w_ref, cos_ref, sin_ref, qg_ref, kg_ref, o_ref, w_bf, *, tn):
    j = pl.program_id(0)

    @pl.when(pl.program_id(1) == 0)
    def _():
        w_bf[...] = w_ref[...].astype(BF16)

    def head_kind(row):
        return "q" if row < DSA_K_ROW else "k" if row < DSA_V_ROW else "v" if row < DSA_QI_ROW else "qi"

    def finish(block):
        out = lax.dot_general(w_bf[...], h_ref[...], (((1,), (1,)), ((), ())), preferred_element_type=F32)
        cos, sin = cos_ref[...], sin_ref[...]
        for n in range(tn // HEAD_DIM):
            sl = slice(n * HEAD_DIM, (n + 1) * HEAD_DIM)
            kind = head_kind(block * tn + n * HEAD_DIM)
            x = out[sl, :]
            if kind == "q":
                x = _head_norm_rope_t(x, qg_ref[...] * DSA_Q_SCALE, cos, sin)
            elif kind == "k":
                x = _head_norm_rope_t(x, kg_ref[...], cos, sin)
            elif kind == "qi":
                x = _head_norm_rope_t(x, None, cos, sin)
            o_ref[sl, :] = x.astype(BF16)

    for block in range(DSA_MAIN_ROWS // tn):
        pl.when(j == block)(functools.partial(finish, block))


def _dsa_in_proj(h, w_in_t, cos_t, sin_t, q_gain, k_gain, seq_len, *, tm=1024, tn=1024):
    m, d = h.shape
    tiles_per_seq = seq_len // tm
    gain_spec = pl.BlockSpec((HEAD_DIM, 1), lambda j, i: (0, 0))
    tab_spec = pl.BlockSpec((HEAD_DIM, tm), lambda j, i: (0, i % tiles_per_seq))
    return pl.pallas_call(
        functools.partial(_dsa_in_kernel, tn=tn),
        grid=(DSA_MAIN_ROWS // tn, m // tm),
        in_specs=[
            pl.BlockSpec((tm, d), lambda j, i: (i, 0)),
            pl.BlockSpec((tn, d), lambda j, i: (j, 0)),
            tab_spec, tab_spec, gain_spec, gain_spec,
        ],
        out_specs=pl.BlockSpec((tn, tm), lambda j, i: (j, i)),
        out_shape=jax.ShapeDtypeStruct((DSA_MAIN_ROWS, m), BF16),
        scratch_shapes=[pltpu.VMEM((tn, d), BF16)],
        compiler_params=_params("arbitrary", "arbitrary"),
        name="dsa_in_proj",
    )(h, w_in_t, cos_t, sin_t, q_gain, k_gain)


def _dsa_tail_kernel(h_ref, w_ref, o_ref):
    o_ref[...] = lax.dot_general(w_ref[...].astype(BF16), h_ref[...], (((1,), (1,)), ((), ())),
                                 preferred_element_type=F32)


def _dsa_in_proj_tail(h, w_tail_t, *, tm=1024):
    m, d = h.shape
    rows = w_tail_t.shape[0]
    return pl.pallas_call(
        _dsa_tail_kernel,
        grid=(m // tm,),
        in_specs=[pl.BlockSpec((tm, d), lambda i: (i, 0)), pl.BlockSpec((rows, d), lambda i: (0, 0))],
        out_specs=pl.BlockSpec((rows, tm), lambda i: (0, i)),
        out_shape=jax.ShapeDtypeStruct((rows, m), F32),
        compiler_params=_params("arbitrary"),
        name="dsa_in_proj_tail",
    )(h, w_tail_t)


def _dsa_prep_kernel(k_ref, v_ref, t_ref, cos_ref, sin_ref, ig_ref, ko_ref, vt_ref, ki_ref):
    ko_ref[...] = k_ref[...].astype(F32).T.astype(BF16)
    vt_ref[...] = v_ref[...]
    ki_t = _head_norm_rope_t(t_ref[0:IDX_DIM, :], ig_ref[...], cos_ref[...], sin_ref[...])
    ki_ref[...] = ki_t.T.astype(BF16)


def _dsa_prep(proc_t, tail_t, cos_t, sin_t, ik_gain, batch, seq_len, *, tm=256, tkc=512):
    m = proc_t.shape[1]
    tiles_per_seq = seq_len // tm
    tiles_per_chunk = tkc // tm
    hk = DSA_KV_HEADS * HEAD_DIM

    def v_block(i):
        pos_tile = i % tiles_per_seq
        return i // tiles_per_seq, pos_tile // tiles_per_chunk, 0, pos_tile % tiles_per_chunk

    tab_spec = pl.BlockSpec((HEAD_DIM, tm), lambda i: (0, i % tiles_per_seq))
    return pl.pallas_call(
        _dsa_prep_kernel,
        grid=(m // tm,),
        in_specs=[
            pl.BlockSpec((hk, tm), lambda i: (DSA_K_ROW // hk, i)),
            pl.BlockSpec((hk, tm), lambda i: (DSA_V_ROW // hk, i)),
            pl.BlockSpec((tail_t.shape[0], tm), lambda i: (0, i)),
            tab_spec, tab_spec,
            pl.BlockSpec((HEAD_DIM, 1), lambda i: (0, 0)),
        ],
        out_specs=[
            pl.BlockSpec((tm, hk), lambda i: (i, 0)),
            pl.BlockSpec((None, None, hk, tm), v_block),
            pl.BlockSpec((tm, IDX_DIM), lambda i: (i, 0)),
        ],
        out_shape=[
            jax.ShapeDtypeStruct((m, hk), BF16),
            jax.ShapeDtypeStruct((batch, seq_len // tkc, hk, tkc), BF16),
            jax.ShapeDtypeStruct((m, IDX_DIM), BF16),
        ],
        compiler_params=_params("arbitrary"),
        name="dsa_kv_layout_kidx",
    )(proc_t, proc_t, tail_t, cos_t, sin_t, ik_gain)


def _key_to_f32(key):
    bits = jnp.where(key < 0, key ^ INT32_MIN, ~key)
    return lax.bitcast_convert_type(bits, F32)


def _key16_to_bf16(key):
    bits = jnp.where(key < 0, key ^ INT32_MIN, ~key) & -65536
    return lax.bitcast_convert_type(bits, F32).astype(BF16)


def _indexer_kernel(qi_lo_ref, qi_hi_ref, ki_ref, w_ref, bias_ref, sc_ref, hi_ref, cnt_ref, cnt_hi_ref, *,
                    tq, tkc, tkb, nkc, topk):
    i = pl.program_id(1)
    t0 = i * tq
    w = w_ref[...] * (1.0 / math.sqrt(IDX_DIM) / math.sqrt(IDX_HEADS))
    key_pos = lax.broadcasted_iota(jnp.int32, (tkc, tq), 0)
    qry_pos = lax.broadcasted_iota(jnp.int32, (tkc, tq), 1) + t0

    def chunk_in_range(c):
        return c * tkc < t0 + tq

    for c in range(nkc):

        @pl.when(chunk_in_range(c))
        def _():
            k_blk = ki_ref[c * tkc:(c + 1) * tkc, :]
            acc = jnp.zeros((tkc, tq), F32)
            for h in range(IDX_HEADS):
                qi_ref, hh = (qi_lo_ref, h) if h < IDX_HEADS // 2 else (qi_hi_ref, h - IDX_HEADS // 2)
                logit = jnp.dot(k_blk, qi_ref[hh * IDX_DIM:(hh + 1) * IDX_DIM, :],
                                preferred_element_type=F32)
                acc = acc + jnp.maximum(logit, 0.0) * w[h:h + 1, :]
            score = jnp.where(key_pos + c * tkc <= qry_pos, acc, -jnp.inf)
            sc_ref[c] = score
            hi_ref[c] = score.astype(BF16)

        @pl.when(jnp.logical_not(chunk_in_range(c)))
        def _():
            sc_ref[c] = jnp.full((tkc, tq), -jnp.inf, F32)
            hi_ref[c] = jnp.full((tkc, tq), -jnp.inf, BF16)

    def sublane_partial(hit):
        return jnp.sum(hit.reshape(tkc // 8, 8, tq), axis=0)

    def count_ge(thr):
        cnt_ref[...] = sublane_partial(jnp.where(sc_ref[0] >= thr, 1.0, 0.0))
        for c in range(1, nkc):

            @pl.when(chunk_in_range(c))
            def _():
                cnt_ref[...] += sublane_partial(jnp.where(sc_ref[c] >= thr, 1.0, 0.0))

        return jnp.sum(cnt_ref[...], axis=0, keepdims=True)

    def count_hi_ge(thr_hi):
        one, zero = jnp.ones((), BF16), jnp.zeros((), BF16)

        def partial(c):
            hit = jnp.where(hi_ref[c] >= thr_hi, one, zero)
            return functools.reduce(lambda a, b: a + b, [hit[r:r + 16] for r in range(0, tkc, 16)])

        cnt_hi_ref[...] = partial(0)
        for c in range(1, nkc):

            @pl.when(chunk_in_range(c))
            def _():
                cnt_hi_ref[...] += partial(c)

        return jnp.sum(cnt_hi_ref[...].astype(F32), axis=0, keepdims=True)

    def store_bias(c, bias):
        for part in range(tkc // tkb):
            bias_ref[0, 0, c * (tkc // tkb) + part] = bias[part * tkb:(part + 1) * tkb]

    def write_bias(select_fn):
        for c in range(nkc):
            store_bias(c, jnp.where(select_fn(sc_ref[c]), 0.0, MASK_BIAS))

    @pl.when(t0 + tq <= topk)
    def _():
        write_bias(lambda sc: sc >= F32_LOWEST)

    @pl.when(t0 + tq > topk)
    def _():
        def coarse_step(step, key):
            trial = key | lax.shift_left(jnp.int32(1), 31 - step)
            keep = count_hi_ge(_key16_to_bf16(trial)) >= topk
            return jnp.where(keep, trial, key)

        coarse = lax.fori_loop(0, 16, coarse_step, jnp.zeros((1, tq), jnp.int32))
        coarse_key = coarse | jnp.where(coarse < 0, 0, 0xFFFF)
        base = coarse_key - 0x8000

        def fine_step(step, offset):
            trial = offset | lax.shift_left(jnp.int32(1), 16 - step)
            keep = count_ge(_key_to_f32(base + trial)) >= topk
            return jnp.where(keep, trial, offset)

        key = base + lax.fori_loop(0, 17, fine_step, jnp.zeros((1, tq), jnp.int32))
        few = lax.broadcasted_iota(jnp.int32, (1, tq), 1) + t0 < topk - 1
        thr = jnp.where(few, F32_LOWEST, _key_to_f32(key))
        write_bias(lambda sc: sc >= thr)

        n_ge = count_ge(thr)

        @pl.when(jnp.max(jnp.where(few, 0.0, n_ge)) > topk)
        def _():
            n_gt = sum(jnp.sum(jnp.where(sc_ref[c] > thr, 1.0, 0.0), axis=0, keepdims=True)
                       for c in range(nkc))
            room = jnp.where(few, float(tkc * nkc), topk - n_gt)
            r2 = lax.broadcasted_iota(jnp.int32, (tkc, tkc), 0)
            c2 = lax.broadcasted_iota(jnp.int32, (tkc, tkc), 1)
            before = jnp.where(c2 < r2, 1.0, 0.0).astype(BF16)
            seen = jnp.zeros((1, tq), F32)
            for c in range(nkc):
                sc = sc_ref[c]
                tied = jnp.where(sc == thr, 1.0, 0.0)
                rank = seen + jnp.dot(before, tied.astype(BF16), preferred_element_type=F32)
                take = jnp.where(sc > thr, 1.0, jnp.where(rank < room, tied, 0.0))
                store_bias(c, jnp.where(take > 0.0, 0.0, MASK_BIAS))
                seen = seen + jnp.sum(tied, axis=0, keepdims=True)


def _dsa_indexer(proc, ki, w_t, batch, seq_len, topk, *, tq, tkb, tkc=512):
    half = IDX_HEADS * IDX_DIM // 2
    qi_block = DSA_QI_ROW // half
    nq = seq_len // tq
    nkc = seq_len // tkc
    nkb = seq_len // tkb
    return pl.pallas_call(
        functools.partial(_indexer_kernel, tq=tq, tkc=tkc, tkb=tkb, nkc=nkc, topk=topk),
        grid=(batch, nq),
        in_specs=[
            pl.BlockSpec((half, tq), lambda b, i: (qi_block, b * nq + i)),
            pl.BlockSpec((half, tq), lambda b, i: (qi_block + 1, b * nq + i)),
            pl.BlockSpec((seq_len, IDX_DIM), lambda b, i: (b, 0)),
            pl.BlockSpec((IDX_HEADS, tq), lambda b, i: (0, b * nq + i)),
        ],
        out_specs=pl.BlockSpec((1, 1, nkb, tkb, tq), lambda b, i: (b, i, 0, 0, 0)),
        out_shape=jax.ShapeDtypeStruct((batch, nq, nkb, tkb, tq), F32),
        scratch_shapes=[pltpu.VMEM((nkc, tkc, tq), F32), pltpu.VMEM((nkc, tkc, tq), BF16),
                        pltpu.VMEM((8, tq), F32), pltpu.VMEM((16, tq), BF16)],
        compiler_params=_params("arbitrary", "arbitrary"),
        name="dsa_indexer_topk",
    )(proc, proc, ki, w_t)


def _dsa_attn_kernel(q_ref, k_ref, vt_ref, bias_ref, o_ref, *, tq, tkb, groups, chain_heads):
    i = pl.program_id(1)
    n_chunks = ((i + 1) * tq + tkb - 1) // tkb
    cols = chain_heads * tq
    chains = [(g, r0) for g in range(groups) for r0 in range(0, DSA_GROUP, chain_heads)]

    def head_lanes(g, r):
        h = g * DSA_GROUP + r
        return slice(h * HEAD_DIM, (h + 1) * HEAD_DIM)

    qs = [jnp.concatenate([q_ref[head_lanes(g, r0 + r), :] for r in range(chain_heads)], axis=1)
          for g, r0 in chains]

    def over_keys(x, op):
        return op(op(x.reshape(tkb // 8, 8, cols), axis=0), axis=0, keepdims=True)

    def kv_lanes(n):
        g, _ = chains[n]
        return slice(g * HEAD_DIM, (g + 1) * HEAD_DIM)

    def scores(n, c):
        k_blk = k_ref[pl.ds(pl.multiple_of(c * tkb, tkb), tkb), kv_lanes(n)]
        return jnp.dot(k_blk, qs[n], preferred_element_type=F32)

    def softmax_step(n, c, state, raw, bias):
        m_run, l_run, acc = state
        s = bias + raw
        m_new = jnp.maximum(m_run, over_keys(s, jnp.max))
        alpha = jnp.exp2(m_run - m_new)
        p = jnp.exp2(s - m_new)
        l_run = alpha * l_run + over_keys(p, jnp.sum)
        acc = alpha * acc + jnp.dot(vt_ref[0, c, kv_lanes(n), :], p.astype(BF16),
                                    preferred_element_type=F32)
        return m_new, l_run, acc

    def body(c, states):
        bias = jnp.concatenate([bias_ref[0, 0, c]] * chain_heads, axis=1)
        raws = [scores(n, c) for n in range(len(chains))]
        return [softmax_step(n, c, states[n], raws[n], bias) for n in range(len(chains))]

    init = (jnp.full((1, cols), -jnp.inf, F32), jnp.zeros((1, cols), F32),
            jnp.zeros((HEAD_DIM, cols), F32))
    states = lax.fori_loop(0, n_chunks, body, [init] * len(chains))
    for (g, r0), (_, l_run, acc) in zip(chains, states):
        out_t = acc / l_run
        for r in range(chain_heads):
            o_ref[:, head_lanes(g, r0 + r)] = out_t[:, r * tq:(r + 1) * tq].T.astype(o_ref.dtype)


def _dsa_attention(proc, k, v_t, bias, batch, seq_len, *, tq, tkb, groups=2, chain_heads=2):
    nq = seq_len // tq
    gw = groups * DSA_GROUP * HEAD_DIM
    kw = groups * HEAD_DIM
    nkb = seq_len // tkb
    return pl.pallas_call(
        functools.partial(_dsa_attn_kernel, tq=tq, tkb=tkb, groups=groups, chain_heads=chain_heads),
        grid=(batch, nq, DSA_KV_HEADS // groups),
        in_specs=[
            pl.BlockSpec((gw, tq), lambda b, i, g: (g, b * nq + i)),
            pl.BlockSpec((seq_len, kw), lambda b, i, g: (b, g)),
            pl.BlockSpec((1, nkb, kw, tkb), lambda b, i, g: (b, 0, g, 0)),
            pl.BlockSpec((1, 1, nkb, tkb, tq), lambda b, i, g: (b, i, 0, 0, 0)),
        ],
        out_specs=pl.BlockSpec((tq, gw), lambda b, i, g: (b * nq + i, g)),
        out_shape=jax.ShapeDtypeStruct((batch * seq_len, N_HEADS * HEAD_DIM), BF16),
        compiler_params=_params("arbitrary", "arbitrary", "arbitrary"),
        name="dsa_attention",
    )(proc, k, v_t, bias)


def _rope_tables_t(seq_len, dim):
    inv_freq = 1.0 / (ROPE_THETA ** (jnp.arange(0, dim, 2, dtype=F32) / dim))
    ang = jnp.arange(seq_len, dtype=F32)[:, None] * inv_freq[None, :]
    cos, sin = jnp.cos(ang).T, jnp.sin(ang).T
    return jnp.concatenate([cos, cos], axis=0), jnp.concatenate([-sin, sin], axis=0)


def _conv_ffn(x, norm_g, w_up, conv_w, conv_b, w_down, layer, seq_len):
    h = _rmsnorm(x, norm_g, layer)
    gated = _ffn_up(h, w_up, conv_w, conv_b, layer, seq_len)
    return _matmul(gated, w_down, layer, x.shape[1], tm=512, tn=1024, out_dtype=F32, res=x,
                   name="ffn_down_residual")


def kernel(x, attn_norm_g, ffn_norm_g, sb_w_qkv, sb_w_o, dsa_w_in, dsa_q_norm_g, dsa_k_norm_g,
           dsa_ik_norm_g, dsa_w_o, ffn_w_up, ffn_conv_w, ffn_conv_b, ffn_w_down):
    batch, seq_len, d_model = x.shape
    m = batch * seq_len
    hd = N_HEADS * HEAD_DIM
    x = x.reshape(m, d_model)
    attn_g = attn_norm_g[:, None, :]
    ffn_g = ffn_norm_g[:, None, :]
    conv_b = ffn_conv_b[:, None, :]
    w_down = _to_bf16(ffn_w_down)

    h = _rmsnorm(x, attn_g, 0)
    qkv = _matmul(h, sb_w_qkv, 0, 3 * hd, tm=2048, tn=1024, out_dtype=BF16, scaled_cols=hd,
                  col_scale=1.0 / math.sqrt(HEAD_DIM), name="sb_qkv_proj")
    mixed = _sb_attention(qkv.reshape(batch, seq_len, 3 * hd)).reshape(m, hd)
    x = _matmul(mixed, sb_w_o, 0, d_model, tm=1024, tn=1024, out_dtype=F32, res=x, name="sb_out_residual")
    x = _conv_ffn(x, ffn_g, ffn_w_up, ffn_conv_w, conv_b, w_down, 0, seq_len)

    h = _rmsnorm(x, attn_g, 1)
    w_in_t = jnp.swapaxes(dsa_w_in[0], 0, 1)
    cos_t, sin_t = _rope_tables_t(seq_len, HEAD_DIM)
    proc = _dsa_in_proj(h, w_in_t, cos_t, sin_t, dsa_q_norm_g[0][:, None], dsa_k_norm_g[0][:, None], seq_len)
    tail_t = _dsa_in_proj_tail(h, w_in_t[DSA_MAIN_ROWS:])
    tq, tkb = 256, 512
    k, v_t, ki = _dsa_prep(proc, tail_t, cos_t, sin_t, dsa_ik_norm_g[0][:, None], batch, seq_len, tkc=tkb)
    topk = min(TOPK_MAX, seq_len // 4)
    w_t = tail_t[IDX_DIM:]
    bias = _dsa_indexer(proc, ki, w_t, batch, seq_len, topk, tq=tq, tkb=tkb)
    mixed = _dsa_attention(proc, k, v_t, bias, batch, seq_len, tq=tq, tkb=tkb)
    x = _matmul(mixed, dsa_w_o, 0, d_model, tm=1024, tn=1024, out_dtype=F32, res=x, name="dsa_out_residual")
    x = _conv_ffn(x, ffn_g, ffn_w_up, ffn_conv_w, conv_b, w_down, 1, seq_len)
    return x.reshape(batch, seq_len, d_model)
```

```python
import functools
import math

import jax
import jax.numpy as jnp
from jax import lax
from jax.experimental import pallas as pl
from jax.experimental.pallas import tpu as pltpu

N_HEADS = 16
HEAD_DIM = 128
DSA_KV_HEADS = 4
DSA_GROUP = N_HEADS // DSA_KV_HEADS
IDX_HEADS = 16
IDX_DIM = 128
TOPK_MAX = 256
D_FF = 5632
CONV_WIDTH = 3
ROPE_THETA = 10000.0
NORM_EPS = 1e-6

V7X_VMEM_BYTES = 64 * 1024 * 1024
VMEM_LIMIT_BYTES = V7X_VMEM_BYTES - 8 * 1024 * 1024

F32 = jnp.float32
BF16 = jnp.bfloat16
MASK_BIAS = -1e30
F32_LOWEST = float(jnp.finfo(jnp.float32).min)
INT32_MIN = -(2 ** 31)
LOG2E = math.log2(math.e)
DSA_Q_SCALE = LOG2E / math.sqrt(HEAD_DIM)


def _params(*sem):
    return pltpu.CompilerParams(dimension_semantics=sem, vmem_limit_bytes=VMEM_LIMIT_BYTES)


def _rmsnorm_kernel(x_ref, g_ref, o_ref):
    x = x_ref[...]
    ms = jnp.mean(x * x, axis=-1, keepdims=True)
    o_ref[...] = ((x * lax.rsqrt(ms + NORM_EPS)) * g_ref[0]).astype(o_ref.dtype)


def _rmsnorm(x, g_all, layer, *, tm=1024):
    m, d = x.shape
    return pl.pallas_call(
        _rmsnorm_kernel,
        grid=(m // tm,),
        in_specs=[
            pl.BlockSpec((tm, d), lambda i: (i, 0)),
            pl.BlockSpec((1, 1, d), lambda i: (layer, 0, 0)),
        ],
        out_specs=pl.BlockSpec((tm, d), lambda i: (i, 0)),
        out_shape=jax.ShapeDtypeStruct((m, d), BF16),
        compiler_params=_params("arbitrary"),
        name="rmsnorm",
    )(x, g_all)


def _matmul_kernel(*refs, has_res, cast_w, scaled_blocks, block_scale):
    a_ref, w_ref, *rest = refs
    r_ref = rest.pop(0) if has_res else None
    o_ref = rest.pop(0)
    if cast_w:
        (w_bf,) = rest

        @pl.when(pl.program_id(1) == 0)
        def _():
            w_bf[...] = w_ref[...].astype(BF16)

        w = w_bf[...]
    else:
        w = w_ref[...]
    out = jnp.dot(a_ref[...], w, preferred_element_type=F32)
    if scaled_blocks:
        out = out * jnp.where(pl.program_id(0) < scaled_blocks, block_scale, 1.0)
    if has_res:
        out = r_ref[...] + out
    o_ref[...] = out.astype(o_ref.dtype)


def _matmul(a, w_all, layer, n_out, *, tm, tn, out_dtype, res=None, scaled_cols=0, col_scale=1.0, name):
    m, k_dim = a.shape
    cast_w = w_all.dtype != BF16
    assert scaled_cols % tn == 0
    in_specs = [
        pl.BlockSpec((tm, k_dim), lambda j, i: (i, 0)),
        pl.BlockSpec((None, k_dim, tn), lambda j, i: (layer, 0, j)),
    ]
    args = [a, w_all]
    if res is not None:
        in_specs.append(pl.BlockSpec((tm, tn), lambda j, i: (i, j)))
        args.append(res)
    return pl.pallas_call(
        functools.partial(_matmul_kernel, has_res=res is not None, cast_w=cast_w,
                          scaled_blocks=scaled_cols // tn, block_scale=col_scale),
        grid=(n_out // tn, m // tm),
        in_specs=in_specs,
        out_specs=pl.BlockSpec((tm, tn), lambda j, i: (i, j)),
        out_shape=jax.ShapeDtypeStruct((m, n_out), out_dtype),
        scratch_shapes=[pltpu.VMEM((k_dim, tn), BF16)] if cast_w else [],
        compiler_params=_params("arbitrary", "arbitrary"),
        name=name,
    )(*args)


def _out_proj_norm_kernel(a_ref, w_ref, r_ref, g_ref, x_ref, h_ref, w_bf):
    @pl.when(pl.program_id(0) == 0)
    def _():
        w_bf[...] = w_ref[...].astype(BF16)

    x = r_ref[...] + jnp.dot(a_ref[...], w_bf[...], preferred_element_type=F32)
    x_ref[...] = x
    ms = jnp.mean(x * x, axis=-1, keepdims=True)
    h_ref[...] = ((x * lax.rsqrt(ms + NORM_EPS)) * g_ref[0]).astype(h_ref.dtype)


def _out_proj_norm(a, w_all, res, g_all, layer, *, tm=512, name):
    m, k_dim = a.shape
    d = w_all.shape[2]
    return pl.pallas_call(
        _out_proj_norm_kernel,
        grid=(m // tm,),
        in_specs=[
            pl.BlockSpec((tm, k_dim), lambda i: (i, 0)),
            pl.BlockSpec((None, k_dim, d), lambda i: (0, 0, 0), pipeline_mode=pl.Buffered(1)),
            pl.BlockSpec((tm, d), lambda i: (i, 0)),
            pl.BlockSpec((1, 1, d), lambda i: (layer, 0, 0)),
        ],
        out_specs=[pl.BlockSpec((tm, d), lambda i: (i, 0)), pl.BlockSpec((tm, d), lambda i: (i, 0))],
        out_shape=[jax.ShapeDtypeStruct((m, d), F32), jax.ShapeDtypeStruct((m, d), BF16)],
        scratch_shapes=[pltpu.VMEM((k_dim, d), BF16)],
        compiler_params=_params("arbitrary"),
        name=name,
    )(a, w_all, res, g_all)


def _cast_kernel(x_ref, o_ref):
    o_ref[...] = x_ref[...].astype(o_ref.dtype)


def _to_bf16(w_all, *, rows=512):
    n_layers, k_dim, n = w_all.shape
    return pl.pallas_call(
        _cast_kernel,
        grid=(n_layers, k_dim // rows),
        in_specs=[pl.BlockSpec((1, rows, n), lambda l, r: (l, r, 0))],
        out_specs=pl.BlockSpec((1, rows, n), lambda l, r: (l, r, 0)),
        out_shape=jax.ShapeDtypeStruct(w_all.shape, BF16),
        compiler_params=_params("arbitrary", "arbitrary"),
        name="weights_to_bf16",
    )(w_all)


def _sb_tile(q, k_blk, v_blk, upper, carry, causal):
    z = lax.dot_general(q, k_blk, (((1,), (1,)), ((), ())), preferred_element_type=F32)
    sp = jnp.maximum(z, 0.0) + jnp.log(1.0 + jnp.exp2(jnp.abs(z) * (-LOG2E)))
    if causal is not None:
        sp = jnp.where(causal, sp, 0.0)
    tail = jnp.dot(sp.astype(BF16), upper, preferred_element_type=F32)
    a = jnp.exp(z - (tail + carry))
    if causal is not None:
        a = jnp.where(causal, a, 0.0)
    out = jnp.dot(a.astype(BF16), v_blk, preferred_element_type=F32)
    return out, carry + jnp.sum(sp, axis=1, keepdims=True)


def _sb_attn_kernel(q_ref, k_ref, v_ref, o_ref, *, tq, tk, heads):
    i = pl.program_id(2)
    per_q = tq // tk
    r2 = lax.broadcasted_iota(jnp.int32, (tk, tk), 0)
    c2 = lax.broadcasted_iota(jnp.int32, (tk, tk), 1)
    upper = jnp.where(r2 >= c2, 1.0, 0.0).astype(BF16)
    rows = lax.broadcasted_iota(jnp.int32, (tq, tk), 0)
    cols = lax.broadcasted_iota(jnp.int32, (tq, tk), 1)
    lanes = [slice(h * HEAD_DIM, (h + 1) * HEAD_DIM) for h in range(heads)]

    def block(j, carries, causal, first_row=0):
        start = pl.multiple_of(j * tk, tk)
        outs = [_sb_tile(q_ref[0, first_row:, lanes[h]], k_ref[0, pl.ds(start, tk), lanes[h]],
                         v_ref[0, pl.ds(start, tk), lanes[h]], upper, carries[h], causal)
                for h in range(heads)]
        return [o for o, _ in outs], [c for _, c in outs]

    carries = [jnp.zeros((tq, 1), F32)] * heads
    accs = [jnp.zeros((tq, HEAD_DIM), F32)] * heads
    for d in reversed(range(per_q)):
        r0 = d * tk
        outs, low = block(i * per_q + d, [c[r0:] for c in carries], (cols + r0 < rows)[r0:], r0)
        if r0:
            carries = [jnp.concatenate([c[:r0], lo], axis=0) for c, lo in zip(carries, low)]
            accs = [jnp.concatenate([acc[:r0], acc[r0:] + o], axis=0) for acc, o in zip(accs, outs)]
        else:
            carries = low
            accs = [acc + o for acc, o in zip(accs, outs)]

    def body(step, state):
        accs, carries = state
        for d in range(per_q):
            outs, carries = block((i - step) * per_q - 1 - d, carries, None)
            accs = [acc + o for acc, o in zip(accs, outs)]
        return accs, carries

    accs, _ = lax.fori_loop(0, i, body, (accs, carries))
    for h in range(heads):
        o_ref[0, :, lanes[h]] = accs[h].astype(o_ref.dtype)


def _sb_attention(qkv, *, tq=512, tk=256, heads=4):
    b, s, _ = qkv.shape
    groups = N_HEADS // heads
    width = heads * HEAD_DIM
    return pl.pallas_call(
        functools.partial(_sb_attn_kernel, tq=tq, tk=tk, heads=heads),
        grid=(b, groups, s // tq),
        in_specs=[
            pl.BlockSpec((1, tq, width), lambda bi, g, i: (bi, i, g)),
            pl.BlockSpec((1, s, width), lambda bi, g, i: (bi, 0, groups + g)),
            pl.BlockSpec((1, s, width), lambda bi, g, i: (bi, 0, 2 * groups + g)),
        ],
        out_specs=pl.BlockSpec((1, tq, width), lambda bi, g, i: (bi, i, g)),
        out_shape=jax.ShapeDtypeStruct((b, s, N_HEADS * HEAD_DIM), BF16),
        compiler_params=_params("arbitrary", "arbitrary", "arbitrary"),
        name="sb_attention",
    )(qkv, qkv, qkv)


HISTORY_ROWS = 8


def _causal_conv(u_ref, cw, cb):
    r0 = HISTORY_ROWS
    rows = u_ref.shape[0] - r0
    return (cb + cw[2:3] * u_ref[r0:r0 + rows] + cw[1:2] * u_ref[r0 - 1:r0 - 1 + rows]
            + cw[0:1] * u_ref[r0 - 2:r0 - 2 + rows])


def _ffn_up_kernel(h_ref, wg_ref, wv_ref, cwg_ref, cwv_ref, cbg_ref, cbv_ref, o_ref,
                   wg_bf, wv_bf, hist_g, hist_v, *stage, tiles_per_seq, chunk_rows, chunk_cols):
    i = pl.program_id(1)
    tm, tn = o_ref.shape
    r0 = HISTORY_ROWS

    @pl.when(i == 0)
    def _():
        wg_bf[...] = wg_ref[...].astype(BF16)
        wv_bf[...] = wv_ref[...].astype(BF16)

    @pl.when(i % tiles_per_seq == 0)
    def _():
        hist_g[...] = jnp.zeros_like(hist_g)
        hist_v[...] = jnp.zeros_like(hist_v)

    piece = 0
    for c0 in range(0, tn, chunk_cols):
        cols = slice(c0, c0 + chunk_cols)
        prev_g = hist_g[:, cols]
        prev_v = hist_v[:, cols]
        for t0 in range(0, tm, chunk_rows):
            ug_s, uv_s = stage[2 * (piece % 2)], stage[2 * (piece % 2) + 1]
            piece += 1
            h = h_ref[t0:t0 + chunk_rows, :]
            ug_s[0:r0] = prev_g
            uv_s[0:r0] = prev_v
            ug_s[r0:] = jnp.dot(h, wg_bf[:, cols], preferred_element_type=F32)
            uv_s[r0:] = jnp.dot(h, wv_bf[:, cols], preferred_element_type=F32)
            gate = _causal_conv(ug_s, cwg_ref[:, cols], cbg_ref[0, :, cols])
            val = _causal_conv(uv_s, cwv_ref[:, cols], cbv_ref[0, :, cols])
            o_ref[t0:t0 + chunk_rows, cols] = (jax.nn.silu(gate) * val).astype(o_ref.dtype)
            prev_g = ug_s[chunk_rows:]
            prev_v = uv_s[chunk_rows:]
        hist_g[:, cols] = prev_g
        hist_v[:, cols] = prev_v


def _ffn_up(h, w_up_all, conv_w_all, conv_b_all, layer, seq_len, *, tm=2048, tn=512,
            chunk_rows=512, chunk_cols=256):
    m, d = h.shape
    nj = D_FF // tn
    return pl.pallas_call(
        functools.partial(_ffn_up_kernel, tiles_per_seq=seq_len // tm, chunk_rows=chunk_rows,
                          chunk_cols=chunk_cols),
        grid=(nj, m // tm),
        in_specs=[
            pl.BlockSpec((tm, d), lambda j, i: (i, 0)),
            pl.BlockSpec((None, d, tn), lambda j, i: (layer, 0, j)),
            pl.BlockSpec((None, d, tn), lambda j, i: (layer, 0, nj + j)),
            pl.BlockSpec((None, CONV_WIDTH, tn), lambda j, i: (layer, 0, j)),
            pl.BlockSpec((None, CONV_WIDTH, tn), lambda j, i: (layer, 0, nj + j)),
            pl.BlockSpec((1, 1, tn), lambda j, i: (layer, 0, j)),
            pl.BlockSpec((1, 1, tn), lambda j, i: (layer, 0, nj + j)),
        ],
        out_specs=pl.BlockSpec((tm, tn), lambda j, i: (i, j)),
        out_shape=jax.ShapeDtypeStruct((m, D_FF), BF16),
        scratch_shapes=[
            pltpu.VMEM((d, tn), BF16),
            pltpu.VMEM((d, tn), BF16),
            pltpu.VMEM((HISTORY_ROWS, tn), F32),
            pltpu.VMEM((HISTORY_ROWS, tn), F32),
        ] + [pltpu.VMEM((HISTORY_ROWS + chunk_rows, chunk_cols), F32)] * 4,
        compiler_params=_params("arbitrary", "arbitrary"),
        name="ffn_up_conv_gate",
    )(h, w_up_all, w_up_all, conv_w_all, conv_w_all, conv_b_all, conv_b_all)


def _head_norm_rope_t(x, gain, cos_t, sin_t):
    if gain is not None:
        ms = jnp.mean(x * x, axis=0, keepdims=True)
        x = (x * lax.rsqrt(ms + NORM_EPS)) * gain
    return x * cos_t + pltpu.roll(x, HEAD_DIM // 2, 0) * sin_t


DSA_K_ROW = N_HEADS * HEAD_DIM
DSA_V_ROW = DSA_K_ROW + DSA_KV_HEADS * HEAD_DIM
DSA_QI_ROW = DSA_V_ROW + DSA_KV_HEADS * HEAD_DIM
DSA_MAIN_ROWS = DSA_QI_ROW + IDX_HEADS * IDX_DIM


def _dsa_in_kernel(h_ref, w_ref, cos_ref, sin_ref, qg_ref, kg_ref, o_ref, w_bf, *, tn):
    j = pl.program_id(0)

    @pl.when(pl.program_id(1) == 0)
    def _():
        w_bf[...] = w_ref[...].astype(BF16)

    def head_kind(row):
        return "q" if row < DSA_K_ROW else "k" if row < DSA_V_ROW else "v" if row < DSA_QI_ROW else "qi"

    def finish(block):
        out = lax.dot_general(w_bf[...], h_ref[...], (((1,), (1,)), ((), ())), preferred_element_type=F32)
        cos, sin = cos_ref[...], sin_ref[...]
        for n in range(tn // HEAD_DIM):
            sl = slice(n * HEAD_DIM, (n + 1) * HEAD_DIM)
            kind = head_kind(block * tn + n * HEAD_DIM)
            x = out[sl, :]
            if kind == "q":
                x = _head_norm_rope_t(x, qg_ref[...] * DSA_Q_SCALE, cos, sin)
            elif kind == "k":
                x = _head_norm_rope_t(x, kg_ref[...], cos, sin)
            elif kind == "qi":
                x = _head_norm_rope_t(x, None, cos, sin)
            o_ref[sl, :] = x.astype(BF16)

    for block in range(DSA_MAIN_ROWS // tn):
        pl.when(j == block)(functools.partial(finish, block))


def _dsa_in_proj(h, w_in_t, cos_t, sin_t, q_gain, k_gain, seq_len, *, tm=1024, tn=1024):
    m, d = h.shape
    tiles_per_seq = seq_len // tm
    gain_spec = pl.BlockSpec((HEAD_DIM, 1), lambda j, i: (0, 0))
    tab_spec = pl.BlockSpec((HEAD_DIM, tm), lambda j, i: (0, i % tiles_per_seq))
    return pl.pallas_call(
        functools.partial(_dsa_in_kernel, tn=tn),
        grid=(DSA_MAIN_ROWS // tn, m // tm),
        in_specs=[
            pl.BlockSpec((tm, d), lambda j, i: (i, 0)),
            pl.BlockSpec((tn, d), lambda j, i: (j, 0)),
            tab_spec, tab_spec, gain_spec, gain_spec,
        ],
        out_specs=pl.BlockSpec((tn, tm), lambda j, i: (j, i)),
        out_shape=jax.ShapeDtypeStruct((DSA_MAIN_ROWS, m), BF16),
        scratch_shapes=[pltpu.VMEM((tn, d), BF16)],
        compiler_params=_params("arbitrary", "arbitrary"),
        name="dsa_in_proj",
    )(h, w_in_t, cos_t, sin_t, q_gain, k_gain)


def _dsa_tail_kernel(h_ref, w_ref, o_ref):
    o_ref[...] = lax.dot_general(w_ref[...].astype(BF16), h_ref[...], (((1,), (1,)), ((), ())),
                                 preferred_element_type=F32)


def _dsa_in_proj_tail(h, w_tail_t, *, tm=2048):
    m, d = h.shape
    rows = w_tail_t.shape[0]
    return pl.pallas_call(
        _dsa_tail_kernel,
        grid=(m // tm,),
        in_specs=[pl.BlockSpec((tm, d), lambda i: (i, 0)), pl.BlockSpec((rows, d), lambda i: (0, 0))],
        out_specs=pl.BlockSpec((rows, tm), lambda i: (0, i)),
        out_shape=jax.ShapeDtypeStruct((rows, m), F32),
        compiler_params=_params("arbitrary"),
        name="dsa_in_proj_tail",
    )(h, w_tail_t)


def _dsa_prep_kernel(k_ref, v_ref, t_ref, cos_ref, sin_ref, ig_ref, ko_ref, vt_ref, ki_ref):
    ko_ref[...] = k_ref[...].astype(F32).T.astype(BF16)
    vt_ref[...] = v_ref[...]
    ki_t = _head_norm_rope_t(t_ref[0:IDX_DIM, :], ig_ref[...], cos_ref[...], sin_ref[...])
    ki_ref[...] = ki_t.T.astype(BF16)


def _dsa_prep(proc_t, tail_t, cos_t, sin_t, ik_gain, batch, seq_len, *, tm=512, tkc=512):
    m = proc_t.shape[1]
    tiles_per_seq = seq_len // tm
    tiles_per_chunk = tkc // tm
    hk = DSA_KV_HEADS * HEAD_DIM

    def v_block(i):
        pos_tile = i % tiles_per_seq
        return i // tiles_per_seq, pos_tile // tiles_per_chunk, 0, pos_tile % tiles_per_chunk

    tab_spec = pl.BlockSpec((HEAD_DIM, tm), lambda i: (0, i % tiles_per_seq))
    return pl.pallas_call(
        _dsa_prep_kernel,
        grid=(m // tm,),
        in_specs=[
            pl.BlockSpec((hk, tm), lambda i: (DSA_K_ROW // hk, i)),
            pl.BlockSpec((hk, tm), lambda i: (DSA_V_ROW // hk, i)),
            pl.BlockSpec((tail_t.shape[0], tm), lambda i: (0, i)),
            tab_spec, tab_spec,
            pl.BlockSpec((HEAD_DIM, 1), lambda i: (0, 0)),
        ],
        out_specs=[
            pl.BlockSpec((tm, hk), lambda i: (i, 0)),
            pl.BlockSpec((None, None, hk, tm), v_block),
            pl.BlockSpec((tm, IDX_DIM), lambda i: (i, 0)),
        ],
        out_shape=[
            jax.ShapeDtypeStruct((m, hk), BF16),
            jax.ShapeDtypeStruct((batch, seq_len // tkc, hk, tkc), BF16),
            jax.ShapeDtypeStruct((m, IDX_DIM), BF16),
        ],
        compiler_params=_params("arbitrary"),
        name="dsa_kv_layout_kidx",
    )(proc_t, proc_t, tail_t, cos_t, sin_t, ik_gain)


def _key_to_f32(key):
    bits = jnp.where(key < 0, key ^ INT32_MIN, ~key)
    return lax.bitcast_convert_type(bits, F32)


def _key16_to_bf16(key):
    bits = jnp.where(key < 0, key ^ INT32_MIN, ~key) & -65536
    return lax.bitcast_convert_type(bits, F32).astype(BF16)


def _indexer_kernel(qi_lo_ref, qi_hi_ref, ki_ref, w_ref, bias_ref, sc_ref, hi_ref, cnt_ref, cnt_hi_ref, *,
                    tq, tkc, tkb, nkc, topk):
    i = pl.program_id(1)
    t0 = i * tq
    w = w_ref[...] * (1.0 / math.sqrt(IDX_DIM) / math.sqrt(IDX_HEADS))
    key_pos = lax.broadcasted_iota(jnp.int32, (tkc, tq), 0)
    qry_pos = lax.broadcasted_iota(jnp.int32, (tkc, tq), 1) + t0

    def chunk_in_range(c):
        return c * tkc < t0 + tq

    for c in range(nkc):

        @pl.when(chunk_in_range(c))
        def _():
            k_blk = ki_ref[c * tkc:(c + 1) * tkc, :]
            acc = jnp.zeros((tkc, tq), F32)
            for h in range(IDX_HEADS):
                qi_ref, hh = (qi_lo_ref, h) if h < IDX_HEADS // 2 else (qi_hi_ref, h - IDX_HEADS // 2)
                logit = jnp.dot(k_blk, qi_ref[hh * IDX_DIM:(hh + 1) * IDX_DIM, :],
                                preferred_element_type=F32)
                acc = acc + jnp.maximum(logit, 0.0) * w[h:h + 1, :]
            score = jnp.where(key_pos + c * tkc <= qry_pos, acc, -jnp.inf)
            sc_ref[c] = score
            hi_ref[c] = score.astype(BF16)

        @pl.when(jnp.logical_not(chunk_in_range(c)))
        def _():
            sc_ref[c] = jnp.full((tkc, tq), -jnp.inf, F32)
            hi_ref[c] = jnp.full((tkc, tq), -jnp.inf, BF16)

    def sublane_partial(hit):
        return jnp.sum(hit.reshape(tkc // 8, 8, tq), axis=0)

    def count_ge(thr):
        cnt_ref[...] = sublane_partial(jnp.where(sc_ref[0] >= thr, 1.0, 0.0))
        for c in range(1, nkc):

            @pl.when(chunk_in_range(c))
            def _():
                cnt_ref[...] += sublane_partial(jnp.where(sc_ref[c] >= thr, 1.0, 0.0))

        return jnp.sum(cnt_ref[...], axis=0, keepdims=True)

    def count_hi_ge(thr_hi):
        one, zero = jnp.ones((), BF16), jnp.zeros((), BF16)

        def partial(c):
            hit = jnp.where(hi_ref[c] >= thr_hi, one, zero)
            return functools.reduce(lambda a, b: a + b, [hit[r:r + 16] for r in range(0, tkc, 16)])

        cnt_hi_ref[...] = partial(0)
        for c in range(1, nkc):

            @pl.when(chunk_in_range(c))
            def _():
                cnt_hi_ref[...] += partial(c)

        return jnp.sum(cnt_hi_ref[...].astype(F32), axis=0, keepdims=True)

    def store_bias(c, bias):
        for part in range(tkc // tkb):
            bias_ref[0, 0, c * (tkc // tkb) + part] = bias[part * tkb:(part + 1) * tkb]

    def write_bias(select_fn):
        for c in range(nkc):
            store_bias(c, jnp.where(select_fn(sc_ref[c]), 0.0, MASK_BIAS))

    @pl.when(t0 + tq <= topk)
    def _():
        write_bias(lambda sc: sc >= F32_LOWEST)

    @pl.when(t0 + tq > topk)
    def _():
        def coarse_step(step, key):
            trial = key | lax.shift_left(jnp.int32(1), 31 - step)
            keep = count_hi_ge(_key16_to_bf16(trial)) >= topk
            return jnp.where(keep, trial, key)

        coarse = lax.fori_loop(0, 16, coarse_step, jnp.zeros((1, tq), jnp.int32))
        coarse_key = coarse | jnp.where(coarse < 0, 0, 0xFFFF)
        base = coarse_key - 0x8000

        def fine_step(step, offset):
            trial = offset | lax.shift_left(jnp.int32(1), 16 - step)
            keep = count_ge(_key_to_f32(base + trial)) >= topk
            return jnp.where(keep, trial, offset)

        key = base + lax.fori_loop(0, 17, fine_step, jnp.zeros((1, tq), jnp.int32))
        few = lax.broadcasted_iota(jnp.int32, (1, tq), 1) + t0 < topk - 1
        thr = jnp.where(few, F32_LOWEST, _key_to_f32(key))
        write_bias(lambda sc: sc >= thr)

        n_ge = count_ge(thr)

        @pl.when(jnp.max(jnp.where(few, 0.0, n_ge)) > topk)
        def _():
            n_gt = sum(jnp.sum(jnp.where(sc_ref[c] > thr, 1.0, 0.0), axis=0, keepdims=True)
                       for c in range(nkc))
            room = jnp.where(few, float(tkc * nkc), topk - n_gt)
            r2 = lax.broadcasted_iota(jnp.int32, (tkc, tkc), 0)
            c2 = lax.broadcasted_iota(jnp.int32, (tkc, tkc), 1)
            before = jnp.where(c2 < r2, 1.0, 0.0).astype(BF16)
            seen = jnp.zeros((1, tq), F32)
            for c in range(nkc):
                sc = sc_ref[c]
                tied = jnp.where(sc == thr, 1.0, 0.0)
                rank = seen + jnp.dot(before, tied.astype(BF16), preferred_element_type=F32)
                take = jnp.where(sc > thr, 1.0, jnp.where(rank < room, tied, 0.0))
                store_bias(c, jnp.where(take > 0.0, 0.0, MASK_BIAS))
                seen = seen + jnp.sum(tied, axis=0, keepdims=True)


def _dsa_indexer(proc, ki, w_t, batch, seq_len, topk, *, tq, tkb, tkc=512):
    half = IDX_HEADS * IDX_DIM // 2
    qi_block = DSA_QI_ROW // half
    nq = seq_len // tq
    nkc = seq_len // tkc
    nkb = seq_len // tkb
    return pl.pallas_call(
        functools.partial(_indexer_kernel, tq=tq, tkc=tkc, tkb=tkb, nkc=nkc, topk=topk),
        grid=(batch, nq),
        in_specs=[
            pl.BlockSpec((half, tq), lambda b, i: (qi_block, b * nq + i)),
            pl.BlockSpec((half, tq), lambda b, i: (qi_block + 1, b * nq + i)),
            pl.BlockSpec((seq_len, IDX_DIM), lambda b, i: (b, 0)),
            pl.BlockSpec((IDX_HEADS, tq), lambda b, i: (0, b * nq + i)),
        ],
        out_specs=pl.BlockSpec((1, 1, nkb, tkb, tq), lambda b, i: (b, i, 0, 0, 0)),
        out_shape=jax.ShapeDtypeStruct((batch, nq, nkb, tkb, tq), F32),
        scratch_shapes=[pltpu.VMEM((nkc, tkc, tq), F32), pltpu.VMEM((nkc, tkc, tq), BF16),
                        pltpu.VMEM((8, tq), F32), pltpu.VMEM((16, tq), BF16)],
        compiler_params=_params("arbitrary", "arbitrary"),
        name="dsa_indexer_topk",
    )(proc, proc, ki, w_t)


def _dsa_attn_kernel(q_ref, k_ref, vt_ref, bias_ref, o_ref, *, tq, tkb, groups, chain_heads):
    i = pl.program_id(1)
    n_chunks = ((i + 1) * tq + tkb - 1) // tkb
    cols = chain_heads * tq
    chains = [(g, r0) for g in range(groups) for r0 in range(0, DSA_GROUP, chain_heads)]

    def head_lanes(g, r):
        h = g * DSA_GROUP + r
        return slice(h * HEAD_DIM, (h + 1) * HEAD_DIM)

    qs = [jnp.concatenate([q_ref[head_lanes(g, r0 + r), :] for r in range(chain_heads)], axis=1)
          for g, r0 in chains]

    def over_keys(x, op):
        return op(op(x.reshape(tkb // 8, 8, cols), axis=0), axis=0, keepdims=True)

    def kv_lanes(n):
        g, _ = chains[n]
        return slice(g * HEAD_DIM, (g + 1) * HEAD_DIM)

    def scores(n, c):
        k_blk = k_ref[pl.ds(pl.multiple_of(c * tkb, tkb), tkb), kv_lanes(n)]
        return jnp.dot(k_blk, qs[n], preferred_element_type=F32)

    def softmax_step(n, c, state, raw, bias):
        m_run, l_run, acc = state
        s = bias + raw
        m_new = jnp.maximum(m_run, over_keys(s, jnp.max))
        alpha = jnp.exp2(m_run - m_new)
        p = jnp.exp2(s - m_new)
        l_run = alpha * l_run + over_keys(p, jnp.sum)
        acc = alpha * acc + jnp.dot(vt_ref[0, c, kv_lanes(n), :], p.astype(BF16),
                                    preferred_element_type=F32)
        return m_new, l_run, acc

    def body(c, states):
        bias = jnp.concatenate([bias_ref[0, 0, c]] * chain_heads, axis=1)
        raws = [scores(n, c) for n in range(len(chains))]
        return [softmax_step(n, c, states[n], raws[n], bias) for n in range(len(chains))]

    init = (jnp.full((1, cols), -jnp.inf, F32), jnp.zeros((1, cols), F32),
            jnp.zeros((HEAD_DIM, cols), F32))
    states = lax.fori_loop(0, n_chunks, body, [init] * len(chains))
    for (g, r0), (_, l_run, acc) in zip(chains, states):
        out_t = acc / l_run
        for r in range(chain_heads):
            o_ref[:, head_lanes(g, r0 + r)] = out_t[:, r * tq:(r + 1) * tq].T.astype(o_ref.dtype)


def _dsa_attention(proc, k, v_t, bias, batch, seq_len, *, tq, tkb, groups=2, chain_heads=2):
    nq = seq_len // tq
    gw = groups * DSA_GROUP * HEAD_DIM
    kw = groups * HEAD_DIM
    nkb = seq_len // tkb
    return pl.pallas_call(
        functools.partial(_dsa_attn_kernel, tq=tq, tkb=tkb, groups=groups, chain_heads=chain_heads),
        grid=(batch, nq, DSA_KV_HEADS // groups),
        in_specs=[
            pl.BlockSpec((gw, tq), lambda b, i, g: (g, b * nq + i)),
            pl.BlockSpec((seq_len, kw), lambda b, i, g: (b, g)),
            pl.BlockSpec((1, nkb, kw, tkb), lambda b, i, g: (b, 0, g, 0)),
            pl.BlockSpec((1, 1, nkb, tkb, tq), lambda b, i, g: (b, i, 0, 0, 0)),
        ],
        out_specs=pl.BlockSpec((tq, gw), lambda b, i, g: (b * nq + i, g)),
        out_shape=jax.ShapeDtypeStruct((batch * seq_len, N_HEADS * HEAD_DIM), BF16),
        compiler_params=_params("arbitrary", "arbitrary", "arbitrary"),
        name="dsa_attention",
    )(proc, k, v_t, bias)


def _rope_tables_t(seq_len, dim):
    inv_freq = 1.0 / (ROPE_THETA ** (jnp.arange(0, dim, 2, dtype=F32) / dim))
    ang = jnp.arange(seq_len, dtype=F32)[:, None] * inv_freq[None, :]
    cos, sin = jnp.cos(ang).T, jnp.sin(ang).T
    return jnp.concatenate([cos, cos], axis=0), jnp.concatenate([-sin, sin], axis=0)


def _conv_ffn(x, h, w_up, conv_w, conv_b, w_down, layer, seq_len):
    gated = _ffn_up(h, w_up, conv_w, conv_b, layer, seq_len)
    return _matmul(gated, w_down, layer, x.shape[1], tm=512, tn=1024, out_dtype=F32, res=x,
                   name="ffn_down_residual")


def kernel(x, attn_norm_g, ffn_norm_g, sb_w_qkv, sb_w_o, dsa_w_in, dsa_q_norm_g, dsa_k_norm_g,
           dsa_ik_norm_g, dsa_w_o, ffn_w_up, ffn_conv_w, ffn_conv_b, ffn_w_down):
    batch, seq_len, d_model = x.shape
    m = batch * seq_len
    hd = N_HEADS * HEAD_DIM
    x = x.reshape(m, d_model)
    attn_g = attn_norm_g[:, None, :]
    ffn_g = ffn_norm_g[:, None, :]
    conv_b = ffn_conv_b[:, None, :]
    w_down = _to_bf16(ffn_w_down)

    h = _rmsnorm(x, attn_g, 0)
    qkv = _matmul(h, sb_w_qkv, 0, 3 * hd, tm=2048, tn=1024, out_dtype=BF16, scaled_cols=hd,
                  col_scale=1.0 / math.sqrt(HEAD_DIM), name="sb_qkv_proj")
    mixed = _sb_attention(qkv.reshape(batch, seq_len, 3 * hd)).reshape(m, hd)
    x, h = _out_proj_norm(mixed, sb_w_o, x, ffn_g, 0, name="sb_out_residual_norm")
    x = _conv_ffn(x, h, ffn_w_up, ffn_conv_w, conv_b, w_down, 0, seq_len)

    h = _rmsnorm(x, attn_g, 1)
    w_in_t = jnp.swapaxes(dsa_w_in[0], 0, 1)
    cos_t, sin_t = _rope_tables_t(seq_len, HEAD_DIM)
    proc = _dsa_in_proj(h, w_in_t, cos_t, sin_t, dsa_q_norm_g[0][:, None], dsa_k_norm_g[0][:, None], seq_len)
    tail_t = _dsa_in_proj_tail(h, w_in_t[DSA_MAIN_ROWS:])
    tq, tkb = 256, 512
    k, v_t, ki = _dsa_prep(proc, tail_t, cos_t, sin_t, dsa_ik_norm_g[0][:, None], batch, seq_len, tkc=tkb)
    topk = min(TOPK_MAX, seq_len // 4)
    w_t = tail_t[IDX_DIM:]
    bias = _dsa_indexer(proc, ki, w_t, batch, seq_len, topk, tq=tq, tkb=tkb)
    mixed = _dsa_attention(proc, k, v_t, bias, batch, seq_len, tq=tq, tkb=tkb)
    x, h = _out_proj_norm(mixed, dsa_w_o, x, ffn_g, 1, name="dsa_out_residual_norm")
    x = _conv_ffn(x, h, ffn_w_up, ffn_conv_w, conv_b, w_down, 1, seq_len)
    return x.reshape(batch, seq_len, d_model)
```

```python
import functools
import math

import jax
import jax.numpy as jnp
from jax import lax
from jax.experimental import pallas as pl
from jax.experimental.pallas import tpu as pltpu

N_HEADS = 16
HEAD_DIM = 128
DSA_KV_HEADS = 4
DSA_GROUP = N_HEADS // DSA_KV_HEADS
IDX_HEADS = 16
IDX_DIM = 128
TOPK_MAX = 256
D_FF = 5632
CONV_WIDTH = 3
ROPE_THETA = 10000.0
NORM_EPS = 1e-6

V7X_VMEM_BYTES = 64 * 1024 * 1024
VMEM_LIMIT_BYTES = V7X_VMEM_BYTES - 8 * 1024 * 1024

F32 = jnp.float32
BF16 = jnp.bfloat16
MASK_BIAS = -1e30
F32_LOWEST = float(jnp.finfo(jnp.float32).min)
INT32_MIN = -(2 ** 31)
LOG2E = math.log2(math.e)
DSA_Q_SCALE = LOG2E / math.sqrt(HEAD_DIM)


def _params(*sem):
    return pltpu.CompilerParams(dimension_semantics=sem, vmem_limit_bytes=VMEM_LIMIT_BYTES)


def _rmsnorm_kernel(x_ref, g_ref, o_ref):
    x = x_ref[...]
    ms = jnp.mean(x * x, axis=-1, keepdims=True)
    o_ref[...] = ((x * lax.rsqrt(ms + NORM_EPS)) * g_ref[0]).astype(o_ref.dtype)


def _rmsnorm(x, g_all, layer, *, tm=1024):
    m, d = x.shape
    return pl.pallas_call(
        _rmsnorm_kernel,
        grid=(m // tm,),
        in_specs=[
            pl.BlockSpec((tm, d), lambda i: (i, 0)),
            pl.BlockSpec((1, 1, d), lambda i: (layer, 0, 0)),
        ],
        out_specs=pl.BlockSpec((tm, d), lambda i: (i, 0)),
        out_shape=jax.ShapeDtypeStruct((m, d), BF16),
        compiler_params=_params("arbitrary"),
        name="rmsnorm",
    )(x, g_all)


def _matmul_kernel(*refs, has_res, cast_w, scaled_blocks, block_scale):
    a_ref, w_ref, *rest = refs
    r_ref = rest.pop(0) if has_res else None
    o_ref = rest.pop(0)
    if cast_w:
        (w_bf,) = rest

        @pl.when(pl.program_id(1) == 0)
        def _():
            w_bf[...] = w_ref[...].astype(BF16)

        w = w_bf[...]
    else:
        w = w_ref[...]
    out = jnp.dot(a_ref[...], w, preferred_element_type=F32)
    if scaled_blocks:
        out = out * jnp.where(pl.program_id(0) < scaled_blocks, block_scale, 1.0)
    if has_res:
        out = r_ref[...] + out
    o_ref[...] = out.astype(o_ref.dtype)


def _matmul(a, w_all, layer, n_out, *, tm, tn, out_dtype, res=None, scaled_cols=0, col_scale=1.0, name):
    m, k_dim = a.shape
    cast_w = w_all.dtype != BF16
    assert scaled_cols % tn == 0
    in_specs = [
        pl.BlockSpec((tm, k_dim), lambda j, i: (i, 0)),
        pl.BlockSpec((None, k_dim, tn), lambda j, i: (layer, 0, j)),
    ]
    args = [a, w_all]
    if res is not None:
        in_specs.append(pl.BlockSpec((tm, tn), lambda j, i: (i, j)))
        args.append(res)
    return pl.pallas_call(
        functools.partial(_matmul_kernel, has_res=res is not None, cast_w=cast_w,
                          scaled_blocks=scaled_cols // tn, block_scale=col_scale),
        grid=(n_out // tn, m // tm),
        in_specs=in_specs,
        out_specs=pl.BlockSpec((tm, tn), lambda j, i: (i, j)),
        out_shape=jax.ShapeDtypeStruct((m, n_out), out_dtype),
        scratch_shapes=[pltpu.VMEM((k_dim, tn), BF16)] if cast_w else [],
        compiler_params=_params("arbitrary", "arbitrary"),
        name=name,
    )(*args)


def _out_proj_norm_kernel(a_ref, w_ref, r_ref, g_ref, x_ref, h_ref, w_bf):
    @pl.when(pl.program_id(0) == 0)
    def _():
        w_bf[...] = w_ref[...].astype(BF16)

    x = r_ref[...] + jnp.dot(a_ref[...], w_bf[...], preferred_element_type=F32)
    x_ref[...] = x
    ms = jnp.mean(x * x, axis=-1, keepdims=True)
    h_ref[...] = ((x * lax.rsqrt(ms + NORM_EPS)) * g_ref[0]).astype(h_ref.dtype)


def _out_proj_norm(a, w_all, res, g_all, layer, *, tm=512, name):
    m, k_dim = a.shape
    d = w_all.shape[2]
    return pl.pallas_call(
        _out_proj_norm_kernel,
        grid=(m // tm,),
        in_specs=[
            pl.BlockSpec((tm, k_dim), lambda i: (i, 0)),
            pl.BlockSpec((None, k_dim, d), lambda i: (0, 0, 0), pipeline_mode=pl.Buffered(1)),
            pl.BlockSpec((tm, d), lambda i: (i, 0)),
            pl.BlockSpec((1, 1, d), lambda i: (layer, 0, 0)),
        ],
        out_specs=[pl.BlockSpec((tm, d), lambda i: (i, 0)), pl.BlockSpec((tm, d), lambda i: (i, 0))],
        out_shape=[jax.ShapeDtypeStruct((m, d), F32), jax.ShapeDtypeStruct((m, d), BF16)],
        scratch_shapes=[pltpu.VMEM((k_dim, d), BF16)],
        compiler_params=_params("arbitrary"),
        name=name,
    )(a, w_all, res, g_all)


def _cast_kernel(x_ref, o_ref):
    o_ref[...] = x_ref[...].astype(o_ref.dtype)


def _to_bf16(w_all, *, rows=512):
    n_layers, k_dim, n = w_all.shape
    return pl.pallas_call(
        _cast_kernel,
        grid=(n_layers, k_dim // rows),
        in_specs=[pl.BlockSpec((1, rows, n), lambda l, r: (l, r, 0))],
        out_specs=pl.BlockSpec((1, rows, n), lambda l, r: (l, r, 0)),
        out_shape=jax.ShapeDtypeStruct(w_all.shape, BF16),
        compiler_params=_params("arbitrary", "arbitrary"),
        name="weights_to_bf16",
    )(w_all)


def _sb_tile(q, k_blk, v_blk, upper, carry, causal):
    z = lax.dot_general(q, k_blk, (((1,), (1,)), ((), ())), preferred_element_type=F32)
    sp = jnp.maximum(z, 0.0) + jnp.log(1.0 + jnp.exp2(jnp.abs(z) * (-LOG2E)))
    if causal is not None:
        sp = jnp.where(causal, sp, 0.0)
    tail = jnp.dot(sp.astype(BF16), upper, preferred_element_type=F32)
    a = jnp.exp(z - (tail + carry))
    if causal is not None:
        a = jnp.where(causal, a, 0.0)
    out = jnp.dot(a.astype(BF16), v_blk, preferred_element_type=F32)
    return out, carry + jnp.sum(sp, axis=1, keepdims=True)


def _sb_attn_kernel(q_ref, k_ref, v_ref, o_ref, *, tq, tk, heads):
    i = pl.program_id(2)
    per_q = tq // tk
    r2 = lax.broadcasted_iota(jnp.int32, (tk, tk), 0)
    c2 = lax.broadcasted_iota(jnp.int32, (tk, tk), 1)
    upper = jnp.where(r2 >= c2, 1.0, 0.0).astype(BF16)
    rows = lax.broadcasted_iota(jnp.int32, (tq, tk), 0)
    cols = lax.broadcasted_iota(jnp.int32, (tq, tk), 1)
    lanes = [slice(h * HEAD_DIM, (h + 1) * HEAD_DIM) for h in range(heads)]

    def block(j, carries, causal, first_row=0):
        start = pl.multiple_of(j * tk, tk)
        outs = [_sb_tile(q_ref[0, first_row:, lanes[h]], k_ref[0, pl.ds(start, tk), lanes[h]],
                         v_ref[0, pl.ds(start, tk), lanes[h]], upper, carries[h], causal)
                for h in range(heads)]
        return [o for o, _ in outs], [c for _, c in outs]

    carries = [jnp.zeros((tq, 1), F32)] * heads
    accs = [jnp.zeros((tq, HEAD_DIM), F32)] * heads
    for d in reversed(range(per_q)):
        r0 = d * tk
        outs, low = block(i * per_q + d, [c[r0:] for c in carries], (cols + r0 < rows)[r0:], r0)
        if r0:
            carries = [jnp.concatenate([c[:r0], lo], axis=0) for c, lo in zip(carries, low)]
            accs = [jnp.concatenate([acc[:r0], acc[r0:] + o], axis=0) for acc, o in zip(accs, outs)]
        else:
            carries = low
            accs = [acc + o for acc, o in zip(accs, outs)]

    def body(step, state):
        accs, carries = state
        for d in range(per_q):
            outs, carries = block((i - step) * per_q - 1 - d, carries, None)
            accs = [acc + o for acc, o in zip(accs, outs)]
        return accs, carries

    accs, _ = lax.fori_loop(0, i, body, (accs, carries))
    for h in range(heads):
        o_ref[0, :, lanes[h]] = accs[h].astype(o_ref.dtype)


def _sb_attention(qkv, *, tq=512, tk=256, heads=4):
    b, s, _ = qkv.shape
    groups = N_HEADS // heads
    width = heads * HEAD_DIM
    return pl.pallas_call(
        functools.partial(_sb_attn_kernel, tq=tq, tk=tk, heads=heads),
        grid=(b, groups, s // tq),
        in_specs=[
            pl.BlockSpec((1, tq, width), lambda bi, g, i: (bi, i, g)),
            pl.BlockSpec((1, s, width), lambda bi, g, i: (bi, 0, groups + g)),
            pl.BlockSpec((1, s, width), lambda bi, g, i: (bi, 0, 2 * groups + g)),
        ],
        out_specs=pl.BlockSpec((1, tq, width), lambda bi, g, i: (bi, i, g)),
        out_shape=jax.ShapeDtypeStruct((b, s, N_HEADS * HEAD_DIM), BF16),
        compiler_params=_params("arbitrary", "arbitrary", "arbitrary"),
        name="sb_attention",
    )(qkv, qkv, qkv)


HISTORY_ROWS = 8


def _causal_conv(u_ref, cw, cb):
    r0 = HISTORY_ROWS
    rows = u_ref.shape[0] - r0
    return (cb + cw[2:3] * u_ref[r0:r0 + rows] + cw[1:2] * u_ref[r0 - 1:r0 - 1 + rows]
            + cw[0:1] * u_ref[r0 - 2:r0 - 2 + rows])


def _ffn_up_kernel(h_ref, wg_ref, wv_ref, cwg_ref, cwv_ref, cbg_ref, cbv_ref, o_ref,
                   wg_bf, wv_bf, hist_g, hist_v, *stage, tiles_per_seq, chunk_rows, chunk_cols):
    i = pl.program_id(1)
    tm, tn = o_ref.shape
    r0 = HISTORY_ROWS

    @pl.when(i == 0)
    def _():
        wg_bf[...] = wg_ref[...].astype(BF16)
        wv_bf[...] = wv_ref[...].astype(BF16)

    @pl.when(i % tiles_per_seq == 0)
    def _():
        hist_g[...] = jnp.zeros_like(hist_g)
        hist_v[...] = jnp.zeros_like(hist_v)

    piece = 0
    for c0 in range(0, tn, chunk_cols):
        cols = slice(c0, c0 + chunk_cols)
        prev_g = hist_g[:, cols]
        prev_v = hist_v[:, cols]
        for t0 in range(0, tm, chunk_rows):
            ug_s, uv_s = stage[2 * (piece % 2)], stage[2 * (piece % 2) + 1]
            piece += 1
            h = h_ref[t0:t0 + chunk_rows, :]
            ug_s[0:r0] = prev_g
            uv_s[0:r0] = prev_v
            ug_s[r0:] = jnp.dot(h, wg_bf[:, cols], preferred_element_type=F32)
            uv_s[r0:] = jnp.dot(h, wv_bf[:, cols], preferred_element_type=F32)
            gate = _causal_conv(ug_s, cwg_ref[:, cols], cbg_ref[0, :, cols])
            val = _causal_conv(uv_s, cwv_ref[:, cols], cbv_ref[0, :, cols])
            o_ref[t0:t0 + chunk_rows, cols] = (jax.nn.silu(gate) * val).astype(o_ref.dtype)
            prev_g = ug_s[chunk_rows:]
            prev_v = uv_s[chunk_rows:]
        hist_g[:, cols] = prev_g
        hist_v[:, cols] = prev_v


def _ffn_up(h, w_up_all, conv_w_all, conv_b_all, layer, seq_len, *, tm=2048, tn=512,
            chunk_rows=512, chunk_cols=256):
    m, d = h.shape
    nj = D_FF // tn
    return pl.pallas_call(
        functools.partial(_ffn_up_kernel, tiles_per_seq=seq_len // tm, chunk_rows=chunk_rows,
                          chunk_cols=chunk_cols),
        grid=(nj, m // tm),
        in_specs=[
            pl.BlockSpec((tm, d), lambda j, i: (i, 0)),
            pl.BlockSpec((None, d, tn), lambda j, i: (layer, 0, j)),
            pl.BlockSpec((None, d, tn), lambda j, i: (layer, 0, nj + j)),
            pl.BlockSpec((None, CONV_WIDTH, tn), lambda j, i: (layer, 0, j)),
            pl.BlockSpec((None, CONV_WIDTH, tn), lambda j, i: (layer, 0, nj + j)),
            pl.BlockSpec((1, 1, tn), lambda j, i: (layer, 0, j)),
            pl.BlockSpec((1, 1, tn), lambda j, i: (layer, 0, nj + j)),
        ],
        out_specs=pl.BlockSpec((tm, tn), lambda j, i: (i, j)),
        out_shape=jax.ShapeDtypeStruct((m, D_FF), BF16),
        scratch_shapes=[
            pltpu.VMEM((d, tn), BF16),
            pltpu.VMEM((d, tn), BF16),
            pltpu.VMEM((HISTORY_ROWS, tn), F32),
            pltpu.VMEM((HISTORY_ROWS, tn), F32),
        ] + [pltpu.VMEM((HISTORY_ROWS + chunk_rows, chunk_cols), F32)] * 4,
        compiler_params=_params("arbitrary", "arbitrary"),
        name="ffn_up_conv_gate",
    )(h, w_up_all, w_up_all, conv_w_all, conv_w_all, conv_b_all, conv_b_all)


def _head_norm_rope_t(x, gain, cos_t, sin_t):
    if gain is not None:
        ms = jnp.mean(x * x, axis=0, keepdims=True)
        x = (x * lax.rsqrt(ms + NORM_EPS)) * gain
    return x * cos_t + pltpu.roll(x, HEAD_DIM // 2, 0) * sin_t


DSA_K_ROW = N_HEADS * HEAD_DIM
DSA_V_ROW = DSA_K_ROW + DSA_KV_HEADS * HEAD_DIM
DSA_QI_ROW = DSA_V_ROW + DSA_KV_HEADS * HEAD_DIM
DSA_MAIN_ROWS = DSA_QI_ROW + IDX_HEADS * IDX_DIM


def _dsa_in_kernel(h_ref, w_ref, cos_ref, sin_ref, qg_ref, kg_ref, o_ref, w_bf, *, tn):
    j = pl.program_id(0)

    @pl.when(pl.program_id(1) == 0)
    def _():
        w_bf[...] = w_ref[...].astype(BF16)

    def head_kind(row):
        return "q" if row < DSA_K_ROW else "k" if row < DSA_V_ROW else "v" if row < DSA_QI_ROW else "qi"

    def finish(block):
        out = lax.dot_general(w_bf[...], h_ref[...], (((1,), (1,)), ((), ())), preferred_element_type=F32)
        cos, sin = cos_ref[...], sin_ref[...]
        for n in range(tn // HEAD_DIM):
            sl = slice(n * HEAD_DIM, (n + 1) * HEAD_DIM)
            kind = head_kind(block * tn + n * HEAD_DIM)
            x = out[sl, :]
            if kind == "q":
                x = _head_norm_rope_t(x, qg_ref[...] * DSA_Q_SCALE, cos, sin)
            elif kind == "k":
                x = _head_norm_rope_t(x, kg_ref[...], cos, sin)
            elif kind == "qi":
                x = _head_norm_rope_t(x, None, cos, sin)
            o_ref[sl, :] = x.astype(BF16)

    for block in range(DSA_MAIN_ROWS // tn):
        pl.when(j == block)(functools.partial(finish, block))


def _dsa_in_proj(h, w_in_t, cos_t, sin_t, q_gain, k_gain, seq_len, *, tm=1024, tn=1024):
    m, d = h.shape
    tiles_per_seq = seq_len // tm
    gain_spec = pl.BlockSpec((HEAD_DIM, 1), lambda j, i: (0, 0))
    tab_spec = pl.BlockSpec((HEAD_DIM, tm), lambda j, i: (0, i % tiles_per_seq))
    return pl.pallas_call(
        functools.partial(_dsa_in_kernel, tn=tn),
        grid=(DSA_MAIN_ROWS // tn, m // tm),
        in_specs=[
            pl.BlockSpec((tm, d), lambda j, i: (i, 0)),
            pl.BlockSpec((tn, d), lambda j, i: (j, 0)),
            tab_spec, tab_spec, gain_spec, gain_spec,
        ],
        out_specs=pl.BlockSpec((tn, tm), lambda j, i: (j, i)),
        out_shape=jax.ShapeDtypeStruct((DSA_MAIN_ROWS, m), BF16),
        scratch_shapes=[pltpu.VMEM((tn, d), BF16)],
        compiler_params=_params("arbitrary", "arbitrary"),
        name="dsa_in_proj",
    )(h, w_in_t, cos_t, sin_t, q_gain, k_gain)


def _dsa_tail_kernel(h_ref, w_ref, o_ref):
    o_ref[...] = lax.dot_general(w_ref[...].astype(BF16), h_ref[...], (((1,), (1,)), ((), ())),
                                 preferred_element_type=F32)


def _dsa_in_proj_tail(h, w_tail_t, *, tm=2048):
    m, d = h.shape
    rows = w_tail_t.shape[0]
    return pl.pallas_call(
        _dsa_tail_kernel,
        grid=(m // tm,),
        in_specs=[pl.BlockSpec((tm, d), lambda i: (i, 0)), pl.BlockSpec((rows, d), lambda i: (0, 0))],
        out_specs=pl.BlockSpec((rows, tm), lambda i: (0, i)),
        out_shape=jax.ShapeDtypeStruct((rows, m), F32),
        compiler_params=_params("arbitrary"),
        name="dsa_in_proj_tail",
    )(h, w_tail_t)


def _dsa_prep_kernel(k_ref, v_ref, t_ref, cos_ref, sin_ref, ig_ref, ko_ref, vt_ref, ki_ref):
    ko_ref[...] = k_ref[...].astype(F32).T.astype(BF16)
    vt_ref[...] = v_ref[...]
    ki_t = _head_norm_rope_t(t_ref[0:IDX_DIM, :], ig_ref[...], cos_ref[...], sin_ref[...])
    ki_ref[...] = ki_t.T.astype(BF16)


def _dsa_prep(proc_t, tail_t, cos_t, sin_t, ik_gain, batch, seq_len, *, tm=512, tkc=512):
    m = proc_t.shape[1]
    tiles_per_seq = seq_len // tm
    tiles_per_chunk = tkc // tm
    hk = DSA_KV_HEADS * HEAD_DIM

    def v_block(i):
        pos_tile = i % tiles_per_seq
        return i // tiles_per_seq, pos_tile // tiles_per_chunk, 0, pos_tile % tiles_per_chunk

    tab_spec = pl.BlockSpec((HEAD_DIM, tm), lambda i: (0, i % tiles_per_seq))
    return pl.pallas_call(
        _dsa_prep_kernel,
        grid=(m // tm,),
        in_specs=[
            pl.BlockSpec((hk, tm), lambda i: (DSA_K_ROW // hk, i)),
            pl.BlockSpec((hk, tm), lambda i: (DSA_V_ROW // hk, i)),
            pl.BlockSpec((tail_t.shape[0], tm), lambda i: (0, i)),
            tab_spec, tab_spec,
            pl.BlockSpec((HEAD_DIM, 1), lambda i: (0, 0)),
        ],
        out_specs=[
            pl.BlockSpec((tm, hk), lambda i: (i, 0)),
            pl.BlockSpec((None, None, hk, tm), v_block),
            pl.BlockSpec((tm, IDX_DIM), lambda i: (i, 0)),
        ],
        out_shape=[
            jax.ShapeDtypeStruct((m, hk), BF16),
            jax.ShapeDtypeStruct((batch, seq_len // tkc, hk, tkc), BF16),
            jax.ShapeDtypeStruct((m, IDX_DIM), BF16),
        ],
        compiler_params=_params("arbitrary"),
        name="dsa_kv_layout_kidx",
    )(proc_t, proc_t, tail_t, cos_t, sin_t, ik_gain)


def _key_to_f32(key):
    bits = jnp.where(key < 0, key ^ INT32_MIN, ~key)
    return lax.bitcast_convert_type(bits, F32)


def _key16_to_bf16(key):
    bits = jnp.where(key < 0, key ^ INT32_MIN, ~key) & -65536
    return lax.bitcast_convert_type(bits, F32).astype(BF16)


def _indexer_kernel(qi_lo_ref, qi_hi_ref, ki_ref, w_ref, bias_ref, sc_ref, hi_ref, *,
                    tq, tkc, tkb, nkc, topk):
    i = pl.program_id(1)
    t0 = i * tq
    w = w_ref[...] * (1.0 / math.sqrt(IDX_DIM) / math.sqrt(IDX_HEADS))
    key_pos = lax.broadcasted_iota(jnp.int32, (tkc, tq), 0)
    qry_pos = lax.broadcasted_iota(jnp.int32, (tkc, tq), 1) + t0

    def chunk_in_range(c):
        return c * tkc < t0 + tq

    for c in range(nkc):

        @pl.when(chunk_in_range(c))
        def _():
            k_blk = ki_ref[c * tkc:(c + 1) * tkc, :]
            acc = jnp.zeros((tkc, tq), F32)
            for h in range(IDX_HEADS):
                qi_ref, hh = (qi_lo_ref, h) if h < IDX_HEADS // 2 else (qi_hi_ref, h - IDX_HEADS // 2)
                logit = jnp.dot(k_blk, qi_ref[hh * IDX_DIM:(hh + 1) * IDX_DIM, :],
                                preferred_element_type=F32)
                acc = acc + jnp.maximum(logit, 0.0) * w[h:h + 1, :]
            score = jnp.where(key_pos + c * tkc <= qry_pos, acc, -jnp.inf)
            sc_ref[c] = score
            hi_ref[c] = score.astype(BF16)

        @pl.when(jnp.logical_not(chunk_in_range(c)))
        def _():
            sc_ref[c] = jnp.full((tkc, tq), -jnp.inf, F32)
            hi_ref[c] = jnp.full((tkc, tq), -jnp.inf, BF16)

    def sublane_partial(hit):
        return jnp.sum(jnp.sum(hit.reshape(8, tkc // 64, 8, tq), axis=1), axis=0)

    n_valid = (t0 + tq + tkc - 1) // tkc

    def count_ge(thr):
        def add_chunk(c, acc):
            return acc + sublane_partial(jnp.where(sc_ref[c] >= thr, 1.0, 0.0))

        acc = lax.fori_loop(0, n_valid, add_chunk, jnp.zeros((8, tq), F32))
        return jnp.sum(acc, axis=0, keepdims=True)

    def count_hi_ge(thr_hi):
        one, zero = jnp.ones((), BF16), jnp.zeros((), BF16)

        def add_chunk(c, acc):
            hit = jnp.where(hi_ref[c] >= thr_hi, one, zero)
            parts = [hit[r:r + 16] for r in range(0, tkc, 16)]
            while len(parts) > 1:
                parts = [a + b for a, b in zip(parts[0::2], parts[1::2])]
            return acc + parts[0]

        acc = lax.fori_loop(0, n_valid, add_chunk, jnp.zeros((16, tq), BF16))
        return jnp.sum(acc.astype(F32), axis=0, keepdims=True)

    def store_bias(c, bias):
        for part in range(tkc // tkb):
            bias_ref[0, 0, c * (tkc // tkb) + part] = bias[part * tkb:(part + 1) * tkb]

    def write_bias(select_fn):
        for c in range(nkc):
            store_bias(c, jnp.where(select_fn(sc_ref[c]), 0.0, MASK_BIAS))

    @pl.when(t0 + tq <= topk)
    def _():
        write_bias(lambda sc: sc >= F32_LOWEST)

    @pl.when(t0 + tq > topk)
    def _():
        def coarse_step(step, key):
            trial = key | lax.shift_left(jnp.int32(1), 31 - step)
            keep = count_hi_ge(_key16_to_bf16(trial)) >= topk
            return jnp.where(keep, trial, key)

        coarse = lax.fori_loop(0, 16, coarse_step, jnp.zeros((1, tq), jnp.int32))
        coarse_key = coarse | jnp.where(coarse < 0, 0, 0xFFFF)
        base = coarse_key - 0x8000

        def fine_step(step, offset):
            trial = offset | lax.shift_left(jnp.int32(1), 16 - step)
            keep = count_ge(_key_to_f32(base + trial)) >= topk
            return jnp.where(keep, trial, offset)

        key = base + lax.fori_loop(0, 17, fine_step, jnp.zeros((1, tq), jnp.int32))
        few = lax.broadcasted_iota(jnp.int32, (1, tq), 1) + t0 < topk - 1
        thr = jnp.where(few, F32_LOWEST, _key_to_f32(key))
        write_bias(lambda sc: sc >= thr)

        n_ge = count_ge(thr)

        @pl.when(jnp.max(jnp.where(few, 0.0, n_ge)) > topk)
        def _():
            n_gt = sum(jnp.sum(jnp.where(sc_ref[c] > thr, 1.0, 0.0), axis=0, keepdims=True)
                       for c in range(nkc))
            room = jnp.where(few, float(tkc * nkc), topk - n_gt)
            r2 = lax.broadcasted_iota(jnp.int32, (tkc, tkc), 0)
            c2 = lax.broadcasted_iota(jnp.int32, (tkc, tkc), 1)
            before = jnp.where(c2 < r2, 1.0, 0.0).astype(BF16)
            seen = jnp.zeros((1, tq), F32)
            for c in range(nkc):
                sc = sc_ref[c]
                tied = jnp.where(sc == thr, 1.0, 0.0)
                rank = seen + jnp.dot(before, tied.astype(BF16), preferred_element_type=F32)
                take = jnp.where(sc > thr, 1.0, jnp.where(rank < room, tied, 0.0))
                store_bias(c, jnp.where(take > 0.0, 0.0, MASK_BIAS))
                seen = seen + jnp.sum(tied, axis=0, keepdims=True)


def _dsa_indexer(proc, ki, w_t, batch, seq_len, topk, *, tq, tkb, tkc=512):
    half = IDX_HEADS * IDX_DIM // 2
    qi_block = DSA_QI_ROW // half
    nq = seq_len // tq
    nkc = seq_len // tkc
    nkb = seq_len // tkb
    return pl.pallas_call(
        functools.partial(_indexer_kernel, tq=tq, tkc=tkc, tkb=tkb, nkc=nkc, topk=topk),
        grid=(batch, nq),
        in_specs=[
            pl.BlockSpec((half, tq), lambda b, i: (qi_block, b * nq + i)),
            pl.BlockSpec((half, tq), lambda b, i: (qi_block + 1, b * nq + i)),
            pl.BlockSpec((seq_len, IDX_DIM), lambda b, i: (b, 0)),
            pl.BlockSpec((IDX_HEADS, tq), lambda b, i: (0, b * nq + i)),
        ],
        out_specs=pl.BlockSpec((1, 1, nkb, tkb, tq), lambda b, i: (b, i, 0, 0, 0)),
        out_shape=jax.ShapeDtypeStruct((batch, nq, nkb, tkb, tq), F32),
        scratch_shapes=[pltpu.VMEM((nkc, tkc, tq), F32), pltpu.VMEM((nkc, tkc, tq), BF16)],
        compiler_params=_params("arbitrary", "arbitrary"),
        name="dsa_indexer_topk",
    )(proc, proc, ki, w_t)


def _dsa_attn_kernel(q_ref, k_ref, vt_ref, bias_ref, o_ref, *, tq, tkb, groups, chain_heads):
    i = pl.program_id(1)
    n_chunks = ((i + 1) * tq + tkb - 1) // tkb
    cols = chain_heads * tq
    chains = [(g, r0) for g in range(groups) for r0 in range(0, DSA_GROUP, chain_heads)]

    def head_lanes(g, r):
        h = g * DSA_GROUP + r
        return slice(h * HEAD_DIM, (h + 1) * HEAD_DIM)

    qs = [jnp.concatenate([q_ref[head_lanes(g, r0 + r), :] for r in range(chain_heads)], axis=1)
          for g, r0 in chains]

    def over_keys(x, op):
        return op(op(op(x.reshape(8, tkb // 64, 8, cols), axis=1), axis=0), axis=0, keepdims=True)

    def kv_lanes(n):
        g, _ = chains[n]
        return slice(g * HEAD_DIM, (g + 1) * HEAD_DIM)

    def scores(n, c):
        k_blk = k_ref[pl.ds(pl.multiple_of(c * tkb, tkb), tkb), kv_lanes(n)]
        return jnp.dot(k_blk, qs[n], preferred_element_type=F32)

    def softmax_step(n, c, state, raw, bias):
        m_run, l_run, acc = state
        s = bias + raw
        m_new = jnp.maximum(m_run, over_keys(s, jnp.max))
        alpha = jnp.exp2(m_run - m_new)
        p = jnp.exp2(s - m_new)
        l_run = alpha * l_run + over_keys(p, jnp.sum)
        acc = alpha * acc + jnp.dot(vt_ref[0, c, kv_lanes(n), :], p.astype(BF16),
                                    preferred_element_type=F32)
        return m_new, l_run, acc

    def body(c, states):
        bias = jnp.concatenate([bias_ref[0, 0, c]] * chain_heads, axis=1)
        raws = [scores(n, c) for n in range(len(chains))]
        return [softmax_step(n, c, states[n], raws[n], bias) for n in range(len(chains))]

    init = (jnp.full((1, cols), -jnp.inf, F32), jnp.zeros((1, cols), F32),
            jnp.zeros((HEAD_DIM, cols), F32))
    states = lax.fori_loop(0, n_chunks, body, [init] * len(chains))
    for (g, r0), (_, l_run, acc) in zip(chains, states):
        out_t = acc / l_run
        for r in range(chain_heads):
            o_ref[:, head_lanes(g, r0 + r)] = out_t[:, r * tq:(r + 1) * tq].T.astype(o_ref.dtype)


def _dsa_attention(proc, k, v_t, bias, batch, seq_len, *, tq, tkb, groups=2, chain_heads=2):
    nq = seq_len // tq
    gw = groups * DSA_GROUP * HEAD_DIM
    kw = groups * HEAD_DIM
    nkb = seq_len // tkb
    return pl.pallas_call(
        functools.partial(_dsa_attn_kernel, tq=tq, tkb=tkb, groups=groups, chain_heads=chain_heads),
        grid=(batch, nq, DSA_KV_HEADS // groups),
        in_specs=[
            pl.BlockSpec((gw, tq), lambda b, i, g: (g, b * nq + i)),
            pl.BlockSpec((seq_len, kw), lambda b, i, g: (b, g)),
            pl.BlockSpec((1, nkb, kw, tkb), lambda b, i, g: (b, 0, g, 0)),
            pl.BlockSpec((1, 1, nkb, tkb, tq), lambda b, i, g: (b, i, 0, 0, 0)),
        ],
        out_specs=pl.BlockSpec((tq, gw), lambda b, i, g: (b * nq + i, g)),
        out_shape=jax.ShapeDtypeStruct((batch * seq_len, N_HEADS * HEAD_DIM), BF16),
        compiler_params=_params("arbitrary", "arbitrary", "arbitrary"),
        name="dsa_attention",
    )(proc, k, v_t, bias)


def _rope_tables_t(seq_len, dim):
    inv_freq = 1.0 / (ROPE_THETA ** (jnp.arange(0, dim, 2, dtype=F32) / dim))
    ang = jnp.arange(seq_len, dtype=F32)[:, None] * inv_freq[None, :]
    cos, sin = jnp.cos(ang).T, jnp.sin(ang).T
    return jnp.concatenate([cos, cos], axis=0), jnp.concatenate([-sin, sin], axis=0)


def _conv_ffn(x, h, w_up, conv_w, conv_b, w_down, layer, seq_len):
    gated = _ffn_up(h, w_up, conv_w, conv_b, layer, seq_len)
    return _matmul(gated, w_down, layer, x.shape[1], tm=512, tn=1024, out_dtype=F32, res=x,
                   name="ffn_down_residual")


def kernel(x, attn_norm_g, ffn_norm_g, sb_w_qkv, sb_w_o, dsa_w_in, dsa_q_norm_g, dsa_k_norm_g,
           dsa_ik_norm_g, dsa_w_o, ffn_w_up, ffn_conv_w, ffn_conv_b, ffn_w_down):
    batch, seq_len, d_model = x.shape
    m = batch * seq_len
    hd = N_HEADS * HEAD_DIM
    x = x.reshape(m, d_model)
    attn_g = attn_norm_g[:, None, :]
    ffn_g = ffn_norm_g[:, None, :]
    conv_b = ffn_conv_b[:, None, :]
    w_down = _to_bf16(ffn_w_down)

    h = _rmsnorm(x, attn_g, 0)
    qkv = _matmul(h, sb_w_qkv, 0, 3 * hd, tm=2048, tn=1024, out_dtype=BF16, scaled_cols=hd,
                  col_scale=1.0 / math.sqrt(HEAD_DIM), name="sb_qkv_proj")
    mixed = _sb_attention(qkv.reshape(batch, seq_len, 3 * hd)).reshape(m, hd)
    x, h = _out_proj_norm(mixed, sb_w_o, x, ffn_g, 0, name="sb_out_residual_norm")
    x = _conv_ffn(x, h, ffn_w_up, ffn_conv_w, conv_b, w_down, 0, seq_len)

    h = _rmsnorm(x, attn_g, 1)
    w_in_t = jnp.swapaxes(dsa_w_in[0], 0, 1)
    cos_t, sin_t = _rope_tables_t(seq_len, HEAD_DIM)
    proc = _dsa_in_proj(h, w_in_t, cos_t, sin_t, dsa_q_norm_g[0][:, None], dsa_k_norm_g[0][:, None], seq_len)
    tail_t = _dsa_in_proj_tail(h, w_in_t[DSA_MAIN_ROWS:])
    tq, tkb = 256, 512
    k, v_t, ki = _dsa_prep(proc, tail_t, cos_t, sin_t, dsa_ik_norm_g[0][:, None], batch, seq_len, tkc=tkb)
    topk = min(TOPK_MAX, seq_len // 4)
    w_t = tail_t[IDX_DIM:]
    bias = _dsa_indexer(proc, ki, w_t, batch, seq_len, topk, tq=tq, tkb=tkb)
    mixed = _dsa_attention(proc, k, v_t, bias, batch, seq_len, tq=tq, tkb=tkb)
    x, h = _out_proj_norm(mixed, dsa_w_o, x, ffn_g, 1, name="dsa_out_residual_norm")
    x = _conv_ffn(x, h, ffn_w_up, ffn_conv_w, conv_b, w_down, 1, seq_len)
    return x.reshape(batch, seq_len, d_model)
```

```python
import functools
import math

import jax
import jax.numpy as jnp
from jax import lax
from jax.experimental import pallas as pl
from jax.experimental.pallas import tpu as pltpu

N_HEADS = 16
HEAD_DIM = 128
DSA_KV_HEADS = 4
DSA_GROUP = N_HEADS // DSA_KV_HEADS
IDX_HEADS = 16
IDX_DIM = 128
TOPK_MAX = 256
D_FF = 5632
CONV_WIDTH = 3
ROPE_THETA = 10000.0
NORM_EPS = 1e-6

V7X_VMEM_BYTES = 64 * 1024 * 1024
VMEM_LIMIT_BYTES = V7X_VMEM_BYTES - 8 * 1024 * 1024

F32 = jnp.float32
BF16 = jnp.bfloat16
MASK_BIAS = -1e30
F32_LOWEST = float(jnp.finfo(jnp.float32).min)
INT32_MIN = -(2 ** 31)
LOG2E = math.log2(math.e)
DSA_Q_SCALE = LOG2E / math.sqrt(HEAD_DIM)


def _params(*sem):
    return pltpu.CompilerParams(dimension_semantics=sem, vmem_limit_bytes=VMEM_LIMIT_BYTES)


def _rmsnorm_kernel(x_ref, g_ref, o_ref):
    x = x_ref[...]
    ms = jnp.mean(x * x, axis=-1, keepdims=True)
    o_ref[...] = ((x * lax.rsqrt(ms + NORM_EPS)) * g_ref[0]).astype(o_ref.dtype)


def _rmsnorm(x, g_all, layer, *, tm=1024):
    m, d = x.shape
    return pl.pallas_call(
        _rmsnorm_kernel,
        grid=(m // tm,),
        in_specs=[
            pl.BlockSpec((tm, d), lambda i: (i, 0)),
            pl.BlockSpec((1, 1, d), lambda i: (layer, 0, 0)),
        ],
        out_specs=pl.BlockSpec((tm, d), lambda i: (i, 0)),
        out_shape=jax.ShapeDtypeStruct((m, d), BF16),
        compiler_params=_params("arbitrary"),
        name="rmsnorm",
    )(x, g_all)


def _matmul_kernel(*refs, has_res, cast_w, scaled_blocks, block_scale):
    a_ref, w_ref, *rest = refs
    r_ref = rest.pop(0) if has_res else None
    o_ref = rest.pop(0)
    if cast_w:
        (w_bf,) = rest

        @pl.when(pl.program_id(1) == 0)
        def _():
            w_bf[...] = w_ref[...].astype(BF16)

        w = w_bf[...]
    else:
        w = w_ref[...]
    out = jnp.dot(a_ref[...], w, preferred_element_type=F32)
    if scaled_blocks:
        out = out * jnp.where(pl.program_id(0) < scaled_blocks, block_scale, 1.0)
    if has_res:
        out = r_ref[...] + out
    o_ref[...] = out.astype(o_ref.dtype)


def _matmul(a, w_all, layer, n_out, *, tm, tn, out_dtype, res=None, scaled_cols=0, col_scale=1.0, name):
    m, k_dim = a.shape
    cast_w = w_all.dtype != BF16
    assert scaled_cols % tn == 0
    in_specs = [
        pl.BlockSpec((tm, k_dim), lambda j, i: (i, 0)),
        pl.BlockSpec((None, k_dim, tn), lambda j, i: (layer, 0, j)),
    ]
    args = [a, w_all]
    if res is not None:
        in_specs.append(pl.BlockSpec((tm, tn), lambda j, i: (i, j)))
        args.append(res)
    return pl.pallas_call(
        functools.partial(_matmul_kernel, has_res=res is not None, cast_w=cast_w,
                          scaled_blocks=scaled_cols // tn, block_scale=col_scale),
        grid=(n_out // tn, m // tm),
        in_specs=in_specs,
        out_specs=pl.BlockSpec((tm, tn), lambda j, i: (i, j)),
        out_shape=jax.ShapeDtypeStruct((m, n_out), out_dtype),
        scratch_shapes=[pltpu.VMEM((k_dim, tn), BF16)] if cast_w else [],
        compiler_params=_params("arbitrary", "arbitrary"),
        name=name,
    )(*args)


def _out_proj_norm_kernel(a_ref, w_ref, r_ref, g_ref, x_ref, h_ref, w_bf):
    @pl.when(pl.program_id(0) == 0)
    def _():
        w_bf[...] = w_ref[...].astype(BF16)

    x = r_ref[...] + jnp.dot(a_ref[...], w_bf[...], preferred_element_type=F32)
    x_ref[...] = x
    ms = jnp.mean(x * x, axis=-1, keepdims=True)
    h_ref[...] = ((x * lax.rsqrt(ms + NORM_EPS)) * g_ref[0]).astype(h_ref.dtype)


def _out_proj_norm(a, w_all, res, g_all, layer, *, tm=512, name):
    m, k_dim = a.shape
    d = w_all.shape[2]
    return pl.pallas_call(
        _out_proj_norm_kernel,
        grid=(m // tm,),
        in_specs=[
            pl.BlockSpec((tm, k_dim), lambda i: (i, 0)),
            pl.BlockSpec((None, k_dim, d), lambda i: (0, 0, 0), pipeline_mode=pl.Buffered(1)),
            pl.BlockSpec((tm, d), lambda i: (i, 0)),
            pl.BlockSpec((1, 1, d), lambda i: (layer, 0, 0)),
        ],
        out_specs=[pl.BlockSpec((tm, d), lambda i: (i, 0)), pl.BlockSpec((tm, d), lambda i: (i, 0))],
        out_shape=[jax.ShapeDtypeStruct((m, d), F32), jax.ShapeDtypeStruct((m, d), BF16)],
        scratch_shapes=[pltpu.VMEM((k_dim, d), BF16)],
        compiler_params=_params("arbitrary"),
        name=name,
    )(a, w_all, res, g_all)


def _sb_tile(q, k_blk, v_blk, upper, carry, causal):
    z = lax.dot_general(q, k_blk, (((1,), (1,)), ((), ())), preferred_element_type=F32)
    sp = jnp.maximum(z, 0.0) + jnp.log(1.0 + jnp.exp2(jnp.abs(z) * (-LOG2E)))
    if causal is not None:
        sp = jnp.where(causal, sp, 0.0)
    tail = jnp.dot(sp.astype(BF16), upper, preferred_element_type=F32)
    a = jnp.exp(z - (tail + carry))
    if causal is not None:
        a = jnp.where(causal, a, 0.0)
    out = jnp.dot(a.astype(BF16), v_blk, preferred_element_type=F32)
    return out, carry + jnp.sum(sp, axis=1, keepdims=True)


def _sb_attn_kernel(q_ref, k_ref, v_ref, o_ref, *, tq, tk, heads):
    i = pl.program_id(2)
    per_q = tq // tk
    r2 = lax.broadcasted_iota(jnp.int32, (tk, tk), 0)
    c2 = lax.broadcasted_iota(jnp.int32, (tk, tk), 1)
    upper = jnp.where(r2 >= c2, 1.0, 0.0).astype(BF16)
    rows = lax.broadcasted_iota(jnp.int32, (tq, tk), 0)
    cols = lax.broadcasted_iota(jnp.int32, (tq, tk), 1)
    lanes = [slice(h * HEAD_DIM, (h + 1) * HEAD_DIM) for h in range(heads)]

    def block(j, carries, causal, first_row=0):
        start = pl.multiple_of(j * tk, tk)
        outs = [_sb_tile(q_ref[0, first_row:, lanes[h]], k_ref[0, pl.ds(start, tk), lanes[h]],
                         v_ref[0, pl.ds(start, tk), lanes[h]], upper, carries[h], causal)
                for h in range(heads)]
        return [o for o, _ in outs], [c for _, c in outs]

    carries = [jnp.zeros((tq, 1), F32)] * heads
    accs = [jnp.zeros((tq, HEAD_DIM), F32)] * heads
    for d in reversed(range(per_q)):
        r0 = d * tk
        outs, low = block(i * per_q + d, [c[r0:] for c in carries], (cols + r0 < rows)[r0:], r0)
        if r0:
            carries = [jnp.concatenate([c[:r0], lo], axis=0) for c, lo in zip(carries, low)]
            accs = [jnp.concatenate([acc[:r0], acc[r0:] + o], axis=0) for acc, o in zip(accs, outs)]
        else:
            carries = low
            accs = [acc + o for acc, o in zip(accs, outs)]

    def body(step, state):
        accs, carries = state
        for d in range(per_q):
            outs, carries = block((i - step) * per_q - 1 - d, carries, None)
            accs = [acc + o for acc, o in zip(accs, outs)]
        return accs, carries

    accs, _ = lax.fori_loop(0, i, body, (accs, carries))
    for h in range(heads):
        o_ref[0, :, lanes[h]] = accs[h].astype(o_ref.dtype)


def _sb_attention(qkv, *, tq=512, tk=256, heads=4):
    b, s, _ = qkv.shape
    groups = N_HEADS // heads
    width = heads * HEAD_DIM
    return pl.pallas_call(
        functools.partial(_sb_attn_kernel, tq=tq, tk=tk, heads=heads),
        grid=(b, groups, s // tq),
        in_specs=[
            pl.BlockSpec((1, tq, width), lambda bi, g, i: (bi, i, g)),
            pl.BlockSpec((1, s, width), lambda bi, g, i: (bi, 0, groups + g)),
            pl.BlockSpec((1, s, width), lambda bi, g, i: (bi, 0, 2 * groups + g)),
        ],
        out_specs=pl.BlockSpec((1, tq, width), lambda bi, g, i: (bi, i, g)),
        out_shape=jax.ShapeDtypeStruct((b, s, N_HEADS * HEAD_DIM), BF16),
        compiler_params=_params("arbitrary", "arbitrary", "arbitrary"),
        name="sb_attention",
    )(qkv, qkv, qkv)


HISTORY_ROWS = 8


def _causal_conv(u_ref, cw, cb):
    r0 = HISTORY_ROWS
    rows = u_ref.shape[0] - r0
    return (cb + cw[2:3] * u_ref[r0:r0 + rows] + cw[1:2] * u_ref[r0 - 1:r0 - 1 + rows]
            + cw[0:1] * u_ref[r0 - 2:r0 - 2 + rows])


def _ffn_up_kernel(h_ref, wg_ref, wv_ref, cwg_ref, cwv_ref, cbg_ref, cbv_ref, wd_ref, o_ref, wd_bf_ref,
                   wg_bf, wv_bf, hist_g, hist_v, *stage, tiles_per_seq, chunk_rows, chunk_cols):
    i = pl.program_id(1)
    tm, tn = o_ref.shape
    r0 = HISTORY_ROWS

    wd_bf_ref[...] = wd_ref[...].astype(BF16)

    @pl.when(i == 0)
    def _():
        wg_bf[...] = wg_ref[...].astype(BF16)
        wv_bf[...] = wv_ref[...].astype(BF16)

    @pl.when(i % tiles_per_seq == 0)
    def _():
        hist_g[...] = jnp.zeros_like(hist_g)
        hist_v[...] = jnp.zeros_like(hist_v)

    piece = 0
    for c0 in range(0, tn, chunk_cols):
        cols = slice(c0, c0 + chunk_cols)
        prev_g = hist_g[:, cols]
        prev_v = hist_v[:, cols]
        for t0 in range(0, tm, chunk_rows):
            ug_s, uv_s = stage[2 * (piece % 2)], stage[2 * (piece % 2) + 1]
            piece += 1
            h = h_ref[t0:t0 + chunk_rows, :]
            ug_s[0:r0] = prev_g
            uv_s[0:r0] = prev_v
            ug_s[r0:] = jnp.dot(h, wg_bf[:, cols], preferred_element_type=F32)
            uv_s[r0:] = jnp.dot(h, wv_bf[:, cols], preferred_element_type=F32)
            gate = _causal_conv(ug_s, cwg_ref[:, cols], cbg_ref[0, :, cols])
            val = _causal_conv(uv_s, cwv_ref[:, cols], cbv_ref[0, :, cols])
            o_ref[t0:t0 + chunk_rows, cols] = (jax.nn.silu(gate) * val).astype(o_ref.dtype)
            prev_g = ug_s[chunk_rows:]
            prev_v = uv_s[chunk_rows:]
        hist_g[:, cols] = prev_g
        hist_v[:, cols] = prev_v


def _ffn_up(h, w_up_all, conv_w_all, conv_b_all, w_down_all, layer, seq_len, *, tm=2048, tn=512,
            chunk_rows=512, chunk_cols=256):
    m, d = h.shape
    nj = D_FF // tn
    ni = m // tm
    slab, rem = divmod(w_down_all.shape[1], nj * ni)
    assert rem == 0 and slab % 16 == 0
    d_out = w_down_all.shape[2]
    return pl.pallas_call(
        functools.partial(_ffn_up_kernel, tiles_per_seq=seq_len // tm, chunk_rows=chunk_rows,
                          chunk_cols=chunk_cols),
        grid=(nj, m // tm),
        in_specs=[
            pl.BlockSpec((tm, d), lambda j, i: (i, 0)),
            pl.BlockSpec((None, d, tn), lambda j, i: (layer, 0, j)),
            pl.BlockSpec((None, d, tn), lambda j, i: (layer, 0, nj + j)),
            pl.BlockSpec((None, CONV_WIDTH, tn), lambda j, i: (layer, 0, j)),
            pl.BlockSpec((None, CONV_WIDTH, tn), lambda j, i: (layer, 0, nj + j)),
            pl.BlockSpec((1, 1, tn), lambda j, i: (layer, 0, j)),
            pl.BlockSpec((1, 1, tn), lambda j, i: (layer, 0, nj + j)),
            pl.BlockSpec((None, slab, d_out), lambda j, i: (layer, j * ni + i, 0)),
        ],
        out_specs=[
            pl.BlockSpec((tm, tn), lambda j, i: (i, j)),
            pl.BlockSpec((None, slab, d_out), lambda j, i: (0, j * ni + i, 0)),
        ],
        out_shape=[
            jax.ShapeDtypeStruct((m, D_FF), BF16),
            jax.ShapeDtypeStruct((1,) + w_down_all.shape[1:], BF16),
        ],
        scratch_shapes=[
            pltpu.VMEM((d, tn), BF16),
            pltpu.VMEM((d, tn), BF16),
            pltpu.VMEM((HISTORY_ROWS, tn), F32),
            pltpu.VMEM((HISTORY_ROWS, tn), F32),
        ] + [pltpu.VMEM((HISTORY_ROWS + chunk_rows, chunk_cols), F32)] * 4,
        compiler_params=_params("arbitrary", "arbitrary"),
        name="ffn_up_conv_gate",
    )(h, w_up_all, w_up_all, conv_w_all, conv_w_all, conv_b_all, conv_b_all, w_down_all)


def _head_norm_rope_t(x, gain, cos_t, sin_t):
    if gain is not None:
        ms = jnp.mean(x * x, axis=0, keepdims=True)
        x = (x * lax.rsqrt(ms + NORM_EPS)) * gain
    return x * cos_t + pltpu.roll(x, HEAD_DIM // 2, 0) * sin_t


DSA_K_ROW = N_HEADS * HEAD_DIM
DSA_V_ROW = DSA_K_ROW + DSA_KV_HEADS * HEAD_DIM
DSA_QI_ROW = DSA_V_ROW + DSA_KV_HEADS * HEAD_DIM
DSA_MAIN_ROWS = DSA_QI_ROW + IDX_HEADS * IDX_DIM
TRANSPOSE_COLS = 256


def _dsa_in_kernel(h_ref, w_ref, cos_ref, sin_ref, qg_ref, kg_ref, o_ref, w_bf, *, tn):
    j = pl.program_id(0)

    @pl.when(pl.program_id(1) == 0)
    def _():
        for c0 in range(0, tn, TRANSPOSE_COLS):
            w_bf[c0:c0 + TRANSPOSE_COLS, :] = w_ref[:, c0:c0 + TRANSPOSE_COLS].T.astype(BF16)

    def head_kind(row):
        return "q" if row < DSA_K_ROW else "k" if row < DSA_V_ROW else "v" if row < DSA_QI_ROW else "qi"

    def finish(block):
        out = lax.dot_general(w_bf[...], h_ref[...], (((1,), (1,)), ((), ())), preferred_element_type=F32)
        cos, sin = cos_ref[...], sin_ref[...]
        for n in range(tn // HEAD_DIM):
            sl = slice(n * HEAD_DIM, (n + 1) * HEAD_DIM)
            kind = head_kind(block * tn + n * HEAD_DIM)
            x = out[sl, :]
            if kind == "q":
                x = _head_norm_rope_t(x, qg_ref[...] * DSA_Q_SCALE, cos, sin)
            elif kind == "k":
                x = _head_norm_rope_t(x, kg_ref[...], cos, sin)
            elif kind == "qi":
                x = _head_norm_rope_t(x, None, cos, sin)
            o_ref[sl, :] = x.astype(BF16)

    for block in range(DSA_MAIN_ROWS // tn):
        pl.when(j == block)(functools.partial(finish, block))


def _dsa_in_proj(h, w_in, cos_t, sin_t, q_gain, k_gain, seq_len, *, tm=1024, tn=1024):
    m, d = h.shape
    tiles_per_seq = seq_len // tm
    gain_spec = pl.BlockSpec((HEAD_DIM, 1), lambda j, i: (0, 0))
    tab_spec = pl.BlockSpec((HEAD_DIM, tm), lambda j, i: (0, i % tiles_per_seq))
    return pl.pallas_call(
        functools.partial(_dsa_in_kernel, tn=tn),
        grid=(DSA_MAIN_ROWS // tn, m // tm),
        in_specs=[
            pl.BlockSpec((tm, d), lambda j, i: (i, 0)),
            pl.BlockSpec((d, tn), lambda j, i: (0, j)),
            tab_spec, tab_spec, gain_spec, gain_spec,
        ],
        out_specs=pl.BlockSpec((tn, tm), lambda j, i: (j, i)),
        out_shape=jax.ShapeDtypeStruct((DSA_MAIN_ROWS, m), BF16),
        scratch_shapes=[pltpu.VMEM((tn, d), BF16)],
        compiler_params=_params("arbitrary", "arbitrary"),
        name="dsa_in_proj",
    )(h, w_in, cos_t, sin_t, q_gain, k_gain)


def _dsa_tail_kernel(h_ref, w_ref, o_ref):
    o_ref[...] = lax.dot_general(w_ref[...].astype(BF16), h_ref[...], (((1,), (1,)), ((), ())),
                                 preferred_element_type=F32)


def _dsa_in_proj_tail(h, w_tail_t, *, tm=2048):
    m, d = h.shape
    rows = w_tail_t.shape[0]
    return pl.pallas_call(
        _dsa_tail_kernel,
        grid=(m // tm,),
        in_specs=[pl.BlockSpec((tm, d), lambda i: (i, 0)), pl.BlockSpec((rows, d), lambda i: (0, 0))],
        out_specs=pl.BlockSpec((rows, tm), lambda i: (0, i)),
        out_shape=jax.ShapeDtypeStruct((rows, m), F32),
        compiler_params=_params("arbitrary"),
        name="dsa_in_proj_tail",
    )(h, w_tail_t)


def _dsa_prep_kernel(k_ref, v_ref, t_ref, cos_ref, sin_ref, ig_ref, ko_ref, vt_ref, ki_ref):
    ko_ref[...] = k_ref[...].astype(F32).T.astype(BF16)
    vt_ref[...] = v_ref[...]
    ki_t = _head_norm_rope_t(t_ref[0:IDX_DIM, :], ig_ref[...], cos_ref[...], sin_ref[...])
    ki_ref[...] = ki_t.T.astype(BF16)


def _dsa_prep(proc_t, tail_t, cos_t, sin_t, ik_gain, batch, seq_len, *, tm=512, tkc=512):
    m = proc_t.shape[1]
    tiles_per_seq = seq_len // tm
    tiles_per_chunk = tkc // tm
    hk = DSA_KV_HEADS * HEAD_DIM

    def v_block(i):
        pos_tile = i % tiles_per_seq
        return i // tiles_per_seq, pos_tile // tiles_per_chunk, 0, pos_tile % tiles_per_chunk

    tab_spec = pl.BlockSpec((HEAD_DIM, tm), lambda i: (0, i % tiles_per_seq))
    return pl.pallas_call(
        _dsa_prep_kernel,
        grid=(m // tm,),
        in_specs=[
            pl.BlockSpec((hk, tm), lambda i: (DSA_K_ROW // hk, i)),
            pl.BlockSpec((hk, tm), lambda i: (DSA_V_ROW // hk, i)),
            pl.BlockSpec((tail_t.shape[0], tm), lambda i: (0, i)),
            tab_spec, tab_spec,
            pl.BlockSpec((HEAD_DIM, 1), lambda i: (0, 0)),
        ],
        out_specs=[
            pl.BlockSpec((tm, hk), lambda i: (i, 0)),
            pl.BlockSpec((None, None, hk, tm), v_block),
            pl.BlockSpec((tm, IDX_DIM), lambda i: (i, 0)),
        ],
        out_shape=[
            jax.ShapeDtypeStruct((m, hk), BF16),
            jax.ShapeDtypeStruct((batch, seq_len // tkc, hk, tkc), BF16),
            jax.ShapeDtypeStruct((m, IDX_DIM), BF16),
        ],
        compiler_params=_params("arbitrary"),
        name="dsa_kv_layout_kidx",
    )(proc_t, proc_t, tail_t, cos_t, sin_t, ik_gain)


def _key_to_f32(key):
    bits = jnp.where(key < 0, key ^ INT32_MIN, ~key)
    return lax.bitcast_convert_type(bits, F32)


def _key16_to_bf16(key):
    bits = jnp.where(key < 0, key ^ INT32_MIN, ~key) & -65536
    return lax.bitcast_convert_type(bits, F32).astype(BF16)


def _indexer_kernel(qi_lo_ref, qi_hi_ref, ki_ref, w_ref, bias_ref, sc_ref, hi_ref, *,
                    tq, tkc, tkb, nkc, topk):
    i = pl.program_id(1)
    t0 = i * tq
    w = w_ref[...] * (1.0 / math.sqrt(IDX_DIM) / math.sqrt(IDX_HEADS))
    key_pos = lax.broadcasted_iota(jnp.int32, (tkc, tq), 0)
    qry_pos = lax.broadcasted_iota(jnp.int32, (tkc, tq), 1) + t0

    def chunk_in_range(c):
        return c * tkc < t0 + tq

    for c in range(nkc):

        @pl.when(chunk_in_range(c))
        def _():
            k_blk = ki_ref[c * tkc:(c + 1) * tkc, :]
            acc = jnp.zeros((tkc, tq), F32)
            for h in range(IDX_HEADS):
                qi_ref, hh = (qi_lo_ref, h) if h < IDX_HEADS // 2 else (qi_hi_ref, h - IDX_HEADS // 2)
                logit = jnp.dot(k_blk, qi_ref[hh * IDX_DIM:(hh + 1) * IDX_DIM, :],
                                preferred_element_type=F32)
                acc = acc + jnp.maximum(logit, 0.0) * w[h:h + 1, :]
            score = jnp.where(key_pos + c * tkc <= qry_pos, acc, -jnp.inf)
            sc_ref[c] = score
            hi_ref[c] = score.astype(BF16)

        @pl.when(jnp.logical_not(chunk_in_range(c)))
        def _():
            sc_ref[c] = jnp.full((tkc, tq), -jnp.inf, F32)
            hi_ref[c] = jnp.full((tkc, tq), -jnp.inf, BF16)

    def sublane_partial(hit):
        return jnp.sum(jnp.sum(hit.reshape(8, tkc // 64, 8, tq), axis=1), axis=0)

    n_valid = (t0 + tq + tkc - 1) // tkc

    def count_ge(thr):
        def add_chunk(c, acc):
            return acc + sublane_partial(jnp.where(sc_ref[c] >= thr, 1.0, 0.0))

        acc = lax.fori_loop(0, n_valid, add_chunk, jnp.zeros((8, tq), F32))
        return jnp.sum(acc, axis=0, keepdims=True)

    def count_hi_ge(thr_hi):
        one, zero = jnp.ones((), BF16), jnp.zeros((), BF16)

        def add_chunk(c, acc):
            hit = jnp.where(hi_ref[c] >= thr_hi, one, zero)
            parts = [hit[r:r + 16] for r in range(0, tkc, 16)]
            while len(parts) > 1:
                parts = [a + b for a, b in zip(parts[0::2], parts[1::2])]
            return acc + parts[0]

        acc = lax.fori_loop(0, n_valid, add_chunk, jnp.zeros((16, tq), BF16))
        return jnp.sum(acc.astype(F32), axis=0, keepdims=True)

    def store_bias(c, bias):
        for part in range(tkc // tkb):
            bias_ref[0, 0, c * (tkc // tkb) + part] = bias[part * tkb:(part + 1) * tkb]

    def write_bias(select_fn):
        for c in range(nkc):
            store_bias(c, jnp.where(select_fn(sc_ref[c]), 0.0, MASK_BIAS))

    @pl.when(t0 + tq <= topk)
    def _():
        write_bias(lambda sc: sc >= F32_LOWEST)

    @pl.when(t0 + tq > topk)
    def _():
        def coarse_step(step, key):
            trial = key | lax.shift_left(jnp.int32(1), 31 - step)
            keep = count_hi_ge(_key16_to_bf16(trial)) >= topk
            return jnp.where(keep, trial, key)

        coarse = lax.fori_loop(0, 16, coarse_step, jnp.zeros((1, tq), jnp.int32))
        coarse_key = coarse | jnp.where(coarse < 0, 0, 0xFFFF)
        base = coarse_key - 0x8000

        def fine_step(step, offset):
            trial = offset | lax.shift_left(jnp.int32(1), 16 - step)
            keep = count_ge(_key_to_f32(base + trial)) >= topk
            return jnp.where(keep, trial, offset)

        key = base + lax.fori_loop(0, 17, fine_step, jnp.zeros((1, tq), jnp.int32))
        few = lax.broadcasted_iota(jnp.int32, (1, tq), 1) + t0 < topk - 1
        thr = jnp.where(few, F32_LOWEST, _key_to_f32(key))
        write_bias(lambda sc: sc >= thr)

        n_ge = count_ge(thr)

        @pl.when(jnp.max(jnp.where(few, 0.0, n_ge)) > topk)
        def _():
            n_gt = sum(jnp.sum(jnp.where(sc_ref[c] > thr, 1.0, 0.0), axis=0, keepdims=True)
                       for c in range(nkc))
            room = jnp.where(few, float(tkc * nkc), topk - n_gt)
            r2 = lax.broadcasted_iota(jnp.int32, (tkc, tkc), 0)
            c2 = lax.broadcasted_iota(jnp.int32, (tkc, tkc), 1)
            before = jnp.where(c2 < r2, 1.0, 0.0).astype(BF16)
            seen = jnp.zeros((1, tq), F32)
            for c in range(nkc):
                sc = sc_ref[c]
                tied = jnp.where(sc == thr, 1.0, 0.0)
                rank = seen + jnp.dot(before, tied.astype(BF16), preferred_element_type=F32)
                take = jnp.where(sc > thr, 1.0, jnp.where(rank < room, tied, 0.0))
                store_bias(c, jnp.where(take > 0.0, 0.0, MASK_BIAS))
                seen = seen + jnp.sum(tied, axis=0, keepdims=True)


def _dsa_indexer(proc, ki, w_t, batch, seq_len, topk, *, tq, tkb, tkc=512):
    half = IDX_HEADS * IDX_DIM // 2
    qi_block = DSA_QI_ROW // half
    nq = seq_len // tq
    nkc = seq_len // tkc
    nkb = seq_len // tkb
    return pl.pallas_call(
        functools.partial(_indexer_kernel, tq=tq, tkc=tkc, tkb=tkb, nkc=nkc, topk=topk),
        grid=(batch, nq),
        in_specs=[
            pl.BlockSpec((half, tq), lambda b, i: (qi_block, b * nq + i)),
            pl.BlockSpec((half, tq), lambda b, i: (qi_block + 1, b * nq + i)),
            pl.BlockSpec((seq_len, IDX_DIM), lambda b, i: (b, 0)),
            pl.BlockSpec((IDX_HEADS, tq), lambda b, i: (0, b * nq + i)),
        ],
        out_specs=pl.BlockSpec((1, 1, nkb, tkb, tq), lambda b, i: (b, i, 0, 0, 0)),
        out_shape=jax.ShapeDtypeStruct((batch, nq, nkb, tkb, tq), F32),
        scratch_shapes=[pltpu.VMEM((nkc, tkc, tq), F32), pltpu.VMEM((nkc, tkc, tq), BF16)],
        compiler_params=_params("arbitrary", "arbitrary"),
        name="dsa_indexer_topk",
    )(proc, proc, ki, w_t)


def _dsa_attn_kernel(q_ref, k_ref, vt_ref, bias_ref, o_ref, *, tq, tkb, groups, chain_heads):
    i = pl.program_id(1)
    n_chunks = ((i + 1) * tq + tkb - 1) // tkb
    cols = chain_heads * tq
    chains = [(g, r0) for g in range(groups) for r0 in range(0, DSA_GROUP, chain_heads)]

    def head_lanes(g, r):
        h = g * DSA_GROUP + r
        return slice(h * HEAD_DIM, (h + 1) * HEAD_DIM)

    qs = [jnp.concatenate([q_ref[head_lanes(g, r0 + r), :] for r in range(chain_heads)], axis=1)
          for g, r0 in chains]

    def over_keys(x, op):
        return op(op(op(x.reshape(8, tkb // 64, 8, cols), axis=1), axis=0), axis=0, keepdims=True)

    def kv_lanes(n):
        g, _ = chains[n]
        return slice(g * HEAD_DIM, (g + 1) * HEAD_DIM)

    def scores(n, c):
        k_blk = k_ref[pl.ds(pl.multiple_of(c * tkb, tkb), tkb), kv_lanes(n)]
        return jnp.dot(k_blk, qs[n], preferred_element_type=F32)

    def softmax_step(n, c, state, raw, bias):
        m_run, l_run, acc = state
        s = bias + raw
        m_new = jnp.maximum(m_run, over_keys(s, jnp.max))
        alpha = jnp.exp2(m_run - m_new)
        p = jnp.exp2(s - m_new)
        l_run = alpha * l_run + over_keys(p, jnp.sum)
        acc = alpha * acc + jnp.dot(vt_ref[0, c, kv_lanes(n), :], p.astype(BF16),
                                    preferred_element_type=F32)
        return m_new, l_run, acc

    def body(c, states):
        bias = jnp.concatenate([bias_ref[0, 0, c]] * chain_heads, axis=1)
        raws = [scores(n, c) for n in range(len(chains))]
        return [softmax_step(n, c, states[n], raws[n], bias) for n in range(len(chains))]

    init = (jnp.full((1, cols), -jnp.inf, F32), jnp.zeros((1, cols), F32),
            jnp.zeros((HEAD_DIM, cols), F32))
    states = lax.fori_loop(0, n_chunks, body, [init] * len(chains))
    for (g, r0), (_, l_run, acc) in zip(chains, states):
        out_t = acc / l_run
        for r in range(chain_heads):
            o_ref[:, head_lanes(g, r0 + r)] = out_t[:, r * tq:(r + 1) * tq].T.astype(o_ref.dtype)


def _dsa_attention(proc, k, v_t, bias, batch, seq_len, *, tq, tkb, groups=2, chain_heads=2):
    nq = seq_len // tq
    gw = groups * DSA_GROUP * HEAD_DIM
    kw = groups * HEAD_DIM
    nkb = seq_len // tkb
    return pl.pallas_call(
        functools.partial(_dsa_attn_kernel, tq=tq, tkb=tkb, groups=groups, chain_heads=chain_heads),
        grid=(batch, nq, DSA_KV_HEADS // groups),
        in_specs=[
            pl.BlockSpec((gw, tq), lambda b, i, g: (g, b * nq + i)),
            pl.BlockSpec((seq_len, kw), lambda b, i, g: (b, g)),
            pl.BlockSpec((1, nkb, kw, tkb), lambda b, i, g: (b, 0, g, 0)),
            pl.BlockSpec((1, 1, nkb, tkb, tq), lambda b, i, g: (b, i, 0, 0, 0)),
        ],
        out_specs=pl.BlockSpec((tq, gw), lambda b, i, g: (b * nq + i, g)),
        out_shape=jax.ShapeDtypeStruct((batch * seq_len, N_HEADS * HEAD_DIM), BF16),
        compiler_params=_params("arbitrary", "arbitrary", "arbitrary"),
        name="dsa_attention",
    )(proc, k, v_t, bias)


def _rope_tables_t(seq_len, dim):
    inv_freq = 1.0 / (ROPE_THETA ** (jnp.arange(0, dim, 2, dtype=F32) / dim))
    ang = jnp.arange(seq_len, dtype=F32)[:, None] * inv_freq[None, :]
    cos, sin = jnp.cos(ang).T, jnp.sin(ang).T
    return jnp.concatenate([cos, cos], axis=0), jnp.concatenate([-sin, sin], axis=0)


def _conv_ffn(x, h, w_up, conv_w, conv_b, w_down, layer, seq_len):
    gated, w_down_bf = _ffn_up(h, w_up, conv_w, conv_b, w_down, layer, seq_len)
    return _matmul(gated, w_down_bf, 0, x.shape[1], tm=512, tn=1024, out_dtype=F32, res=x,
                   name="ffn_down_residual")


def kernel(x, attn_norm_g, ffn_norm_g, sb_w_qkv, sb_w_o, dsa_w_in, dsa_q_norm_g, dsa_k_norm_g,
           dsa_ik_norm_g, dsa_w_o, ffn_w_up, ffn_conv_w, ffn_conv_b, ffn_w_down):
    batch, seq_len, d_model = x.shape
    m = batch * seq_len
    hd = N_HEADS * HEAD_DIM
    x = x.reshape(m, d_model)
    attn_g = attn_norm_g[:, None, :]
    ffn_g = ffn_norm_g[:, None, :]
    conv_b = ffn_conv_b[:, None, :]

    h = _rmsnorm(x, attn_g, 0)
    qkv = _matmul(h, sb_w_qkv, 0, 3 * hd, tm=2048, tn=1024, out_dtype=BF16, scaled_cols=hd,
                  col_scale=1.0 / math.sqrt(HEAD_DIM), name="sb_qkv_proj")
    mixed = _sb_attention(qkv.reshape(batch, seq_len, 3 * hd)).reshape(m, hd)
    x, h = _out_proj_norm(mixed, sb_w_o, x, ffn_g, 0, name="sb_out_residual_norm")
    x = _conv_ffn(x, h, ffn_w_up, ffn_conv_w, conv_b, ffn_w_down, 0, seq_len)

    h = _rmsnorm(x, attn_g, 1)
    w_in = dsa_w_in[0]
    cos_t, sin_t = _rope_tables_t(seq_len, HEAD_DIM)
    proc = _dsa_in_proj(h, w_in, cos_t, sin_t, dsa_q_norm_g[0][:, None], dsa_k_norm_g[0][:, None], seq_len)
    tail_t = _dsa_in_proj_tail(h, jnp.swapaxes(w_in[:, DSA_MAIN_ROWS:], 0, 1))
    tq, tkb = 256, 512
    k, v_t, ki = _dsa_prep(proc, tail_t, cos_t, sin_t, dsa_ik_norm_g[0][:, None], batch, seq_len, tkc=tkb)
    topk = min(TOPK_MAX, seq_len // 4)
    w_t = tail_t[IDX_DIM:]
    bias = _dsa_indexer(proc, ki, w_t, batch, seq_len, topk, tq=tq, tkb=tkb)
    mixed = _dsa_attention(proc, k, v_t, bias, batch, seq_len, tq=tq, tkb=tkb)
    x, h = _out_proj_norm(mixed, dsa_w_o, x, ffn_g, 1, name="dsa_out_residual_norm")
    x = _conv_ffn(x, h, ffn_w_up, ffn_conv_w, conv_b, ffn_w_down, 1, seq_len)
    return x.reshape(batch, seq_len, d_model)
```

```python
import functools
import math

import jax
import jax.numpy as jnp
from jax import lax
from jax.experimental import pallas as pl
from jax.experimental.pallas import tpu as pltpu

N_HEADS = 16
HEAD_DIM = 128
DSA_KV_HEADS = 4
DSA_GROUP = N_HEADS // DSA_KV_HEADS
IDX_HEADS = 16
IDX_DIM = 128
TOPK_MAX = 256
D_FF = 5632
CONV_WIDTH = 3
ROPE_THETA = 10000.0
NORM_EPS = 1e-6

V7X_VMEM_BYTES = 64 * 1024 * 1024
VMEM_LIMIT_BYTES = V7X_VMEM_BYTES - 8 * 1024 * 1024

F32 = jnp.float32
BF16 = jnp.bfloat16
MASK_BIAS = -1e30
F32_LOWEST = float(jnp.finfo(jnp.float32).min)
INT32_MIN = -(2 ** 31)
LOG2E = math.log2(math.e)
DSA_Q_SCALE = LOG2E / math.sqrt(HEAD_DIM)


def _params(*sem):
    return pltpu.CompilerParams(dimension_semantics=sem, vmem_limit_bytes=VMEM_LIMIT_BYTES)


def _rmsnorm_kernel(x_ref, g_ref, o_ref):
    x = x_ref[...]
    ms = jnp.mean(x * x, axis=-1, keepdims=True)
    o_ref[...] = ((x * lax.rsqrt(ms + NORM_EPS)) * g_ref[0]).astype(o_ref.dtype)


def _rmsnorm(x, g_all, layer, *, tm=1024):
    m, d = x.shape
    return pl.pallas_call(
        _rmsnorm_kernel,
        grid=(m // tm,),
        in_specs=[
            pl.BlockSpec((tm, d), lambda i: (i, 0)),
            pl.BlockSpec((1, 1, d), lambda i: (layer, 0, 0)),
        ],
        out_specs=pl.BlockSpec((tm, d), lambda i: (i, 0)),
        out_shape=jax.ShapeDtypeStruct((m, d), BF16),
        compiler_params=_params("arbitrary"),
        name="rmsnorm",
    )(x, g_all)


def _matmul_kernel(*refs, has_res, cast_w, scaled_blocks, block_scale):
    a_ref, w_ref, *rest = refs
    r_ref = rest.pop(0) if has_res else None
    o_ref = rest.pop(0)
    if cast_w:
        (w_bf,) = rest

        @pl.when(pl.program_id(1) == 0)
        def _():
            w_bf[...] = w_ref[...].astype(BF16)

        w = w_bf[...]
    else:
        w = w_ref[...]
    out = jnp.dot(a_ref[...], w, preferred_element_type=F32)
    if scaled_blocks:
        out = out * jnp.where(pl.program_id(0) < scaled_blocks, block_scale, 1.0)
    if has_res:
        out = r_ref[...] + out
    o_ref[...] = out.astype(o_ref.dtype)


def _matmul(a, w_all, layer, n_out, *, tm, tn, out_dtype, res=None, scaled_cols=0, col_scale=1.0, name):
    m, k_dim = a.shape
    cast_w = w_all.dtype != BF16
    assert scaled_cols % tn == 0
    in_specs = [
        pl.BlockSpec((tm, k_dim), lambda j, i: (i, 0)),
        pl.BlockSpec((None, k_dim, tn), lambda j, i: (layer, 0, j)),
    ]
    args = [a, w_all]
    if res is not None:
        in_specs.append(pl.BlockSpec((tm, tn), lambda j, i: (i, j)))
        args.append(res)
    return pl.pallas_call(
        functools.partial(_matmul_kernel, has_res=res is not None, cast_w=cast_w,
                          scaled_blocks=scaled_cols // tn, block_scale=col_scale),
        grid=(n_out // tn, m // tm),
        in_specs=in_specs,
        out_specs=pl.BlockSpec((tm, tn), lambda j, i: (i, j)),
        out_shape=jax.ShapeDtypeStruct((m, n_out), out_dtype),
        scratch_shapes=[pltpu.VMEM((k_dim, tn), BF16)] if cast_w else [],
        compiler_params=_params("arbitrary", "arbitrary"),
        name=name,
    )(*args)


def _out_proj_norm_kernel(a_ref, w_hbm, r_ref, g_ref, x_ref, h_ref, w_f32, w_bf, sems):
    i = pl.program_id(0)
    n_chunks, _, cw = w_f32.shape

    def norm_store(x):
        ms = jnp.mean(x * x, axis=-1, keepdims=True)
        h_ref[...] = ((x * lax.rsqrt(ms + NORM_EPS)) * g_ref[0]).astype(h_ref.dtype)

    @pl.when(i == 0)
    def _():
        copies = [pltpu.make_async_copy(w_hbm.at[0, :, c * cw:(c + 1) * cw], w_f32.at[c], sems.at[c])
                  for c in range(n_chunks)]
        for cp in copies:
            cp.start()
        for c, cp in enumerate(copies):
            cols = slice(c * cw, (c + 1) * cw)
            cp.wait()
            w_bf[:, cols] = w_f32[c].astype(BF16)
            x_ref[:, cols] = r_ref[:, cols] + jnp.dot(a_ref[...], w_bf[:, cols], preferred_element_type=F32)
        norm_store(x_ref[...])

    @pl.when(i > 0)
    def _():
        x = r_ref[...] + jnp.dot(a_ref[...], w_bf[...], preferred_element_type=F32)
        x_ref[...] = x
        norm_store(x)


W_FETCH_CHUNKS = 4


def _out_proj_norm(a, w_all, res, g_all, layer, *, tm=512, name):
    m, k_dim = a.shape
    d = w_all.shape[2]
    return pl.pallas_call(
        _out_proj_norm_kernel,
        grid=(m // tm,),
        in_specs=[
            pl.BlockSpec((tm, k_dim), lambda i: (i, 0)),
            pl.BlockSpec(memory_space=pl.ANY),
            pl.BlockSpec((tm, d), lambda i: (i, 0)),
            pl.BlockSpec((1, 1, d), lambda i: (layer, 0, 0)),
        ],
        out_specs=[pl.BlockSpec((tm, d), lambda i: (i, 0)), pl.BlockSpec((tm, d), lambda i: (i, 0))],
        out_shape=[jax.ShapeDtypeStruct((m, d), F32), jax.ShapeDtypeStruct((m, d), BF16)],
        scratch_shapes=[
            pltpu.VMEM((W_FETCH_CHUNKS, k_dim, d // W_FETCH_CHUNKS), F32),
            pltpu.VMEM((k_dim, d), BF16),
            pltpu.SemaphoreType.DMA((W_FETCH_CHUNKS,)),
        ],
        compiler_params=_params("arbitrary"),
        name=name,
    )(a, w_all, res, g_all)


def _sb_tile(q, k_blk, v_blk, upper, carry, causal):
    z = lax.dot_general(q, k_blk, (((1,), (1,)), ((), ())), preferred_element_type=F32)
    sp = jnp.maximum(z, 0.0) + jnp.log(1.0 + jnp.exp2(jnp.abs(z) * (-LOG2E)))
    if causal is not None:
        sp = jnp.where(causal, sp, 0.0)
    tail = jnp.dot(sp.astype(BF16), upper, preferred_element_type=F32)
    a = jnp.exp(z - (tail + carry))
    if causal is not None:
        a = jnp.where(causal, a, 0.0)
    out = jnp.dot(a.astype(BF16), v_blk, preferred_element_type=F32)
    return out, carry + jnp.sum(sp, axis=1, keepdims=True)


def _sb_attn_kernel(q_ref, k_ref, v_ref, o_ref, *, tq, tk, heads):
    i = pl.program_id(2)
    per_q = tq // tk
    r2 = lax.broadcasted_iota(jnp.int32, (tk, tk), 0)
    c2 = lax.broadcasted_iota(jnp.int32, (tk, tk), 1)
    upper = jnp.where(r2 >= c2, 1.0, 0.0).astype(BF16)
    rows = lax.broadcasted_iota(jnp.int32, (tq, tk), 0)
    cols = lax.broadcasted_iota(jnp.int32, (tq, tk), 1)
    lanes = [slice(h * HEAD_DIM, (h + 1) * HEAD_DIM) for h in range(heads)]

    def block(j, carries, causal, first_row=0):
        start = pl.multiple_of(j * tk, tk)
        outs = [_sb_tile(q_ref[0, first_row:, lanes[h]], k_ref[0, pl.ds(start, tk), lanes[h]],
                         v_ref[0, pl.ds(start, tk), lanes[h]], upper, carries[h], causal)
                for h in range(heads)]
        return [o for o, _ in outs], [c for _, c in outs]

    carries = [jnp.zeros((tq, 1), F32)] * heads
    accs = [jnp.zeros((tq, HEAD_DIM), F32)] * heads
    for d in reversed(range(per_q)):
        r0 = d * tk
        outs, low = block(i * per_q + d, [c[r0:] for c in carries], (cols + r0 < rows)[r0:], r0)
        if r0:
            carries = [jnp.concatenate([c[:r0], lo], axis=0) for c, lo in zip(carries, low)]
            accs = [jnp.concatenate([acc[:r0], acc[r0:] + o], axis=0) for acc, o in zip(accs, outs)]
        else:
            carries = low
            accs = [acc + o for acc, o in zip(accs, outs)]

    def body(step, state):
        accs, carries = state
        for d in range(per_q):
            outs, carries = block((i - step) * per_q - 1 - d, carries, None)
            accs = [acc + o for acc, o in zip(accs, outs)]
        return accs, carries

    accs, _ = lax.fori_loop(0, i, body, (accs, carries))
    for h in range(heads):
        o_ref[0, :, lanes[h]] = accs[h].astype(o_ref.dtype)


def _sb_attention(qkv, *, tq=512, tk=256, heads=4):
    b, s, _ = qkv.shape
    groups = N_HEADS // heads
    width = heads * HEAD_DIM
    return pl.pallas_call(
        functools.partial(_sb_attn_kernel, tq=tq, tk=tk, heads=heads),
        grid=(b, groups, s // tq),
        in_specs=[
            pl.BlockSpec((1, tq, width), lambda bi, g, i: (bi, i, g)),
            pl.BlockSpec((1, s, width), lambda bi, g, i: (bi, 0, groups + g)),
            pl.BlockSpec((1, s, width), lambda bi, g, i: (bi, 0, 2 * groups + g)),
        ],
        out_specs=pl.BlockSpec((1, tq, width), lambda bi, g, i: (bi, i, g)),
        out_shape=jax.ShapeDtypeStruct((b, s, N_HEADS * HEAD_DIM), BF16),
        compiler_params=_params("arbitrary", "arbitrary", "arbitrary"),
        name="sb_attention",
    )(qkv, qkv, qkv)


HISTORY_ROWS = 8


def _causal_conv(u_ref, cw, cb):
    r0 = HISTORY_ROWS
    rows = u_ref.shape[0] - r0
    return (cb + cw[2:3] * u_ref[r0:r0 + rows] + cw[1:2] * u_ref[r0 - 1:r0 - 1 + rows]
            + cw[0:1] * u_ref[r0 - 2:r0 - 2 + rows])


def _ffn_up_kernel(h_ref, wg_ref, wv_ref, cwg_ref, cwv_ref, cbg_ref, cbv_ref, wd_ref, o_ref, wd_bf_ref,
                   wg_bf, wv_bf, hist_g, hist_v, *stage, tiles_per_seq, chunk_rows, chunk_cols):
    i = pl.program_id(1)
    tm, tn = o_ref.shape
    r0 = HISTORY_ROWS

    wd_bf_ref[...] = wd_ref[...].astype(BF16)

    @pl.when(i == 0)
    def _():
        wg_bf[...] = wg_ref[...].astype(BF16)
        wv_bf[...] = wv_ref[...].astype(BF16)

    @pl.when(i % tiles_per_seq == 0)
    def _():
        hist_g[...] = jnp.zeros_like(hist_g)
        hist_v[...] = jnp.zeros_like(hist_v)

    piece = 0
    for c0 in range(0, tn, chunk_cols):
        cols = slice(c0, c0 + chunk_cols)
        prev_g = hist_g[:, cols]
        prev_v = hist_v[:, cols]
        for t0 in range(0, tm, chunk_rows):
            ug_s, uv_s = stage[2 * (piece % 2)], stage[2 * (piece % 2) + 1]
            piece += 1
            h = h_ref[t0:t0 + chunk_rows, :]
            ug_s[0:r0] = prev_g
            uv_s[0:r0] = prev_v
            ug_s[r0:] = jnp.dot(h, wg_bf[:, cols], preferred_element_type=F32)
            uv_s[r0:] = jnp.dot(h, wv_bf[:, cols], preferred_element_type=F32)
            gate = _causal_conv(ug_s, cwg_ref[:, cols], cbg_ref[0, :, cols])
            val = _causal_conv(uv_s, cwv_ref[:, cols], cbv_ref[0, :, cols])
            o_ref[t0:t0 + chunk_rows, cols] = (jax.nn.silu(gate) * val).astype(o_ref.dtype)
            prev_g = ug_s[chunk_rows:]
            prev_v = uv_s[chunk_rows:]
        hist_g[:, cols] = prev_g
        hist_v[:, cols] = prev_v


def _ffn_up(h, w_up_all, conv_w_all, conv_b_all, w_down_all, layer, seq_len, *, tm=2048, tn=512,
            chunk_rows=512, chunk_cols=256):
    m, d = h.shape
    nj = D_FF // tn
    ni = m // tm
    slab, rem = divmod(w_down_all.shape[1], nj * ni)
    assert rem == 0 and slab % 16 == 0
    d_out = w_down_all.shape[2]
    return pl.pallas_call(
        functools.partial(_ffn_up_kernel, tiles_per_seq=seq_len // tm, chunk_rows=chunk_rows,
                          chunk_cols=chunk_cols),
        grid=(nj, m // tm),
        in_specs=[
            pl.BlockSpec((tm, d), lambda j, i: (i, 0)),
            pl.BlockSpec((None, d, tn), lambda j, i: (layer, 0, j)),
            pl.BlockSpec((None, d, tn), lambda j, i: (layer, 0, nj + j)),
            pl.BlockSpec((None, CONV_WIDTH, tn), lambda j, i: (layer, 0, j)),
            pl.BlockSpec((None, CONV_WIDTH, tn), lambda j, i: (layer, 0, nj + j)),
            pl.BlockSpec((1, 1, tn), lambda j, i: (layer, 0, j)),
            pl.BlockSpec((1, 1, tn), lambda j, i: (layer, 0, nj + j)),
            pl.BlockSpec((None, slab, d_out), lambda j, i: (layer, j * ni + i, 0)),
        ],
        out_specs=[
            pl.BlockSpec((tm, tn), lambda j, i: (i, j)),
            pl.BlockSpec((None, slab, d_out), lambda j, i: (0, j * ni + i, 0)),
        ],
        out_shape=[
            jax.ShapeDtypeStruct((m, D_FF), BF16),
            jax.ShapeDtypeStruct((1,) + w_down_all.shape[1:], BF16),
        ],
        scratch_shapes=[
            pltpu.VMEM((d, tn), BF16),
            pltpu.VMEM((d, tn), BF16),
            pltpu.VMEM((HISTORY_ROWS, tn), F32),
            pltpu.VMEM((HISTORY_ROWS, tn), F32),
        ] + [pltpu.VMEM((HISTORY_ROWS + chunk_rows, chunk_cols), F32)] * 4,
        compiler_params=_params("arbitrary", "arbitrary"),
        name="ffn_up_conv_gate",
    )(h, w_up_all, w_up_all, conv_w_all, conv_w_all, conv_b_all, conv_b_all, w_down_all)


def _head_norm_rope_t(x, gain, cos_t, sin_t):
    if gain is not None:
        ms = jnp.mean(x * x, axis=0, keepdims=True)
        x = (x * lax.rsqrt(ms + NORM_EPS)) * gain
    return x * cos_t + pltpu.roll(x, HEAD_DIM // 2, 0) * sin_t


DSA_K_ROW = N_HEADS * HEAD_DIM
DSA_V_ROW = DSA_K_ROW + DSA_KV_HEADS * HEAD_DIM
DSA_QI_ROW = DSA_V_ROW + DSA_KV_HEADS * HEAD_DIM
DSA_MAIN_ROWS = DSA_QI_ROW + IDX_HEADS * IDX_DIM


def _dsa_in_kernel(h_ref, w_ref, cos_ref, sin_ref, qg_ref, kg_ref, o_ref, w_bf, *, tn):
    j = pl.program_id(0)

    @pl.when(pl.program_id(1) == 0)
    def _():
        w_bf[...] = w_ref[...].astype(BF16)

    def head_kind(row):
        return "q" if row < DSA_K_ROW else "k" if row < DSA_V_ROW else "v" if row < DSA_QI_ROW else "qi"

    def finish(block):
        out = lax.dot_general(w_bf[...], h_ref[...], (((1,), (1,)), ((), ())), preferred_element_type=F32)
        cos, sin = cos_ref[...], sin_ref[...]
        for n in range(tn // HEAD_DIM):
            sl = slice(n * HEAD_DIM, (n + 1) * HEAD_DIM)
            kind = head_kind(block * tn + n * HEAD_DIM)
            x = out[sl, :]
            if kind == "q":
                x = _head_norm_rope_t(x, qg_ref[...] * DSA_Q_SCALE, cos, sin)
            elif kind == "k":
                x = _head_norm_rope_t(x, kg_ref[...], cos, sin)
            elif kind == "qi":
                x = _head_norm_rope_t(x, None, cos, sin)
            o_ref[sl, :] = x.astype(BF16)

    for block in range(DSA_MAIN_ROWS // tn):
        pl.when(j == block)(functools.partial(finish, block))


def _dsa_in_proj(h, w_in_t, cos_t, sin_t, q_gain, k_gain, seq_len, *, tm=1024, tn=1024):
    m, d = h.shape
    tiles_per_seq = seq_len // tm
    gain_spec = pl.BlockSpec((HEAD_DIM, 1), lambda j, i: (0, 0))
    tab_spec = pl.BlockSpec((HEAD_DIM, tm), lambda j, i: (0, i % tiles_per_seq))
    return pl.pallas_call(
        functools.partial(_dsa_in_kernel, tn=tn),
        grid=(DSA_MAIN_ROWS // tn, m // tm),
        in_specs=[
            pl.BlockSpec((tm, d), lambda j, i: (i, 0)),
            pl.BlockSpec((tn, d), lambda j, i: (j, 0)),
            tab_spec, tab_spec, gain_spec, gain_spec,
        ],
        out_specs=pl.BlockSpec((tn, tm), lambda j, i: (j, i)),
        out_shape=jax.ShapeDtypeStruct((DSA_MAIN_ROWS, m), BF16),
        scratch_shapes=[pltpu.VMEM((tn, d), BF16)],
        compiler_params=_params("arbitrary", "arbitrary"),
        name="dsa_in_proj",
    )(h, w_in_t, cos_t, sin_t, q_gain, k_gain)


def _dsa_tail_kernel(h_ref, w_ref, o_ref):
    o_ref[...] = lax.dot_general(w_ref[...].astype(BF16), h_ref[...], (((1,), (1,)), ((), ())),
                                 preferred_element_type=F32)


def _dsa_in_proj_tail(h, w_tail_t, *, tm=2048):
    m, d = h.shape
    rows = w_tail_t.shape[0]
    return pl.pallas_call(
        _dsa_tail_kernel,
        grid=(m // tm,),
        in_specs=[pl.BlockSpec((tm, d), lambda i: (i, 0)), pl.BlockSpec((rows, d), lambda i: (0, 0))],
        out_specs=pl.BlockSpec((rows, tm), lambda i: (0, i)),
        out_shape=jax.ShapeDtypeStruct((rows, m), F32),
        compiler_params=_params("arbitrary"),
        name="dsa_in_proj_tail",
    )(h, w_tail_t)


def _dsa_prep_kernel(k_ref, v_ref, t_ref, cos_ref, sin_ref, ig_ref, ko_ref, vt_ref, ki_ref):
    ko_ref[...] = k_ref[...].astype(F32).T.astype(BF16)
    vt_ref[...] = v_ref[...]
    ki_t = _head_norm_rope_t(t_ref[0:IDX_DIM, :], ig_ref[...], cos_ref[...], sin_ref[...])
    ki_ref[...] = ki_t.T.astype(BF16)


def _dsa_prep(proc_t, tail_t, cos_t, sin_t, ik_gain, batch, seq_len, *, tm=512, tkc=512):
    m = proc_t.shape[1]
    tiles_per_seq = seq_len // tm
    tiles_per_chunk = tkc // tm
    hk = DSA_KV_HEADS * HEAD_DIM

    def v_block(i):
        pos_tile = i % tiles_per_seq
        return i // tiles_per_seq, pos_tile // tiles_per_chunk, 0, pos_tile % tiles_per_chunk

    tab_spec = pl.BlockSpec((HEAD_DIM, tm), lambda i: (0, i % tiles_per_seq))
    return pl.pallas_call(
        _dsa_prep_kernel,
        grid=(m // tm,),
        in_specs=[
            pl.BlockSpec((hk, tm), lambda i: (DSA_K_ROW // hk, i)),
            pl.BlockSpec((hk, tm), lambda i: (DSA_V_ROW // hk, i)),
            pl.BlockSpec((tail_t.shape[0], tm), lambda i: (0, i)),
            tab_spec, tab_spec,
            pl.BlockSpec((HEAD_DIM, 1), lambda i: (0, 0)),
        ],
        out_specs=[
            pl.BlockSpec((tm, hk), lambda i: (i, 0)),
            pl.BlockSpec((None, None, hk, tm), v_block),
            pl.BlockSpec((tm, IDX_DIM), lambda i: (i, 0)),
        ],
        out_shape=[
            jax.ShapeDtypeStruct((m, hk), BF16),
            jax.ShapeDtypeStruct((batch, seq_len // tkc, hk, tkc), BF16),
            jax.ShapeDtypeStruct((m, IDX_DIM), BF16),
        ],
        compiler_params=_params("arbitrary"),
        name="dsa_kv_layout_kidx",
    )(proc_t, proc_t, tail_t, cos_t, sin_t, ik_gain)


def _key_to_f32(key):
    bits = jnp.where(key < 0, key ^ INT32_MIN, ~key)
    return lax.bitcast_convert_type(bits, F32)


def _key16_to_bf16(key):
    bits = jnp.where(key < 0, key ^ INT32_MIN, ~key) & -65536
    return lax.bitcast_convert_type(bits, F32).astype(BF16)


def _indexer_kernel(qi_lo_ref, qi_hi_ref, ki_ref, w_ref, bias_ref, sc_ref, hi_ref, *,
                    tq, tkc, tkb, nkc, topk):
    i = pl.program_id(1)
    t0 = i * tq
    w = w_ref[...] * (1.0 / math.sqrt(IDX_DIM) / math.sqrt(IDX_HEADS))
    key_pos = lax.broadcasted_iota(jnp.int32, (tkc, tq), 0)
    qry_pos = lax.broadcasted_iota(jnp.int32, (tkc, tq), 1) + t0

    def chunk_in_range(c):
        return c * tkc < t0 + tq

    for c in range(nkc):

        @pl.when(chunk_in_range(c))
        def _():
            k_blk = ki_ref[c * tkc:(c + 1) * tkc, :]
            acc = jnp.zeros((tkc, tq), F32)
            for h in range(IDX_HEADS):
                qi_ref, hh = (qi_lo_ref, h) if h < IDX_HEADS // 2 else (qi_hi_ref, h - IDX_HEADS // 2)
                logit = jnp.dot(k_blk, qi_ref[hh * IDX_DIM:(hh + 1) * IDX_DIM, :],
                                preferred_element_type=F32)
                acc = acc + jnp.maximum(logit, 0.0) * w[h:h + 1, :]
            score = jnp.where(key_pos + c * tkc <= qry_pos, acc, -jnp.inf)
            sc_ref[c] = score
            hi_ref[c] = score.astype(BF16)

        @pl.when(jnp.logical_not(chunk_in_range(c)))
        def _():
            sc_ref[c] = jnp.full((tkc, tq), -jnp.inf, F32)
            hi_ref[c] = jnp.full((tkc, tq), -jnp.inf, BF16)

    def sublane_partial(hit):
        return jnp.sum(jnp.sum(hit.reshape(8, tkc // 64, 8, tq), axis=1), axis=0)

    n_valid = (t0 + tq + tkc - 1) // tkc

    def count_ge(thr):
        def add_chunk(c, acc):
            return acc + sublane_partial(jnp.where(sc_ref[c] >= thr, 1.0, 0.0))

        acc = lax.fori_loop(0, n_valid, add_chunk, jnp.zeros((8, tq), F32))
        return jnp.sum(acc, axis=0, keepdims=True)

    def count_hi_ge(thr_hi):
        one, zero = jnp.ones((), BF16), jnp.zeros((), BF16)

        def add_chunk(c, acc):
            hit = jnp.where(hi_ref[c] >= thr_hi, one, zero)
            parts = [hit[r:r + 16] for r in range(0, tkc, 16)]
            while len(parts) > 1:
                parts = [a + b for a, b in zip(parts[0::2], parts[1::2])]
            return acc + parts[0]

        acc = lax.fori_loop(0, n_valid, add_chunk, jnp.zeros((16, tq), BF16))
        return jnp.sum(acc.astype(F32), axis=0, keepdims=True)

    def store_bias(c, bias):
        for part in range(tkc // tkb):
            bias_ref[0, 0, c * (tkc // tkb) + part] = bias[part * tkb:(part + 1) * tkb]

    def write_bias(select_fn):
        for c in range(nkc):
            store_bias(c, jnp.where(select_fn(sc_ref[c]), 0.0, MASK_BIAS))

    @pl.when(t0 + tq <= topk)
    def _():
        write_bias(lambda sc: sc >= F32_LOWEST)

    @pl.when(t0 + tq > topk)
    def _():
        def coarse_step(step, key):
            trial = key | lax.shift_left(jnp.int32(1), 31 - step)
            keep = count_hi_ge(_key16_to_bf16(trial)) >= topk
            return jnp.where(keep, trial, key)

        coarse = lax.fori_loop(0, 16, coarse_step, jnp.zeros((1, tq), jnp.int32))
        coarse_key = coarse | jnp.where(coarse < 0, 0, 0xFFFF)
        base = coarse_key - 0x8000

        def fine_step(step, offset):
            trial = offset | lax.shift_left(jnp.int32(1), 16 - step)
            keep = count_ge(_key_to_f32(base + trial)) >= topk
            return jnp.where(keep, trial, offset)

        key = base + lax.fori_loop(0, 17, fine_step, jnp.zeros((1, tq), jnp.int32))
        few = lax.broadcasted_iota(jnp.int32, (1, tq), 1) + t0 < topk - 1
        thr = jnp.where(few, F32_LOWEST, _key_to_f32(key))
        write_bias(lambda sc: sc >= thr)

        n_ge = count_ge(thr)

        @pl.when(jnp.max(jnp.where(few, 0.0, n_ge)) > topk)
        def _():
            n_gt = sum(jnp.sum(jnp.where(sc_ref[c] > thr, 1.0, 0.0), axis=0, keepdims=True)
                       for c in range(nkc))
            room = jnp.where(few, float(tkc * nkc), topk - n_gt)
            r2 = lax.broadcasted_iota(jnp.int32, (tkc, tkc), 0)
            c2 = lax.broadcasted_iota(jnp.int32, (tkc, tkc), 1)
            before = jnp.where(c2 < r2, 1.0, 0.0).astype(BF16)
            seen = jnp.zeros((1, tq), F32)
            for c in range(nkc):
                sc = sc_ref[c]
                tied = jnp.where(sc == thr, 1.0, 0.0)
                rank = seen + jnp.dot(before, tied.astype(BF16), preferred_element_type=F32)
                take = jnp.where(sc > thr, 1.0, jnp.where(rank < room, tied, 0.0))
                store_bias(c, jnp.where(take > 0.0, 0.0, MASK_BIAS))
                seen = seen + jnp.sum(tied, axis=0, keepdims=True)


def _dsa_indexer(proc, ki, w_t, batch, seq_len, topk, *, tq, tkb, tkc=512):
    half = IDX_HEADS * IDX_DIM // 2
    qi_block = DSA_QI_ROW // half
    nq = seq_len // tq
    nkc = seq_len // tkc
    nkb = seq_len // tkb
    return pl.pallas_call(
        functools.partial(_indexer_kernel, tq=tq, tkc=tkc, tkb=tkb, nkc=nkc, topk=topk),
        grid=(batch, nq),
        in_specs=[
            pl.BlockSpec((half, tq), lambda b, i: (qi_block, b * nq + i)),
            pl.BlockSpec((half, tq), lambda b, i: (qi_block + 1, b * nq + i)),
            pl.BlockSpec((seq_len, IDX_DIM), lambda b, i: (b, 0)),
            pl.BlockSpec((IDX_HEADS, tq), lambda b, i: (0, b * nq + i)),
        ],
        out_specs=pl.BlockSpec((1, 1, nkb, tkb, tq), lambda b, i: (b, i, 0, 0, 0)),
        out_shape=jax.ShapeDtypeStruct((batch, nq, nkb, tkb, tq), F32),
        scratch_shapes=[pltpu.VMEM((nkc, tkc, tq), F32), pltpu.VMEM((nkc, tkc, tq), BF16)],
        compiler_params=_params("arbitrary", "arbitrary"),
        name="dsa_indexer_topk",
    )(proc, proc, ki, w_t)


def _dsa_attn_kernel(q_ref, k_ref, vt_ref, bias_ref, o_ref, *, tq, tkb, groups, chain_heads):
    i = pl.program_id(1)
    n_chunks = ((i + 1) * tq + tkb - 1) // tkb
    cols = chain_heads * tq
    chains = [(g, r0) for g in range(groups) for r0 in range(0, DSA_GROUP, chain_heads)]

    def head_lanes(g, r):
        h = g * DSA_GROUP + r
        return slice(h * HEAD_DIM, (h + 1) * HEAD_DIM)

    qs = [jnp.concatenate([q_ref[head_lanes(g, r0 + r), :] for r in range(chain_heads)], axis=1)
          for g, r0 in chains]

    def over_keys(x, op):
        return op(op(op(x.reshape(8, tkb // 64, 8, cols), axis=1), axis=0), axis=0, keepdims=True)

    def kv_lanes(n):
        g, _ = chains[n]
        return slice(g * HEAD_DIM, (g + 1) * HEAD_DIM)

    def scores(n, c):
        k_blk = k_ref[pl.ds(pl.multiple_of(c * tkb, tkb), tkb), kv_lanes(n)]
        return jnp.dot(k_blk, qs[n], preferred_element_type=F32)

    def softmax_step(n, c, state, raw, bias):
        m_run, l_run, acc = state
        s = bias + raw
        m_new = jnp.maximum(m_run, over_keys(s, jnp.max))
        alpha = jnp.exp2(m_run - m_new)
        p = jnp.exp2(s - m_new)
        l_run = alpha * l_run + over_keys(p, jnp.sum)
        acc = alpha * acc + jnp.dot(vt_ref[0, c, kv_lanes(n), :], p.astype(BF16),
                                    preferred_element_type=F32)
        return m_new, l_run, acc

    def body(c, states):
        bias = jnp.concatenate([bias_ref[0, 0, c]] * chain_heads, axis=1)
        raws = [scores(n, c) for n in range(len(chains))]
        return [softmax_step(n, c, states[n], raws[n], bias) for n in range(len(chains))]

    init = (jnp.full((1, cols), -jnp.inf, F32), jnp.zeros((1, cols), F32),
            jnp.zeros((HEAD_DIM, cols), F32))
    states = lax.fori_loop(0, n_chunks, body, [init] * len(chains))
    for (g, r0), (_, l_run, acc) in zip(chains, states):
        out_t = acc / l_run
        for r in range(chain_heads):
            o_ref[:, head_lanes(g, r0 + r)] = out_t[:, r * tq:(r + 1) * tq].T.astype(o_ref.dtype)


def _dsa_attention(proc, k, v_t, bias, batch, seq_len, *, tq, tkb, groups=2, chain_heads=2):
    nq = seq_len // tq
    gw = groups * DSA_GROUP * HEAD_DIM
    kw = groups * HEAD_DIM
    nkb = seq_len // tkb
    return pl.pallas_call(
        functools.partial(_dsa_attn_kernel, tq=tq, tkb=tkb, groups=groups, chain_heads=chain_heads),
        grid=(batch, nq, DSA_KV_HEADS // groups),
        in_specs=[
            pl.BlockSpec((gw, tq), lambda b, i, g: (g, b * nq + i)),
            pl.BlockSpec((seq_len, kw), lambda b, i, g: (b, g)),
            pl.BlockSpec((1, nkb, kw, tkb), lambda b, i, g: (b, 0, g, 0)),
            pl.BlockSpec((1, 1, nkb, tkb, tq), lambda b, i, g: (b, i, 0, 0, 0)),
        ],
        out_specs=pl.BlockSpec((tq, gw), lambda b, i, g: (b * nq + i, g)),
        out_shape=jax.ShapeDtypeStruct((batch * seq_len, N_HEADS * HEAD_DIM), BF16),
        compiler_params=_params("arbitrary", "arbitrary", "arbitrary"),
        name="dsa_attention",
    )(proc, k, v_t, bias)


def _rope_tables_t(seq_len, dim):
    inv_freq = 1.0 / (ROPE_THETA ** (jnp.arange(0, dim, 2, dtype=F32) / dim))
    ang = jnp.arange(seq_len, dtype=F32)[:, None] * inv_freq[None, :]
    cos, sin = jnp.cos(ang).T, jnp.sin(ang).T
    return jnp.concatenate([cos, cos], axis=0), jnp.concatenate([-sin, sin], axis=0)


def _conv_ffn(x, h, w_up, conv_w, conv_b, w_down, layer, seq_len):
    gated, w_down_bf = _ffn_up(h, w_up, conv_w, conv_b, w_down, layer, seq_len)
    return _matmul(gated, w_down_bf, 0, x.shape[1], tm=512, tn=1024, out_dtype=F32, res=x,
                   name="ffn_down_residual")


def kernel(x, attn_norm_g, ffn_norm_g, sb_w_qkv, sb_w_o, dsa_w_in, dsa_q_norm_g, dsa_k_norm_g,
           dsa_ik_norm_g, dsa_w_o, ffn_w_up, ffn_conv_w, ffn_conv_b, ffn_w_down):
    batch, seq_len, d_model = x.shape
    m = batch * seq_len
    hd = N_HEADS * HEAD_DIM
    x = x.reshape(m, d_model)
    attn_g = attn_norm_g[:, None, :]
    ffn_g = ffn_norm_g[:, None, :]
    conv_b = ffn_conv_b[:, None, :]

    h = _rmsnorm(x, attn_g, 0)
    qkv = _matmul(h, sb_w_qkv, 0, 3 * hd, tm=2048, tn=1024, out_dtype=BF16, scaled_cols=hd,
                  col_scale=1.0 / math.sqrt(HEAD_DIM), name="sb_qkv_proj")
    mixed = _sb_attention(qkv.reshape(batch, seq_len, 3 * hd)).reshape(m, hd)
    x, h = _out_proj_norm(mixed, sb_w_o, x, ffn_g, 0, name="sb_out_residual_norm")
    x = _conv_ffn(x, h, ffn_w_up, ffn_conv_w, conv_b, ffn_w_down, 0, seq_len)

    h = _rmsnorm(x, attn_g, 1)
    w_in_t = jnp.swapaxes(dsa_w_in[0], 0, 1)
    cos_t, sin_t = _rope_tables_t(seq_len, HEAD_DIM)
    proc = _dsa_in_proj(h, w_in_t, cos_t, sin_t, dsa_q_norm_g[0][:, None], dsa_k_norm_g[0][:, None], seq_len)
    tail_t = _dsa_in_proj_tail(h, w_in_t[DSA_MAIN_ROWS:])
    tq, tkb = 256, 512
    k, v_t, ki = _dsa_prep(proc, tail_t, cos_t, sin_t, dsa_ik_norm_g[0][:, None], batch, seq_len, tkc=tkb)
    topk = min(TOPK_MAX, seq_len // 4)
    w_t = tail_t[IDX_DIM:]
    bias = _dsa_indexer(proc, ki, w_t, batch, seq_len, topk, tq=tq, tkb=tkb)
    mixed = _dsa_attention(proc, k, v_t, bias, batch, seq_len, tq=tq, tkb=tkb)
    x, h = _out_proj_norm(mixed, dsa_w_o, x, ffn_g, 1, name="dsa_out_residual_norm")
    x = _conv_ffn(x, h, ffn_w_up, ffn_conv_w, conv_b, ffn_w_down, 1, seq_len)
    return x.reshape(batch, seq_len, d_model)
```
